```python
import jax, jax.numpy as jnp
from jax import lax
import numpy as np

D_MODEL = 1024
BATCH = 8
SEQ = 2048
DEPTH = 2
DEC_BATCH = 128
DEC_SEQ = 8
PAST_LEN = 16384
PAGE_SIZE = 128

N_META = 16
RW_HEADS = 8
RW_HD = 64
D_RW = RW_HEADS * RW_HD
RW_W_LORA = 64
RW_A_LORA = 64
RW_G_LORA = 128
RW_SHIFT_W = 3 * D_RW + RW_W_LORA + RW_A_LORA + RW_G_LORA
RW_GN_EPS = 64e-5
ML_HEADS = 4
ML_HD = 128
D_ML = ML_HEADS * ML_HD
CONV_W = 4
ML_CHUNK = 64
ML_IN_W = 4 * D_ML + 2 * ML_HEADS
ML_GN_EPS = 1e-5
N_IN = RW_SHIFT_W + ML_IN_W + 2 * D_MODEL
D_FF = 4 * D_MODEL
RMS_EPS = 1e-6

kernel_name = "rwkv7_mlstm_gated_hybrid_step"

F32 = jnp.float32


def _split(a, sizes):
    idx = [int(i) for i in np.cumsum(sizes)[:-1]]
    return jnp.split(a, idx, axis=-1)


def _rms_norm(x, g):
    xf = x.astype(F32)
    y = xf * lax.rsqrt(jnp.mean(xf * xf, axis=-1, keepdims=True) + RMS_EPS)
    return (y * g.astype(F32)).astype(x.dtype)


def _head_norm(x, g, eps):
    mu = jnp.mean(x, axis=-1, keepdims=True)
    xc = x - mu
    var = jnp.mean(xc * xc, axis=-1, keepdims=True)
    return xc * lax.rsqrt(var + eps) * g.astype(F32)


def _rwkv7_step(S, inp):
    r, w, k, v, kk, a = inp
    sa = jnp.einsum('bhvk,bhk->bhv', S, -kk)
    S = S * w[:, :, None, :] + sa[..., None] * (kk * a)[:, :, None, :] + v[..., None] * k[:, :, None, :]
    return S, jnp.einsum('bhvk,bhk->bhv', S, r)


def _rwkv7_branch(z, shift_buf, S0, mu, w0, w_up, a0, a_up, g_up, k_k, k_a, r_k, gn_g, gn_b):
    B, T, _ = z.shape
    z = z.astype(F32)
    prev = jnp.concatenate([shift_buf[:, None].astype(F32), z[:, :-1]], axis=1)
    zs = z + (prev - z) * mu.astype(F32)
    r, k, v, wl, al, gl = _split(zs, [D_RW, D_RW, D_RW, RW_W_LORA, RW_A_LORA, RW_G_LORA])
    w = -jax.nn.softplus(-(w0.astype(F32) + jnp.tanh(wl) @ w_up.astype(F32))) - 0.5
    decay = jnp.exp(-jnp.exp(w))
    a = jax.nn.sigmoid(a0.astype(F32) + al @ a_up.astype(F32))
    g = jax.nn.sigmoid(gl) @ g_up.astype(F32)
    heads = lambda t: t.reshape(B, T, RW_HEADS, RW_HD)
    kk = heads(k * k_k.astype(F32))
    kk = kk * lax.rsqrt(jnp.maximum(jnp.sum(kk * kk, axis=-1, keepdims=True), 1e-24))
    k = k * (1.0 + (a - 1.0) * k_a.astype(F32))
    r, decay, k, v, a = heads(r), heads(decay), heads(k), heads(v), heads(a)
    tm = lambda t: jnp.swapaxes(t, 0, 1)
    S_T, y = lax.scan(_rwkv7_step, S0.astype(F32), (tm(r), tm(decay), tm(k), tm(v), tm(kk), tm(a)))
    y = tm(y)
    y = _head_norm(y, gn_g, RW_GN_EPS) + gn_b.astype(F32)
    y = y + jnp.sum(r * k * r_k.astype(F32), axis=-1, keepdims=True) * v
    y = y.reshape(B, T, D_RW) * g
    return y, z[:, -1], S_T


def _mlstm_chunk(carry, inp):
    C, n, m = carry
    q, k, v, ig, lf = inp
    L = q.shape[1]
    bt = jnp.swapaxes(jnp.cumsum(lf, axis=1), 1, 2)
    igt = jnp.swapaxes(ig, 1, 2)
    causal = jnp.tril(jnp.ones((L, L), dtype=bool))
    dlog = jnp.where(causal, bt[..., :, None] - bt[..., None, :] + igt[..., None, :], -jnp.inf)
    inter = bt + m[..., None]
    m_t = jnp.maximum(inter, jnp.max(dlog, axis=-1))
    A = jnp.exp(dlog - m_t[..., None])
    sc = jnp.exp(inter - m_t)
    aqk = A * jnp.einsum('blhd,bshd->bhls', q, k)
    num = jnp.einsum('bhls,bshd->blhd', aqk, v) + jnp.swapaxes(sc, 1, 2)[..., None] * jnp.einsum('bhvk,blhk->blhv', C, q)
    den = jnp.sum(aqk, axis=-1) + sc * jnp.einsum('bhk,blhk->bhl', n, q)
    denom = jnp.maximum(jnp.abs(den), jnp.exp(-m_t))
    h = num / jnp.swapaxes(denom, 1, 2)[..., None]
    m_new = m_t[..., -1]
    wc = jnp.exp(bt[..., -1:] - bt + igt - m_new[..., None])
    dec = jnp.exp(bt[..., -1] + m - m_new)
    C_new = dec[..., None, None] * C + jnp.einsum('bhs,bshv,bshk->bhvk', wc, v, k)
    n_new = dec[..., None] * n + jnp.einsum('bhs,bshk->bhk', wc, k)
    return (C_new, n_new, m_new), h


def _mlstm_run(state, seqs, chunk):
    B, T = seqs[0].shape[:2]
    nc = T // chunk
    blk = lambda a: jnp.swapaxes(a.reshape((B, nc, chunk) + a.shape[2:]), 0, 1)
    state, h = lax.scan(_mlstm_chunk, state, tuple(blk(a) for a in seqs))
    h = jnp.swapaxes(h, 0, 1).reshape((B, T) + h.shape[3:])
    return state, h


def _mlstm_branch(z, conv_buf, C0, n0, m0, conv_w, conv_b, i_bias, f_bias, gn_g, n_lead, chunk):
    B, T, _ = z.shape
    z = z.astype(F32)
    qk_raw, v, o, ig, fg = _split(z, [2 * D_ML, D_ML, D_ML, ML_HEADS, ML_HEADS])
    xp = jnp.concatenate([conv_buf.astype(F32), qk_raw], axis=1)
    cw = conv_w.astype(F32)
    qk = conv_b.astype(F32) + xp[:, 0:T] * cw[0]
    for j in range(1, CONV_W):
        qk = qk + xp[:, j:j + T] * cw[j]
    qk = jax.nn.silu(qk)
    q, k = jnp.split(qk, 2, axis=-1)
    heads = lambda t: t.reshape(B, T, ML_HEADS, ML_HD)
    q, k, v = heads(q), heads(k) * (ML_HD ** -0.5), heads(v)
    ig = ig + i_bias.astype(F32)
    lf = jax.nn.log_sigmoid(fg + f_bias.astype(F32))
    state = (C0.astype(F32), n0.astype(F32), m0.astype(F32))
    seqs = (q, k, v, ig, lf)
    if n_lead:
        state, h_lead = _mlstm_run(state, tuple(s[:, :n_lead] for s in seqs), n_lead)
        state, h_rest = _mlstm_run(state, tuple(s[:, n_lead:] for s in seqs), chunk)
        h = jnp.concatenate([h_lead, h_rest], axis=1)
    else:
        state, h = _mlstm_run(state, seqs, chunk)
    h = _head_norm(h, gn_g, ML_GN_EPS).reshape(B, T, D_ML) * jax.nn.sigmoid(o)
    return h, xp[:, -(CONV_W - 1):], state


def _layer(x, st, lp, n_lead, chunk):
    S0, sh0, C0, n0, m0, cb0 = st
    u = _rms_norm(x, lp['pre1'])
    z = u @ lp['w_in']
    z_rw, z_ml, z_gate = _split(z, [RW_SHIFT_W, ML_IN_W, 2 * D_MODEL])
    ya, sh1, S1 = _rwkv7_branch(z_rw, sh0, S0, lp['rw_mu'], lp['rw_w0'], lp['rw_w_up'], lp['rw_a0'],
                                lp['rw_a_up'], lp['rw_g_up'], lp['rw_k_k'], lp['rw_k_a'], lp['rw_r_k'],
                                lp['rw_gn_g'], lp['rw_gn_b'])
    yb, cb1, (C1, n1, m1) = _mlstm_branch(z_ml, cb0, C0, n0, m0, lp['ml_conv_w'], lp['ml_conv_b'],
                                          lp['ml_i_bias'], lp['ml_f_bias'], lp['ml_gn_g'], n_lead, chunk)
    ga, gb = jnp.split(jax.nn.sigmoid(z_gate.astype(F32)), 2, axis=-1)
    merged = ga * (ya @ lp['p_a'].astype(F32)) + gb * (yb @ lp['p_b'].astype(F32))
    h = x + _rms_norm(merged.astype(x.dtype) @ lp['w_out'], lp['post1'])
    u2 = _rms_norm(h, lp['pre2'])
    f = jnp.square(jax.nn.relu(u2 @ lp['w_ff_up'])) @ lp['w_ff_down']
    out = h + _rms_norm(f, lp['post2'])
    dt = x.dtype
    return out, (S1.astype(dt), sh1.astype(dt), C1.astype(dt), n1.astype(dt), m1.astype(dt), cb1.astype(dt))


def setup_inputs(seed: int = 0) -> dict:
    key = jax.random.key(seed)
    ks = jax.random.split(key, 40)
    nrm = lambda i, shape, s: jax.random.normal(ks[i], shape, F32) * s
    L = DEPTH
    d = {}
    d['x_prompt'] = nrm(0, (BATCH, SEQ, D_MODEL), 1.0)
    d['x_sample'] = nrm(1, (DEC_BATCH, DEC_SEQ, D_MODEL), 1.0)
    d['state_rwkv_S'] = nrm(2, (L, DEC_BATCH, RW_HEADS, RW_HD, RW_HD), 0.1)
    d['state_rwkv_shift'] = nrm(3, (L, DEC_BATCH, RW_SHIFT_W), 1.0)
    d['state_mlstm_C'] = nrm(4, (L, DEC_BATCH, ML_HEADS, ML_HD, ML_HD), 0.3)
    d['state_mlstm_n'] = nrm(5, (L, DEC_BATCH, ML_HEADS, ML_HD), 0.3)
    d['state_mlstm_m'] = nrm(6, (L, DEC_BATCH, ML_HEADS), 0.5)
    d['state_mlstm_conv'] = nrm(7, (L, DEC_BATCH, CONV_W - 1, 2 * D_ML), 1.0)
    d['meta_tokens'] = nrm(8, (N_META, D_MODEL), 1.0)
    d['w_in'] = nrm(9, (L, D_MODEL, N_IN), D_MODEL ** -0.5)
    d['rw_mu'] = jax.random.uniform(ks[10], (L, RW_SHIFT_W), F32)
    d['rw_w0'] = jax.random.uniform(ks[11], (L, D_RW), F32, -6.0, 1.0)
    d['rw_w_up'] = nrm(12, (L, RW_W_LORA, D_RW), 0.5 * RW_W_LORA ** -0.5)
    d['rw_a0'] = nrm(13, (L, D_RW), 0.5)
    d['rw_a_up'] = nrm(14, (L, RW_A_LORA, D_RW), 0.5 * RW_A_LORA ** -0.5)
    d['rw_g_up'] = nrm(15, (L, RW_G_LORA, D_RW), RW_G_LORA ** -0.5)
    d['rw_k_k'] = 0.85 + nrm(16, (L, D_RW), 0.02)
    d['rw_k_a'] = 1.0 + nrm(17, (L, D_RW), 0.02)
    d['rw_r_k'] = nrm(18, (L, RW_HEADS, RW_HD), 0.1)
    d['rw_gn_g'] = 1.0 + nrm(19, (L, RW_HEADS, RW_HD), 0.02)
    d['rw_gn_b'] = nrm(20, (L, RW_HEADS, RW_HD), 0.02)
    d['ml_conv_w'] = nrm(21, (L, CONV_W, 2 * D_ML), CONV_W ** -0.5)
    d['ml_conv_b'] = nrm(22, (L, 2 * D_ML), 0.01)
    d['ml_i_bias'] = nrm(23, (L, ML_HEADS), 0.1)
    d['ml_f_bias'] = jnp.linspace(3.0, 6.0, ML_HEADS, dtype=F32)[None] + nrm(24, (L, ML_HEADS), 0.1)
    d['ml_gn_g'] = 1.0 + nrm(25, (L, ML_HEADS, ML_HD), 0.02)
    d['p_a'] = nrm(26, (L, D_RW, D_MODEL), D_RW ** -0.5)
    d['p_b'] = nrm(27, (L, D_ML, D_MODEL), D_ML ** -0.5)
    d['w_out'] = nrm(28, (L, D_MODEL, D_MODEL), D_MODEL ** -0.5)
    d['pre1'] = 1.0 + nrm(29, (L, D_MODEL), 0.02)
    d['post1'] = 1.0 + nrm(30, (L, D_MODEL), 0.02)
    d['pre2'] = 1.0 + nrm(31, (L, D_MODEL), 0.02)
    d['post2'] = 1.0 + nrm(32, (L, D_MODEL), 0.02)
    d['w_ff_up'] = nrm(33, (L, D_MODEL, D_FF), D_MODEL ** -0.5)
    d['w_ff_down'] = nrm(34, (L, D_FF, D_MODEL), D_FF ** -0.5)
    return d


def reference(x_prompt, x_sample, state_rwkv_S, state_rwkv_shift, state_mlstm_C, state_mlstm_n,
              state_mlstm_m, state_mlstm_conv, meta_tokens, w_in, rw_mu, rw_w0, rw_w_up, rw_a0, rw_a_up,
              rw_g_up, rw_k_k, rw_k_a, rw_r_k, rw_gn_g, rw_gn_b, ml_conv_w, ml_conv_b, ml_i_bias, ml_f_bias,
              ml_gn_g, p_a, p_b, w_out, pre1, post1, pre2, post2, w_ff_up, w_ff_down):
    B = x_prompt.shape[0]
    dt = x_prompt.dtype
    xp = jnp.concatenate([jnp.broadcast_to(meta_tokens[None].astype(dt), (B, N_META, D_MODEL)), x_prompt], axis=1)
    xs = x_sample
    zero_state = (jnp.zeros((B, RW_HEADS, RW_HD, RW_HD), F32), jnp.zeros((B, RW_SHIFT_W), F32),
                  jnp.zeros((B, ML_HEADS, ML_HD, ML_HD), F32), jnp.zeros((B, ML_HEADS, ML_HD), F32),
                  jnp.zeros((B, ML_HEADS), F32), jnp.zeros((B, CONV_W - 1, 2 * D_ML), F32))
    p_states, s_states = [], []
    for l in range(DEPTH):
        lp = dict(w_in=w_in[l], rw_mu=rw_mu[l], rw_w0=rw_w0[l], rw_w_up=rw_w_up[l], rw_a0=rw_a0[l],
                  rw_a_up=rw_a_up[l], rw_g_up=rw_g_up[l], rw_k_k=rw_k_k[l], rw_k_a=rw_k_a[l], rw_r_k=rw_r_k[l],
                  rw_gn_g=rw_gn_g[l], rw_gn_b=rw_gn_b[l], ml_conv_w=ml_conv_w[l], ml_conv_b=ml_conv_b[l],
                  ml_i_bias=ml_i_bias[l], ml_f_bias=ml_f_bias[l], ml_gn_g=ml_gn_g[l], p_a=p_a[l], p_b=p_b[l],
                  w_out=w_out[l], pre1=pre1[l], post1=post1[l], pre2=pre2[l], post2=post2[l],
                  w_ff_up=w_ff_up[l], w_ff_down=w_ff_down[l])
        xp, st_p = _layer(xp, zero_state, lp, N_META, ML_CHUNK)
        st_in = (state_rwkv_S[l], state_rwkv_shift[l], state_mlstm_C[l], state_mlstm_n[l],
                 state_mlstm_m[l], state_mlstm_conv[l])
        xs, st_s = _layer(xs, st_in, lp, 0, xs.shape[1])
        p_states.append(st_p)
        s_states.append(st_s)
    stk = lambda lst, i: jnp.stack([s[i] for s in lst])
    y_prompt = xp[:, N_META:]
    return (y_prompt, xs,
            stk(p_states, 0), stk(p_states, 1), stk(p_states, 2), stk(p_states, 3), stk(p_states, 4), stk(p_states, 5),
            stk(s_states, 0), stk(s_states, 1), stk(s_states, 2), stk(s_states, 3), stk(s_states, 4), stk(s_states, 5))
```

```python
import functools

import jax
import jax.numpy as jnp
from jax import lax
from jax.experimental import pallas as pl
from jax.experimental.pallas import tpu as pltpu

F32 = jnp.float32
BF16 = jnp.bfloat16
HI = lax.Precision.HIGHEST

D_MODEL = 1024
N_META = 16
RW_HEADS = 8
RW_HD = 64
D_RW = RW_HEADS * RW_HD
RW_LORA_W = 128
RW_G_LORA = 128
RW_SHIFT_W = 3 * D_RW + RW_LORA_W + RW_G_LORA
RW_GN_EPS = 64e-5
ML_HEADS = 4
ML_HD = 128
D_ML = ML_HEADS * ML_HD
CONV_W = 4
ML_MAIN_W = 4 * D_ML
ML_GN_EPS = 1e-5
GATE_W = 128
D_FF = 4 * D_MODEL
RMS_EPS = 1e-6
SUBLANES = 8

PROMPT_CHUNK = 64
VMEM_LIMIT = 56 * 2**20


def _dot(a, b, prec=HI):
    return jnp.dot(a, b, precision=prec, preferred_element_type=F32)


def _dot_nt(a, b, prec=HI):
    return lax.dot_general(a, b, (((1,), (1,)), ((), ())), precision=prec, preferred_element_type=F32)


def _dot_tn(a, b, prec=HI):
    return lax.dot_general(a, b, (((0,), (0,)), ((), ())), precision=prec, preferred_element_type=F32)


def _bdot(a, b):
    return jnp.einsum('hts,hsr->htr', a, b, precision=HI, preferred_element_type=F32)


def _sigmoid(x):
    return 1.0 / (1.0 + jnp.exp(-x))


def _softplus(x):
    return jnp.maximum(x, 0.0) + jnp.log(1.0 + jnp.exp(-jnp.abs(x)))


def _iota(shape, dim):
    return lax.broadcasted_iota(jnp.int32, shape, dim)


def _rms(x, g):
    return x * lax.rsqrt(jnp.mean(x * x, axis=-1, keepdims=True) + RMS_EPS) * g


def _const_spec(shape):
    nd = len(shape)
    return pl.BlockSpec(shape, lambda *_: (0,) * nd, pipeline_mode=pl.Buffered(1))


def _in_proj_body(x_ref, g_ref, wrw_ref, wml_ref, wif_ref, zrw_ref, zml_ref, zif_ref):
    u = _rms(x_ref[0], g_ref[...]).astype(BF16)
    zrw_ref[0] = jnp.dot(u, wrw_ref[...], preferred_element_type=F32)
    zml_ref[0] = jnp.dot(u, wml_ref[...], preferred_element_type=F32)
    zif_ref[0] = jnp.dot(u, wif_ref[...], preferred_element_type=F32)


def _in_proj(x, g, w_rw, w_ml, w_if, tm):
    B, T, _ = x.shape
    row = lambda w: pl.BlockSpec((1, tm, w), lambda b, i: (b, i, 0))
    return pl.pallas_call(
        _in_proj_body,
        out_shape=(jax.ShapeDtypeStruct((B, T, RW_SHIFT_W), F32),
                   jax.ShapeDtypeStruct((B, T, ML_MAIN_W), F32),
                   jax.ShapeDtypeStruct((B, T, GATE_W), F32)),
        grid=(B, T // tm),
        in_specs=[row(D_MODEL), _const_spec((1, D_MODEL)), _const_spec(w_rw.shape), _const_spec(w_ml.shape),
                  _const_spec(w_if.shape)],
        out_specs=(row(RW_SHIFT_W), row(ML_MAIN_W), row(GATE_W)),
        compiler_params=pltpu.CompilerParams(dimension_semantics=("arbitrary", "arbitrary"),
                                             vmem_limit_bytes=VMEM_LIMIT),
        name="in_proj",
    )(x, g, w_rw, w_ml, w_if)


def _head_sum(x, ones_bd):
    return _dot(x, ones_bd)


def _rwkv_body(L, nc, z_ref, sh0_ref, s0_ref, mu_ref, w0_ref, a0_ref, wc_ref, gup_ref, kk_ref, ka_ref, rk_ref,
               gng_ref, gnb_ref, y_ref, sout_ref, shout_ref, g_scr, prev_scr):
    c = pl.program_id(1)
    H, D = RW_HEADS, D_RW
    same_head = (_iota((D, D), 0) >> 6) == (_iota((D, D), 1) >> 6)
    ones_bd = same_head.astype(F32)
    lane_head = _iota((1, D), 1) >> 6
    hmask = [(lane_head == h).astype(F32) for h in range(H)]

    @pl.when(c == 0)
    def _():
        prev_scr[...] = sh0_ref[0]
        spread = (_iota((RW_HD, D), 0) == (_iota((RW_HD, D), 1) & (RW_HD - 1))).astype(F32)
        g_scr[...] = jnp.where(same_head, _dot(s0_ref[0], spread), 0.0)

    z = z_ref[0]
    prev = jnp.where(_iota(z.shape, 0) == 0, prev_scr[...], pltpu.roll(z, 1, 0))
    prev_scr[...] = z[L - 1:L, :]
    zs = z + (prev - z) * mu_ref[...]
    r = zs[:, 0:D]
    kraw = zs[:, D:2 * D]
    v = zs[:, 2 * D:3 * D]
    xl = zs[:, 3 * D:3 * D + RW_LORA_W]
    gl = zs[:, 3 * D + RW_LORA_W:]
    xl = jnp.where(_iota(xl.shape, 1) < RW_LORA_W // 2, jnp.tanh(xl), xl)
    lora = _dot(xl, wc_ref[...])
    w = -_softplus(-(w0_ref[...] + lora[:, :D])) - 0.5
    logw = -jnp.exp(w)
    a = _sigmoid(a0_ref[...] + lora[:, D:])
    g = _dot(_sigmoid(gl), gup_ref[...])
    kk = kraw * kk_ref[...]
    kk = kk * lax.rsqrt(jnp.maximum(_head_sum(kk * kk, ones_bd), 1e-24))
    k = kraw * (1.0 + (a - 1.0) * ka_ref[...])

    tril = (_iota((L, L), 0) >= _iota((L, L), 1)).astype(F32)
    cl = _dot(tril, logw)
    cl_last = cl[L - 1:L, :]
    at = -kk * jnp.exp(cl - logw)
    bt = kk * a * jnp.exp(-cl)
    kt = k * jnp.exp(-cl)
    rt = r * jnp.exp(cl)
    b_end = kk * a * jnp.exp(cl_last - cl)
    k_end = k * jnp.exp(cl_last - cl)
    stack = lambda x: jnp.concatenate([x * hmask[h] for h in range(H)], axis=0)

    n = _dot_nt(stack(at), bt).reshape(H, L, L)
    n = jnp.where(_iota((H, L, L), 1) > _iota((H, L, L), 2), n, 0.0)
    tinv = n + (_iota((H, L, L), 1) == _iota((H, L, L), 2)).astype(F32)
    p = n
    for _ in range(max(L.bit_length() - 2, 0)):
        p = _bdot(p, p)
        tinv = tinv + _bdot(tinv, p)

    t_row = _iota((L, H * L), 0)
    s_col = _iota((L, H * L), 1) & (L - 1)
    kt_stack = stack(kt)
    v_stack = stack(v)
    a_ak = jnp.where(s_col < t_row, _dot_nt(at, kt_stack), 0.0)
    incl = (_iota((L, 2 * H * L), 1) & (L - 1)) <= _iota((L, 2 * H * L), 0)
    a_r = jnp.where(incl,_dot_nt(rt, jnp.concatenate([stack(bt), kt_stack], axis=0)), 0.0)

    G = g_scr[...]
    wmat = _dot_nt(at, G) + _dot(a_ak, v_stack)
    u = hmask[0] * _dot(tinv[0], wmat)
    for h in range(1, H):
        u = u + hmask[h] * _dot(tinv[h], wmat)
    y = _dot_nt(rt, G) + _dot(a_r, jnp.concatenate([stack(u), v_stack], axis=0))
    upd = _dot_tn(jnp.concatenate([u, v], axis=0), jnp.concatenate([b_end, k_end], axis=0))
    g_new = G * jnp.exp(cl_last) + jnp.where(same_head, upd, 0.0)
    g_scr[...] = g_new

    mean = _head_sum(y, ones_bd) * (1.0 / RW_HD)
    yc = y - mean
    var = _head_sum(yc * yc, ones_bd) * (1.0 / RW_HD)
    yn = yc * lax.rsqrt(var + RW_GN_EPS) * gng_ref[...] + gnb_ref[...]
    yn = yn + _head_sum(r * k * rk_ref[...], ones_bd) * v
    y_ref[0] = yn * g

    @pl.when(c == nc - 1)
    def _():
        gather = ((_iota((D, RW_HD), 0) & (RW_HD - 1)) == _iota((D, RW_HD), 1)).astype(F32)
        sout_ref[0] = _dot(g_new, gather)
        shout_ref[0] = z[L - 1:L, :]


def _rwkv(z, shift0, s0, params, L):
    B, T, _ = z.shape
    nc = T // L
    per_b = lambda shp: pl.BlockSpec((1,) + shp, lambda b, c: (b, 0, 0))
    return pl.pallas_call(
        functools.partial(_rwkv_body, L, nc),
        out_shape=(jax.ShapeDtypeStruct((B, T, D_RW), F32),
                   jax.ShapeDtypeStruct((B, D_RW, RW_HD), F32),
                   jax.ShapeDtypeStruct((B, 1, RW_SHIFT_W), F32)),
        grid=(B, nc),
        in_specs=[pl.BlockSpec((1, L, RW_SHIFT_W), lambda b, c: (b, c, 0)),
                  per_b((1, RW_SHIFT_W)), per_b((D_RW, RW_HD))] + [_const_spec(p.shape) for p in params],
        out_specs=(pl.BlockSpec((1, L, D_RW), lambda b, c: (b, c, 0)),
                   per_b((D_RW, RW_HD)), per_b((1, RW_SHIFT_W))),
        scratch_shapes=[pltpu.VMEM((D_RW, D_RW), F32), pltpu.VMEM((1, RW_SHIFT_W), F32)],
        compiler_params=pltpu.CompilerParams(dimension_semantics=("arbitrary", "arbitrary"),
                                             vmem_limit_bytes=VMEM_LIMIT),
        name="rwkv",
    )(z, shift0, s0, *params)


def _mlstm_body(L, nc, n_pad, zm_ref, zif_ref, cv0_ref, c0_ref, nm0_ref, cw_ref, cb_ref, ifb_ref, gn_ref,
                y_ref, cout_ref, nmout_ref, cvout_ref, c_scr, nm_scr, cv_scr):
    c = pl.program_id(1)
    H, HD, D = ML_HEADS, ML_HD, D_ML

    @pl.when(c == 0)
    def _():
        cv_scr[...] = cv0_ref[0]
        c_scr[...] = c0_ref[0]
        nm_scr[...] = nm0_ref[0]

    zm = zm_ref[0]
    raw = zm[:, :2 * D]
    v = zm[:, 2 * D:3 * D]
    o = zm[:, 3 * D:]
    ext = jnp.concatenate([cv_scr[...], raw], axis=0)
    cv_scr[...] = ext[L:, :]
    cw = cw_ref[...]
    qk = cb_ref[...] + raw * cw[CONV_W - 1:CONV_W, :]
    for s in range(1, CONV_W):
        qk = qk + pltpu.roll(ext, s, 0)[SUBLANES:, :] * cw[CONV_W - 1 - s:CONV_W - s, :]
    qk = qk * _sigmoid(qk)
    q = qk[:, :D]
    k = qk[:, D:] * (HD ** -0.5)

    gi = zif_ref[0] + ifb_ref[...]
    lane = _iota(gi.shape, 1)
    lf = jnp.minimum(gi, 0.0) - jnp.log(1.0 + jnp.exp(-jnp.abs(gi)))
    gcol = jnp.where(lane < H, gi, jnp.where(lane < 2 * H, lf, 0.0))
    if n_pad:
        pad_col = (c * L + _iota((L, 1), 0)) < n_pad
        pad_row = (c * L + _iota((1, L), 1)) < n_pad
        gcol = jnp.where(pad_col, 0.0, gcol)
    eye = (_iota((GATE_W, GATE_W), 0) == _iota((GATE_W, GATE_W), 1)).astype(F32)
    grow = _dot_nt(eye, gcol)
    tril = (_iota((L, L), 0) >= _iota((L, L), 1)).astype(F32)
    causal = _iota((L, L), 0) >= _iota((L, L), 1)
    if n_pad:
        causal = causal & jnp.logical_not(pad_row)
    b_col = _dot(tril, gcol)
    b_row = _dot_nt(grow, tril)

    nm = nm_scr[...]
    outs = []
    m_next = nm[H:H + 1, :]
    for h in range(H):
        sl = slice(h * HD, (h + 1) * HD)
        qh, kh, vh = q[:, sl], k[:, sl], v[:, sl]
        ch = c_scr[sl, :]
        nh = nm[h:h + 1, :]
        m_prev = nm[H:H + 1, h:h + 1]
        bc = b_col[:, H + h:H + h + 1]
        ic = gcol[:, h:h + 1]
        br = b_row[H + h:H + h + 1, :]
        ir = grow[h:h + 1, :]
        dlog = jnp.where(causal, bc - br + ir, -jnp.inf)
        inter = bc + m_prev
        m_t = jnp.maximum(inter, jnp.max(dlog, axis=-1, keepdims=True))
        amat = jnp.exp(dlog - m_t)
        sc = jnp.exp(inter - m_t)
        aqk = amat * _dot_nt(qh, kh)
        num = _dot(aqk, vh) + sc * _dot_nt(qh, ch)
        den = jnp.sum(aqk, axis=-1, keepdims=True) + sc * jnp.sum(qh * nh, axis=-1, keepdims=True)
        hh = num / jnp.maximum(jnp.abs(den), jnp.exp(-m_t))
        m_new = m_t[L - 1:L, :]
        b_last = bc[L - 1:L, :]
        wc = jnp.exp(b_last - bc + ic - m_new)
        if n_pad:
            wc = jnp.where(pad_col, 0.0, wc)
        dec = jnp.exp(b_last + m_prev - m_new)
        c_scr[sl, :] = dec * ch + _dot_tn(vh * wc, kh)
        nm_scr[h:h + 1, :] = dec * nh + jnp.sum(kh * wc, axis=0, keepdims=True)
        m_next = jnp.where(_iota(m_next.shape, 1) == h, m_new, m_next)
        mu = jnp.mean(hh, axis=-1, keepdims=True)
        xc = hh - mu
        var = jnp.mean(xc * xc, axis=-1, keepdims=True)
        outs.append(xc * lax.rsqrt(var + ML_GN_EPS) * gn_ref[:, sl] * _sigmoid(o[:, sl]))
    nm_scr[H:H + 1, :] = m_next
    y_ref[0] = jnp.concatenate(outs, axis=1)

    @pl.when(c == nc - 1)
    def _():
        cout_ref[0] = c_scr[...]
        nmout_ref[0] = nm_scr[...]
        cvout_ref[0] = cv_scr[...]


def _mlstm(zm, zif, cv0, c0, nm0, params, L, n_pad):
    B, T, _ = zm.shape
    nc = T // L
    per_b = lambda shp: pl.BlockSpec((1,) + shp, lambda b, c: (b, 0, 0))
    chunk = lambda w: pl.BlockSpec((1, L, w), lambda b, c: (b, c, 0))
    state_shapes = ((SUBLANES, 2 * D_ML), (D_ML, ML_HD), (SUBLANES, ML_HD))
    return pl.pallas_call(
        functools.partial(_mlstm_body, L, nc, n_pad),
        out_shape=(jax.ShapeDtypeStruct((B, T, D_ML), F32),
                   jax.ShapeDtypeStruct((B,) + state_shapes[1], F32),
                   jax.ShapeDtypeStruct((B,) + state_shapes[2], F32),
                   jax.ShapeDtypeStruct((B,) + state_shapes[0], F32)),
        grid=(B, nc),
        in_specs=[chunk(ML_MAIN_W), chunk(GATE_W), per_b(state_shapes[0]), per_b(state_shapes[1]),
                  per_b(state_shapes[2])] + [_const_spec(p.shape) for p in params],
        out_specs=(chunk(D_ML), per_b(state_shapes[1]), per_b(state_shapes[2]), per_b(state_shapes[0])),
        scratch_shapes=[pltpu.VMEM(state_shapes[1], F32), pltpu.VMEM(state_shapes[2], F32),
                        pltpu.VMEM(state_shapes[0], F32)],
        compiler_params=pltpu.CompilerParams(dimension_semantics=("arbitrary", "arbitrary"),
                                             vmem_limit_bytes=VMEM_LIMIT),
        name="mlstm",
    )(zm, zif, cv0, c0, nm0, *params)


def _merge_body(x_ref, ya_ref, yb_ref, pre_ref, post_ref, wg_ref, pa_ref, pb_ref, wo_ref, h_ref):
    x = x_ref[0]
    u = _rms(x, pre_ref[...]).astype(BF16)
    gate = _sigmoid(jnp.dot(u, wg_ref[...], preferred_element_type=F32))
    pa = jnp.dot(ya_ref[0].astype(BF16), pa_ref[...], preferred_element_type=F32)
    pb = jnp.dot(yb_ref[0].astype(BF16), pb_ref[...], preferred_element_type=F32)
    merged = gate[:, :D_MODEL] * pa + gate[:, D_MODEL:] * pb
    o = jnp.dot(merged.astype(BF16), wo_ref[...], preferred_element_type=F32)
    h_ref[0] = x + _rms(o, post_ref[...])


def _merge(x, ya, yb, pre, post, wg, pa, pb, wo, tm):
    B, T, _ = x.shape
    row = lambda w: pl.BlockSpec((1, tm, w), lambda b, i: (b, i, 0))
    consts = (pre, post, wg, pa, pb, wo)
    return pl.pallas_call(
        _merge_body,
        out_shape=jax.ShapeDtypeStruct((B, T, D_MODEL), F32),
        grid=(B, T // tm),
        in_specs=[row(D_MODEL), row(D_RW), row(D_ML)] + [_const_spec(p.shape) for p in consts],
        out_specs=row(D_MODEL),
        compiler_params=pltpu.CompilerParams(dimension_semantics=("arbitrary", "arbitrary"),
                                             vmem_limit_bytes=VMEM_LIMIT),
        name="merge",
    )(x, ya, yb, *consts)


FF_SPLIT = 4


def _ffn_body(tm, n_pad, h_ref, pre_ref, post_ref, wu_ref, wd_ref, o_ref):
    h = h_ref[0]
    u = _rms(h, pre_ref[...]).astype(BF16)
    step = D_FF // FF_SPLIT
    f = None
    for j in range(FF_SPLIT):
        t = jnp.maximum(jnp.dot(u, wu_ref[:, j * step:(j + 1) * step], preferred_element_type=F32), 0.0)
        part = jnp.dot((t * t).astype(BF16), wd_ref[j * step:(j + 1) * step, :], preferred_element_type=F32)
        f = part if f is None else f + part
    out = h + _rms(f, post_ref[...])
    if n_pad:
        out = jnp.where(pl.program_id(1) * tm + _iota((tm, 1), 0) < n_pad, 0.0, out)
    o_ref[0] = out


def _ffn(h, pre, post, wu, wd, tm, n_pad):
    B, T, _ = h.shape
    row = pl.BlockSpec((1, tm, D_MODEL), lambda b, i: (b, i, 0))
    consts = (pre, post, wu, wd)
    return pl.pallas_call(
        functools.partial(_ffn_body, tm, n_pad),
        out_shape=jax.ShapeDtypeStruct((B, T, D_MODEL), F32),
        grid=(B, T // tm),
        in_specs=[row] + [_const_spec(p.shape) for p in consts],
        out_specs=row,
        compiler_params=pltpu.CompilerParams(dimension_semantics=("arbitrary", "arbitrary"),
                                             vmem_limit_bytes=VMEM_LIMIT),
        name="ffn",
    )(h, *consts)


def _row_tile(T):
    for tm in (704, 512, 256, 128, 64, 32, 16, 8):
        if T % tm == 0:
            return tm
    raise ValueError(f"unsupported sequence length {T}")


def _layer(x, st, lp, L, n_pad):
    B, T, _ = x.shape
    S0, sh0, C0, n0, m0, cb0 = st
    flat = (lambda a: a.reshape(1, B * T, a.shape[-1])) if not n_pad else (lambda a: a)
    unflat = lambda a: a.reshape(B, T, a.shape[-1])
    xf = flat(x)
    tm = _row_tile(xf.shape[1])
    z_rw, z_ml, z_if = map(unflat, _in_proj(xf, lp['pre1'], lp['w_rw'], lp['w_ml'], lp['w_if'], tm))

    ya, S1, sh1 = _rwkv(z_rw, sh0[:, None, :], S0.reshape(B, D_RW, RW_HD), lp['rw_params'], L)

    cv0 = jnp.pad(cb0, ((0, 0), (SUBLANES - (CONV_W - 1), 0), (0, 0)))
    nm0 = jnp.concatenate([n0, jnp.pad(m0, ((0, 0), (0, ML_HD - ML_HEADS)))[:, None, :],
                           jnp.zeros((B, SUBLANES - ML_HEADS - 1, ML_HD), F32)], axis=1)
    yb, C1, nm1, cv1 = _mlstm(z_ml, z_if, cv0, C0.reshape(B, D_ML, ML_HD), nm0, lp['ml_params'], L, n_pad)

    h = _merge(xf, flat(ya), flat(yb), lp['pre1'], lp['post1'], lp['w_gate'], lp['p_a'], lp['p_b'], lp['w_out'], tm)
    out = unflat(_ffn(h, lp['pre2'], lp['post2'], lp['w_ff_up'], lp['w_ff_down'], tm, n_pad))
    new_state = (S1.reshape(B, RW_HEADS, RW_HD, RW_HD), sh1[:, 0, :], C1.reshape(B, ML_HEADS, ML_HD, ML_HD),
                 nm1[:, :ML_HEADS, :], nm1[:, ML_HEADS, :ML_HEADS], cv1[:, SUBLANES - (CONV_W - 1):, :])
    return out, new_state


def _layer_params(l, w_in, rw_mu, rw_w0, rw_w_up, rw_a0, rw_a_up, rw_g_up, rw_k_k, rw_k_a, rw_r_k, rw_gn_g,
                  rw_gn_b, ml_conv_w, ml_conv_b, ml_i_bias, ml_f_bias, ml_gn_g, p_a, p_b, w_out, pre1, post1,
                  pre2, post2, w_ff_up, w_ff_down):
    row = lambda a: a[l].reshape(1, -1).astype(F32)
    w = w_in[l]
    c_ml = RW_SHIFT_W
    c_if = c_ml + ML_MAIN_W
    c_gate = c_if + 2 * ML_HEADS
    half = RW_LORA_W // 2
    lora = jnp.zeros((RW_LORA_W, 2 * D_RW), F32)
    lora = lora.at[:half, :D_RW].set(rw_w_up[l]).at[half:, D_RW:].set(rw_a_up[l])
    if_bias = jnp.zeros((1, GATE_W), F32)
    if_bias = if_bias.at[0, :ML_HEADS].set(ml_i_bias[l]).at[0, ML_HEADS:2 * ML_HEADS].set(ml_f_bias[l])
    return dict(
        pre1=row(pre1), post1=row(post1), pre2=row(pre2), post2=row(post2),
        w_rw=w[:, :c_ml].astype(BF16),
        w_ml=w[:, c_ml:c_if].astype(BF16),
        w_if=jnp.pad(w[:, c_if:c_gate], ((0, 0), (0, GATE_W - 2 * ML_HEADS))).astype(BF16),
        w_gate=w[:, c_gate:].astype(BF16),
        rw_params=(row(rw_mu), row(rw_w0), row(rw_a0), lora, rw_g_up[l].astype(F32), row(rw_k_k), row(rw_k_a),
                   row(rw_r_k), row(rw_gn_g), row(rw_gn_b)),
        ml_params=(ml_conv_w[l].astype(F32), row(ml_conv_b), if_bias, row(ml_gn_g)),
        p_a=p_a[l].astype(BF16), p_b=p_b[l].astype(BF16), w_out=w_out[l].astype(BF16),
        w_ff_up=w_ff_up[l].astype(BF16), w_ff_down=w_ff_down[l].astype(BF16),
    )


def kernel(x_prompt, x_sample, state_rwkv_S, state_rwkv_shift, state_mlstm_C, state_mlstm_n, state_mlstm_m,
           state_mlstm_conv, meta_tokens, w_in, rw_mu, rw_w0, rw_w_up, rw_a0, rw_a_up, rw_g_up, rw_k_k, rw_k_a,
           rw_r_k, rw_gn_g, rw_gn_b, ml_conv_w, ml_conv_b, ml_i_bias, ml_f_bias, ml_gn_g, p_a, p_b, w_out, pre1,
           post1, pre2, post2, w_ff_up, w_ff_down):
    B, T, _ = x_prompt.shape
    dt = x_prompt.dtype
    depth = w_in.shape[0]
    L = PROMPT_CHUNK
    n_pad = (-(T + N_META)) % L
    xp = jnp.concatenate([jnp.zeros((B, n_pad, D_MODEL), dt),
                          jnp.broadcast_to(meta_tokens[None].astype(dt), (B, N_META, D_MODEL)), x_prompt], axis=1)
    xs = x_sample
    Bs, Ts, _ = xs.shape
    zero_state = (jnp.zeros((B, RW_HEADS, RW_HD, RW_HD), F32), jnp.zeros((B, RW_SHIFT_W), F32),
                  jnp.zeros((B, ML_HEADS, ML_HD, ML_HD), F32), jnp.zeros((B, ML_HEADS, ML_HD), F32),
                  jnp.zeros((B, ML_HEADS), F32), jnp.zeros((B, CONV_W - 1, 2 * D_ML), F32))
    p_states, s_states = [], []
    for l in range(depth):
        lp = _layer_params(l, w_in, rw_mu, rw_w0, rw_w_up, rw_a0, rw_a_up, rw_g_up, rw_k_k, rw_k_a, rw_r_k,
                           rw_gn_g, rw_gn_b, ml_conv_w, ml_conv_b, ml_i_bias, ml_f_bias, ml_gn_g, p_a, p_b, w_out,
                           pre1, post1, pre2, post2, w_ff_up, w_ff_down)
        xp, st_p = _layer(xp, zero_state, lp, L, n_pad)
        st_in = (state_rwkv_S[l], state_rwkv_shift[l], state_mlstm_C[l], state_mlstm_n[l], state_mlstm_m[l],
                 state_mlstm_conv[l])
        xs, st_s = _layer(xs, st_in, lp, Ts, 0)
        p_states.append(st_p)
        s_states.append(st_s)
    stk = lambda lst, i: jnp.stack([s[i] for s in lst]).astype(dt)
    y_prompt = xp[:, n_pad + N_META:]
    return (y_prompt, xs,
            stk(p_states, 0), stk(p_states, 1), stk(p_states, 2), stk(p_states, 3), stk(p_states, 4), stk(p_states, 5),
            stk(s_states, 0), stk(s_states, 1), stk(s_states, 2), stk(s_states, 3), stk(s_states, 4), stk(s_states, 5))
```

```python
import functools

import jax
import jax.numpy as jnp
from jax import lax
from jax.experimental import pallas as pl
from jax.experimental.pallas import tpu as pltpu

F32 = jnp.float32
BF16 = jnp.bfloat16

D_MODEL = 1024
N_META = 16
RW_HEADS = 8
RW_HD = 64
D_RW = RW_HEADS * RW_HD
RW_LORA_W = 128
RW_G_LORA = 128
RW_SHIFT_W = 3 * D_RW + RW_LORA_W + RW_G_LORA
RW_GN_EPS = 64e-5
ML_HEADS = 4
ML_HD = 128
D_ML = ML_HEADS * ML_HD
CONV_W = 4
ML_MAIN_W = 4 * D_ML
ML_GN_EPS = 1e-5
GATE_W = 128
D_FF = 4 * D_MODEL
RMS_EPS = 1e-6
SUBLANES = 8

PROMPT_CHUNK = 64
VMEM_LIMIT = 56 * 2**20


def _mm(a, b):
    return jnp.dot(a.astype(BF16), b.astype(BF16), preferred_element_type=F32)


def _mm_nt(a, b):
    return lax.dot_general(a.astype(BF16), b.astype(BF16), (((1,), (1,)), ((), ())), preferred_element_type=F32)


def _mm_tn(a, b):
    return lax.dot_general(a.astype(BF16), b.astype(BF16), (((0,), (0,)), ((), ())), preferred_element_type=F32)


def _bmm(a, b):
    return jnp.einsum('hts,hsr->htr', a.astype(BF16), b.astype(BF16), preferred_element_type=F32)


def _bmm3(a, b):
    m = a.shape[1]
    a_hi, a_lo = _split(a, 2)
    b_hi, b_lo = _split(b, 2)
    r = jnp.einsum('hts,hsr->htr', jnp.concatenate([a_hi, a_lo], axis=1), b_hi, preferred_element_type=F32)
    return r[:, :m] + r[:, m:] + jnp.einsum('hts,hsr->htr', a_hi, b_lo, preferred_element_type=F32)


def _split(x, terms):
    parts = []
    for _ in range(terms - 1):
        p = x.astype(BF16)
        parts.append(p)
        x = x - p.astype(F32)
    parts.append(x.astype(BF16))
    return parts


def _sel_right(x, sel, terms):
    m = x.shape[0]
    r = jnp.dot(jnp.concatenate(_split(x, terms), axis=0), sel, preferred_element_type=F32)
    return sum(r[i * m:(i + 1) * m] for i in range(terms))


def _sel_left(sel, x, terms):
    n = x.shape[1]
    r = jnp.dot(sel, jnp.concatenate(_split(x, terms), axis=1), preferred_element_type=F32)
    return sum(r[:, i * n:(i + 1) * n] for i in range(terms))


def _sigmoid(x):
    return 1.0 / (1.0 + jnp.exp(-x))


def _softplus(x):
    return jnp.maximum(x, 0.0) + jnp.log(1.0 + jnp.exp(-jnp.abs(x)))


def _iota(shape, dim):
    return lax.broadcasted_iota(jnp.int32, shape, dim)


def _rms(x, g):
    return x * lax.rsqrt(jnp.mean(x * x, axis=-1, keepdims=True) + RMS_EPS) * g


def _const_spec(shape):
    nd = len(shape)
    return pl.BlockSpec(shape, lambda *_: (0,) * nd, pipeline_mode=pl.Buffered(1))


def _in_proj_body(x_ref, g_ref, wrw_ref, wml_ref, wif_ref, zrw_ref, zml_ref, zif_ref):
    u = _rms(x_ref[0], g_ref[...]).astype(BF16)
    zrw_ref[0] = jnp.dot(u, wrw_ref[...], preferred_element_type=F32)
    zml_ref[0] = jnp.dot(u, wml_ref[...], preferred_element_type=F32)
    zif_ref[0] = jnp.dot(u, wif_ref[...], preferred_element_type=F32)


def _in_proj(x, g, w_rw, w_ml, w_if, tm):
    B, T, _ = x.shape
    row = lambda w: pl.BlockSpec((1, tm, w), lambda b, i: (b, i, 0))
    return pl.pallas_call(
        _in_proj_body,
        out_shape=(jax.ShapeDtypeStruct((B, T, RW_SHIFT_W), F32),
                   jax.ShapeDtypeStruct((B, T, ML_MAIN_W), F32),
                   jax.ShapeDtypeStruct((B, T, GATE_W), F32)),
        grid=(B, T // tm),
        in_specs=[row(D_MODEL), _const_spec((1, D_MODEL)), _const_spec(w_rw.shape), _const_spec(w_ml.shape),
                  _const_spec(w_if.shape)],
        out_specs=(row(RW_SHIFT_W), row(ML_MAIN_W), row(GATE_W)),
        compiler_params=pltpu.CompilerParams(dimension_semantics=("arbitrary", "arbitrary"),
                                             vmem_limit_bytes=VMEM_LIMIT),
        name="in_proj",
    )(x, g, w_rw, w_ml, w_if)


def _rwkv_body(L, nc, z_ref, sh0_ref, s0_ref, mu_ref, w0_ref, a0_ref, wc_ref, gup_ref, kk_ref, ka_ref, rk_ref,
               gng_ref, gnb_ref, y_ref, sout_ref, shout_ref, g_scr, prev_scr):
    c = pl.program_id(1)
    H, D = RW_HEADS, D_RW
    same_head = (_iota((D, D), 0) >> 6) == (_iota((D, D), 1) >> 6)
    ones_bd = same_head.astype(BF16)
    head_sum = lambda x: _sel_right(x, ones_bd, 2)
    lane_head = _iota((1, D), 1) >> 6
    hmask = [(lane_head == h).astype(F32) for h in range(H)]

    @pl.when(c == 0)
    def _():
        prev_scr[...] = sh0_ref[0]
        spread = (_iota((RW_HD, D), 0) == (_iota((RW_HD, D), 1) & (RW_HD - 1))).astype(BF16)
        g_scr[...] = jnp.where(same_head, _sel_right(s0_ref[0], spread, 3), 0.0)

    z = z_ref[0]
    prev = jnp.where(_iota(z.shape, 0) == 0, prev_scr[...], pltpu.roll(z, 1, 0))
    prev_scr[...] = z[L - 1:L, :]
    zs = z + (prev - z) * mu_ref[...]
    r = zs[:, 0:D]
    kraw = zs[:, D:2 * D]
    v = zs[:, 2 * D:3 * D]
    xl = zs[:, 3 * D:3 * D + RW_LORA_W]
    gl = zs[:, 3 * D + RW_LORA_W:]
    xl = jnp.where(_iota(xl.shape, 1) < RW_LORA_W // 2, jnp.tanh(xl), xl)
    xl_hi, xl_lo = _split(xl, 2)
    lora = jnp.dot(jnp.concatenate([xl_hi, xl_lo, xl_hi], axis=1), wc_ref[...], preferred_element_type=F32)
    w = -_softplus(-(w0_ref[...] + lora[:, :D])) - 0.5
    logw = -jnp.exp(w)
    a = _sigmoid(a0_ref[...] + lora[:, D:])
    g = _mm(_sigmoid(gl), gup_ref[...])
    kk = kraw * kk_ref[...]
    k = kraw * (1.0 + (a - 1.0) * ka_ref[...])
    sums = head_sum(jnp.concatenate([kk * kk, r * k * rk_ref[...]], axis=0))
    kk = kk * lax.rsqrt(jnp.maximum(sums[:L], 1e-24))
    bonus = sums[L:]

    tril = (_iota((L, L), 0) >= _iota((L, L), 1)).astype(BF16)
    cl = _sel_left(tril, logw, 3)
    cl_last = cl[L - 1:L, :]
    at = -kk * jnp.exp(cl - logw)
    bt = kk * a * jnp.exp(-cl)
    kt = k * jnp.exp(-cl)
    rt = r * jnp.exp(cl)
    b_end = kk * a * jnp.exp(cl_last - cl)
    k_end = k * jnp.exp(cl_last - cl)
    stack = lambda x: jnp.concatenate([x * hmask[h] for h in range(H)], axis=0)

    n = _mm_nt(stack(at), bt).reshape(H, L, L)
    n = jnp.where(_iota((H, L, L), 1) > _iota((H, L, L), 2), n, 0.0)
    tinv = n + (_iota((H, L, L), 1) == _iota((H, L, L), 2)).astype(F32)
    p = n
    for _ in range(max(L.bit_length() - 2, 0)):
        p = _bmm3(p, p)
        tinv = tinv + _bmm3(tinv, p)

    v_stack = stack(v)
    att = _mm_nt(jnp.concatenate([at, rt], axis=0), jnp.concatenate([stack(bt), stack(kt)], axis=0))
    s_col = _iota((L, H * L), 1) & (L - 1)
    a_ak = jnp.where(s_col < _iota((L, H * L), 0), att[:L, H * L:], 0.0)
    incl = (_iota((L, 2 * H * L), 1) & (L - 1)) <= _iota((L, 2 * H * L), 0)
    a_r = jnp.where(incl, att[L:, :], 0.0)

    G = g_scr[...]
    from_state = _mm_nt(jnp.concatenate([at, rt], axis=0), G)
    wmat = from_state[:L] + _mm(a_ak, v_stack)
    wmat_b = wmat.astype(BF16)
    u = hmask[0] * _mm(tinv[0], wmat_b)
    for h in range(1, H):
        u = u + hmask[h] * _mm(tinv[h], wmat_b)
    y = from_state[L:] + _mm(a_r, jnp.concatenate([stack(u), v_stack], axis=0))
    upd = _mm_tn(jnp.concatenate([u, v], axis=0), jnp.concatenate([b_end, k_end], axis=0))
    g_new = G * jnp.exp(cl_last) + jnp.where(same_head, upd, 0.0)
    g_scr[...] = g_new

    mean = head_sum(y) * (1.0 / RW_HD)
    yc = y - mean
    var = head_sum(yc * yc) * (1.0 / RW_HD)
    yn = yc * lax.rsqrt(var + RW_GN_EPS) * gng_ref[...] + gnb_ref[...]
    y_ref[0] = (yn + bonus * v) * g

    @pl.when(c == nc - 1)
    def _():
        gather = ((_iota((D, RW_HD), 0) & (RW_HD - 1)) == _iota((D, RW_HD), 1)).astype(BF16)
        sout_ref[0] = _sel_right(g_new, gather, 3)
        shout_ref[0] = z[L - 1:L, :]


def _rwkv(z, shift0, s0, params, L):
    B, T, _ = z.shape
    nc = T // L
    per_b = lambda shp: pl.BlockSpec((1,) + shp, lambda b, c: (b, 0, 0))
    return pl.pallas_call(
        functools.partial(_rwkv_body, L, nc),
        out_shape=(jax.ShapeDtypeStruct((B, T, D_RW), F32),
                   jax.ShapeDtypeStruct((B, D_RW, RW_HD), F32),
                   jax.ShapeDtypeStruct((B, 1, RW_SHIFT_W), F32)),
        grid=(B, nc),
        in_specs=[pl.BlockSpec((1, L, RW_SHIFT_W), lambda b, c: (b, c, 0)),
                  per_b((1, RW_SHIFT_W)), per_b((D_RW, RW_HD))] + [_const_spec(p.shape) for p in params],
        out_specs=(pl.BlockSpec((1, L, D_RW), lambda b, c: (b, c, 0)),
                   per_b((D_RW, RW_HD)), per_b((1, RW_SHIFT_W))),
        scratch_shapes=[pltpu.VMEM((D_RW, D_RW), F32), pltpu.VMEM((1, RW_SHIFT_W), F32)],
        compiler_params=pltpu.CompilerParams(dimension_semantics=("arbitrary", "arbitrary"),
                                             vmem_limit_bytes=VMEM_LIMIT),
        name="rwkv",
    )(z, shift0, s0, *params)


def _mlstm_body(L, nc, n_pad, zm_ref, zif_ref, cv0_ref, c0_ref, nm0_ref, cw_ref, cb_ref, ifb_ref, gn_ref,
                y_ref, cout_ref, nmout_ref, cvout_ref, c_scr, nm_scr, cv_scr):
    c = pl.program_id(1)
    H, HD, D = ML_HEADS, ML_HD, D_ML

    @pl.when(c == 0)
    def _():
        cv_scr[...] = cv0_ref[0]
        c_scr[...] = c0_ref[0]
        nm_scr[...] = nm0_ref[0]

    zm = zm_ref[0]
    raw = zm[:, :2 * D]
    v = zm[:, 2 * D:3 * D]
    o = zm[:, 3 * D:]
    ext = jnp.concatenate([cv_scr[...], raw], axis=0)
    cv_scr[...] = ext[L:, :]
    cw = cw_ref[...]
    qk = cb_ref[...] + raw * cw[CONV_W - 1:CONV_W, :]
    for s in range(1, CONV_W):
        qk = qk + pltpu.roll(ext, s, 0)[SUBLANES:, :] * cw[CONV_W - 1 - s:CONV_W - s, :]
    qk = qk * _sigmoid(qk)
    q = qk[:, :D]
    k = qk[:, D:] * (HD ** -0.5)

    gi = zif_ref[0] + ifb_ref[...]
    lane = _iota(gi.shape, 1)
    lf = jnp.minimum(gi, 0.0) - jnp.log(1.0 + jnp.exp(-jnp.abs(gi)))
    gcol = jnp.where(lane < H, gi, jnp.where(lane < 2 * H, lf, 0.0))
    if n_pad:
        pad_col = (c * L + _iota((L, 1), 0)) < n_pad
        pad_row = (c * L + _iota((1, L), 1)) < n_pad
        gcol = jnp.where(pad_col, 0.0, gcol)
    grow = gcol.T
    causal = _iota((L, L), 0) >= _iota((L, L), 1)
    tril = causal.astype(BF16)
    if n_pad:
        causal = causal & jnp.logical_not(pad_row)
    b_col = _sel_left(tril, gcol, 3)
    b_row = b_col.T

    nm = nm_scr[...]
    outs = []
    m_next = nm[H:H + 1, :]
    for h in range(H):
        sl = slice(h * HD, (h + 1) * HD)
        qh, kh, vh = q[:, sl], k[:, sl], v[:, sl]
        ch = c_scr[sl, :]
        nh = nm[h:h + 1, :]
        m_prev = nm[H:H + 1, h:h + 1]
        bc = b_col[:, H + h:H + h + 1]
        ic = gcol[:, h:h + 1]
        br = b_row[H + h:H + h + 1, :]
        ir = grow[h:h + 1, :]
        dlog = jnp.where(causal, bc - br + ir, -jnp.inf)
        inter = bc + m_prev
        m_t = jnp.maximum(inter, jnp.max(dlog, axis=-1, keepdims=True))
        amat = jnp.exp(dlog - m_t)
        sc = jnp.exp(inter - m_t)
        aqk = amat * _mm_nt(qh, kh)
        num = _mm(aqk, vh) + sc * _mm_nt(qh, ch)
        den = jnp.sum(aqk, axis=-1, keepdims=True) + sc * jnp.sum(qh * nh, axis=-1, keepdims=True)
        hh = num / jnp.maximum(jnp.abs(den), jnp.exp(-m_t))
        m_new = m_t[L - 1:L, :]
        b_last = bc[L - 1:L, :]
        wc = jnp.exp(b_last - bc + ic - m_new)
        if n_pad:
            wc = jnp.where(pad_col, 0.0, wc)
        dec = jnp.exp(b_last + m_prev - m_new)
        c_scr[sl, :] = dec * ch + _mm_tn(vh * wc, kh)
        nm_scr[h:h + 1, :] = dec * nh + jnp.sum(kh * wc, axis=0, keepdims=True)
        m_next = jnp.where(_iota(m_next.shape, 1) == h, m_new, m_next)
        mu = jnp.mean(hh, axis=-1, keepdims=True)
        xc = hh - mu
        var = jnp.mean(xc * xc, axis=-1, keepdims=True)
        outs.append(xc * lax.rsqrt(var + ML_GN_EPS) * gn_ref[:, sl] * _sigmoid(o[:, sl]))
    nm_scr[H:H + 1, :] = m_next
    y_ref[0] = jnp.concatenate(outs, axis=1)

    @pl.when(c == nc - 1)
    def _():
        cout_ref[0] = c_scr[...]
        nmout_ref[0] = nm_scr[...]
        cvout_ref[0] = cv_scr[...]


def _mlstm(zm, zif, cv0, c0, nm0, params, L, n_pad):
    B, T, _ = zm.shape
    nc = T // L
    per_b = lambda shp: pl.BlockSpec((1,) + shp, lambda b, c: (b, 0, 0))
    chunk = lambda w: pl.BlockSpec((1, L, w), lambda b, c: (b, c, 0))
    state_shapes = ((SUBLANES, 2 * D_ML), (D_ML, ML_HD), (SUBLANES, ML_HD))
    return pl.pallas_call(
        functools.partial(_mlstm_body, L, nc, n_pad),
        out_shape=(jax.ShapeDtypeStruct((B, T, D_ML), F32),
                   jax.ShapeDtypeStruct((B,) + state_shapes[1], F32),
                   jax.ShapeDtypeStruct((B,) + state_shapes[2], F32),
                   jax.ShapeDtypeStruct((B,) + state_shapes[0], F32)),
        grid=(B, nc),
        in_specs=[chunk(ML_MAIN_W), chunk(GATE_W), per_b(state_shapes[0]), per_b(state_shapes[1]),
                  per_b(state_shapes[2])] + [_const_spec(p.shape) for p in params],
        out_specs=(chunk(D_ML), per_b(state_shapes[1]), per_b(state_shapes[2]), per_b(state_shapes[0])),
        scratch_shapes=[pltpu.VMEM(state_shapes[1], F32), pltpu.VMEM(state_shapes[2], F32),
                        pltpu.VMEM(state_shapes[0], F32)],
        compiler_params=pltpu.CompilerParams(dimension_semantics=("arbitrary", "arbitrary"),
                                             vmem_limit_bytes=VMEM_LIMIT),
        name="mlstm",
    )(zm, zif, cv0, c0, nm0, *params)


def _merge_body(x_ref, ya_ref, yb_ref, pre_ref, post_ref, wg_ref, pa_ref, pb_ref, wo_ref, h_ref):
    x = x_ref[0]
    u = _rms(x, pre_ref[...]).astype(BF16)
    gate = _sigmoid(jnp.dot(u, wg_ref[...], preferred_element_type=F32))
    pa = jnp.dot(ya_ref[0].astype(BF16), pa_ref[...], preferred_element_type=F32)
    pb = jnp.dot(yb_ref[0].astype(BF16), pb_ref[...], preferred_element_type=F32)
    merged = gate[:, :D_MODEL] * pa + gate[:, D_MODEL:] * pb
    o = jnp.dot(merged.astype(BF16), wo_ref[...], preferred_element_type=F32)
    h_ref[0] = x + _rms(o, post_ref[...])


def _merge(x, ya, yb, pre, post, wg, pa, pb, wo, tm):
    B, T, _ = x.shape
    row = lambda w: pl.BlockSpec((1, tm, w), lambda b, i: (b, i, 0))
    consts = (pre, post, wg, pa, pb, wo)
    return pl.pallas_call(
        _merge_body,
        out_shape=jax.ShapeDtypeStruct((B, T, D_MODEL), F32),
        grid=(B, T // tm),
        in_specs=[row(D_MODEL), row(D_RW), row(D_ML)] + [_const_spec(p.shape) for p in consts],
        out_specs=row(D_MODEL),
        compiler_params=pltpu.CompilerParams(dimension_semantics=("arbitrary", "arbitrary"),
                                             vmem_limit_bytes=VMEM_LIMIT),
        name="merge",
    )(x, ya, yb, *consts)


FF_SPLIT = 4


def _ffn_body(tm, n_pad, h_ref, pre_ref, post_ref, wu_ref, wd_ref, o_ref):
    h = h_ref[0]
    u = _rms(h, pre_ref[...]).astype(BF16)
    step = D_FF // FF_SPLIT
    f = None
    for j in range(FF_SPLIT):
        t = jnp.maximum(jnp.dot(u, wu_ref[:, j * step:(j + 1) * step], preferred_element_type=F32), 0.0)
        part = jnp.dot((t * t).astype(BF16), wd_ref[j * step:(j + 1) * step, :], preferred_element_type=F32)
        f = part if f is None else f + part
    out = h + _rms(f, post_ref[...])
    if n_pad:
        out = jnp.where(pl.program_id(1) * tm + _iota((tm, 1), 0) < n_pad, 0.0, out)
    o_ref[0] = out


def _ffn(h, pre, post, wu, wd, tm, n_pad):
    B, T, _ = h.shape
    row = pl.BlockSpec((1, tm, D_MODEL), lambda b, i: (b, i, 0))
    consts = (pre, post, wu, wd)
    return pl.pallas_call(
        functools.partial(_ffn_body, tm, n_pad),
        out_shape=jax.ShapeDtypeStruct((B, T, D_MODEL), F32),
        grid=(B, T // tm),
        in_specs=[row] + [_const_spec(p.shape) for p in consts],
        out_specs=row,
        compiler_params=pltpu.CompilerParams(dimension_semantics=("arbitrary", "arbitrary"),
                                             vmem_limit_bytes=VMEM_LIMIT),
        name="ffn",
    )(h, *consts)


def _row_tile(T):
    for tm in (704, 512, 256, 128, 64, 32, 16, 8):
        if T % tm == 0:
            return tm
    raise ValueError(f"unsupported sequence length {T}")


def _layer(x, st, lp, L, n_pad):
    B, T, _ = x.shape
    S0, sh0, C0, n0, m0, cb0 = st
    flat = (lambda a: a.reshape(1, B * T, a.shape[-1])) if not n_pad else (lambda a: a)
    unflat = lambda a: a.reshape(B, T, a.shape[-1])
    xf = flat(x)
    tm = _row_tile(xf.shape[1])
    z_rw, z_ml, z_if = map(unflat, _in_proj(xf, lp['pre1'], lp['w_rw'], lp['w_ml'], lp['w_if'], tm))

    ya, S1, sh1 = _rwkv(z_rw, sh0[:, None, :], S0.reshape(B, D_RW, RW_HD), lp['rw_params'], L)

    cv0 = jnp.pad(cb0, ((0, 0), (SUBLANES - (CONV_W - 1), 0), (0, 0)))
    nm0 = jnp.concatenate([n0, jnp.pad(m0, ((0, 0), (0, ML_HD - ML_HEADS)))[:, None, :],
                           jnp.zeros((B, SUBLANES - ML_HEADS - 1, ML_HD), F32)], axis=1)
    yb, C1, nm1, cv1 = _mlstm(z_ml, z_if, cv0, C0.reshape(B, D_ML, ML_HD), nm0, lp['ml_params'], L, n_pad)

    h = _merge(xf, flat(ya), flat(yb), lp['pre1'], lp['post1'], lp['w_gate'], lp['p_a'], lp['p_b'], lp['w_out'], tm)
    out = unflat(_ffn(h, lp['pre2'], lp['post2'], lp['w_ff_up'], lp['w_ff_down'], tm, n_pad))
    new_state = (S1.reshape(B, RW_HEADS, RW_HD, RW_HD), sh1[:, 0, :], C1.reshape(B, ML_HEADS, ML_HD, ML_HD),
                 nm1[:, :ML_HEADS, :], nm1[:, ML_HEADS, :ML_HEADS], cv1[:, SUBLANES - (CONV_W - 1):, :])
    return out, new_state


def _layer_params(l, w_in, rw_mu, rw_w0, rw_w_up, rw_a0, rw_a_up, rw_g_up, rw_k_k, rw_k_a, rw_r_k, rw_gn_g,
                  rw_gn_b, ml_conv_w, ml_conv_b, ml_i_bias, ml_f_bias, ml_gn_g, p_a, p_b, w_out, pre1, post1,
                  pre2, post2, w_ff_up, w_ff_down):
    row = lambda a: a[l].reshape(1, -1).astype(F32)
    w = w_in[l]
    c_ml = RW_SHIFT_W
    c_if = c_ml + ML_MAIN_W
    c_gate = c_if + 2 * ML_HEADS
    half = RW_LORA_W // 2
    lora = jnp.zeros((RW_LORA_W, 2 * D_RW), F32)
    lora = lora.at[:half, :D_RW].set(rw_w_up[l]).at[half:, D_RW:].set(rw_a_up[l])
    lora_hi = lora.astype(BF16)
    lora_lo = (lora - lora_hi.astype(F32)).astype(BF16)
    lora = jnp.concatenate([lora_hi, lora_hi, lora_lo], axis=0)
    if_bias = jnp.zeros((1, GATE_W), F32)
    if_bias = if_bias.at[0, :ML_HEADS].set(ml_i_bias[l]).at[0, ML_HEADS:2 * ML_HEADS].set(ml_f_bias[l])
    return dict(
        pre1=row(pre1), post1=row(post1), pre2=row(pre2), post2=row(post2),
        w_rw=w[:, :c_ml].astype(BF16),
        w_ml=w[:, c_ml:c_if].astype(BF16),
        w_if=jnp.pad(w[:, c_if:c_gate], ((0, 0), (0, GATE_W - 2 * ML_HEADS))).astype(BF16),
        w_gate=w[:, c_gate:].astype(BF16),
        rw_params=(row(rw_mu), row(rw_w0), row(rw_a0), lora, rw_g_up[l].astype(BF16), row(rw_k_k), row(rw_k_a),
                   row(rw_r_k), row(rw_gn_g), row(rw_gn_b)),
        ml_params=(ml_conv_w[l].astype(F32), row(ml_conv_b), if_bias, row(ml_gn_g)),
        p_a=p_a[l].astype(BF16), p_b=p_b[l].astype(BF16), w_out=w_out[l].astype(BF16),
        w_ff_up=w_ff_up[l].astype(BF16), w_ff_down=w_ff_down[l].astype(BF16),
    )


def kernel(x_prompt, x_sample, state_rwkv_S, state_rwkv_shift, state_mlstm_C, state_mlstm_n, state_mlstm_m,
           state_mlstm_conv, meta_tokens, w_in, rw_mu, rw_w0, rw_w_up, rw_a0, rw_a_up, rw_g_up, rw_k_k, rw_k_a,
           rw_r_k, rw_gn_g, rw_gn_b, ml_conv_w, ml_conv_b, ml_i_bias, ml_f_bias, ml_gn_g, p_a, p_b, w_out, pre1,
           post1, pre2, post2, w_ff_up, w_ff_down):
    B, T, _ = x_prompt.shape
    dt = x_prompt.dtype
    depth = w_in.shape[0]
    L = PROMPT_CHUNK
    n_pad = (-(T + N_META)) % L
    xp = jnp.concatenate([jnp.zeros((B, n_pad, D_MODEL), dt),
                          jnp.broadcast_to(meta_tokens[None].astype(dt), (B, N_META, D_MODEL)), x_prompt], axis=1)
    xs = x_sample
    Bs, Ts, _ = xs.shape
    zero_state = (jnp.zeros((B, RW_HEADS, RW_HD, RW_HD), F32), jnp.zeros((B, RW_SHIFT_W), F32),
                  jnp.zeros((B, ML_HEADS, ML_HD, ML_HD), F32), jnp.zeros((B, ML_HEADS, ML_HD), F32),
                  jnp.zeros((B, ML_HEADS), F32), jnp.zeros((B, CONV_W - 1, 2 * D_ML), F32))
    p_states, s_states = [], []
    for l in range(depth):
        lp = _layer_params(l, w_in, rw_mu, rw_w0, rw_w_up, rw_a0, rw_a_up, rw_g_up, rw_k_k, rw_k_a, rw_r_k,
                           rw_gn_g, rw_gn_b, ml_conv_w, ml_conv_b, ml_i_bias, ml_f_bias, ml_gn_g, p_a, p_b, w_out,
                           pre1, post1, pre2, post2, w_ff_up, w_ff_down)
        xp, st_p = _layer(xp, zero_state, lp, L, n_pad)
        st_in = (state_rwkv_S[l], state_rwkv_shift[l], state_mlstm_C[l], state_mlstm_n[l], state_mlstm_m[l],
                 state_mlstm_conv[l])
        xs, st_s = _layer(xs, st_in, lp, Ts, 0)
        p_states.append(st_p)
        s_states.append(st_s)
    stk = lambda lst, i: jnp.stack([s[i] for s in lst]).astype(dt)
    y_prompt = xp[:, n_pad + N_META:]
    return (y_prompt, xs,
            stk(p_states, 0), stk(p_states, 1), stk(p_states, 2), stk(p_states, 3), stk(p_states, 4), stk(p_states, 5),
            stk(s_states, 0), stk(s_states, 1), stk(s_states, 2), stk(s_states, 3), stk(s_states, 4), stk(s_states, 5))
```

```python
import functools

import jax
import jax.numpy as jnp
from jax import lax
from jax.experimental import pallas as pl
from jax.experimental.pallas import tpu as pltpu

F32 = jnp.float32
BF16 = jnp.bfloat16

D_MODEL = 1024
N_META = 16
RW_HEADS = 8
RW_HD = 64
D_RW = RW_HEADS * RW_HD
RW_HALF_HEADS = RW_HEADS // 2
RW_HALF = RW_HALF_HEADS * RW_HD
RW_LORA_W = 128
RW_G_LORA = 128
RW_SHIFT_W = 3 * D_RW + RW_LORA_W + RW_G_LORA
RW_GN_EPS = 64e-5
ML_HEADS = 4
ML_HD = 128
D_ML = ML_HEADS * ML_HD
CONV_W = 4
ML_MAIN_W = 4 * D_ML
ML_GN_EPS = 1e-5
GATE_W = 128
D_FF = 4 * D_MODEL
RMS_EPS = 1e-6
SUBLANES = 8

PROMPT_CHUNK = 64
VMEM_LIMIT = 56 * 2**20
RW_SEQS_LONG, RW_SEQS_SHORT = 8, 16
ML_SEQS_LONG, ML_SEQS_SHORT = 8, 16


def _bmm_nn(a, b):
    return jnp.einsum('gmk,gkn->gmn', a.astype(BF16), b.astype(BF16), preferred_element_type=F32)


def _bmm_nt(a, b):
    return jnp.einsum('gmk,gnk->gmn', a.astype(BF16), b.astype(BF16), preferred_element_type=F32)


def _bmm_tn(a, b):
    return jnp.einsum('gkm,gkn->gmn', a.astype(BF16), b.astype(BF16), preferred_element_type=F32)


def _bmm3(a, b):
    m = a.shape[1]
    a_hi, a_lo = _split(a, 2)
    b_hi, b_lo = _split(b, 2)
    r = jnp.einsum('gmk,gkn->gmn', jnp.concatenate([a_hi, a_lo], axis=1), b_hi, preferred_element_type=F32)
    return r[:, :m] + r[:, m:] + jnp.einsum('gmk,gkn->gmn', a_hi, b_lo, preferred_element_type=F32)


def _split(x, terms):
    parts = []
    for _ in range(terms - 1):
        p = x.astype(BF16)
        parts.append(p)
        x = x - p.astype(F32)
    parts.append(x.astype(BF16))
    return parts


def _sel_right(x, sel, terms):
    m = x.shape[0]
    pieces = _split(x, terms)
    r = jnp.dot(jnp.concatenate(pieces, axis=0) if terms > 1 else pieces[0], sel, preferred_element_type=F32)
    return sum(r[i * m:(i + 1) * m] for i in range(terms))


def _cumsum_rows(x):
    row = _iota(x.shape, 1)
    s = 1
    while s < x.shape[1]:
        x = x + jnp.where(row >= s, pltpu.roll(x, s, 1), 0.0)
        s *= 2
    return x


def _sigmoid(x):
    return 1.0 / (1.0 + jnp.exp(-x))


def _softplus(x):
    return jnp.maximum(x, 0.0) + jnp.log(1.0 + jnp.exp(-jnp.abs(x)))


def _iota(shape, dim):
    return lax.broadcasted_iota(jnp.int32, shape, dim)


def _rms(x, g):
    return x * lax.rsqrt(jnp.mean(x * x, axis=-1, keepdims=True) + RMS_EPS) * g


def _const_spec(shape):
    nd = len(shape)
    return pl.BlockSpec(shape, lambda *_: (0,) * nd, pipeline_mode=pl.Buffered(1))


def _group(n, target):
    return max(d for d in range(1, target + 1) if n % d == 0)


def _in_proj_body(x_ref, g_ref, wrw_ref, wml_ref, wif_ref, zrw_ref, zml_ref, zif_ref):
    u = _rms(x_ref[0], g_ref[...]).astype(BF16)
    zrw_ref[0] = jnp.dot(u, wrw_ref[...], preferred_element_type=F32)
    zml_ref[0] = jnp.dot(u, wml_ref[...], preferred_element_type=F32)
    zif_ref[0] = jnp.dot(u, wif_ref[...], preferred_element_type=F32)


def _in_proj(x, g, w_rw, w_ml, w_if, tm):
    B, T, _ = x.shape
    row = lambda w: pl.BlockSpec((1, tm, w), lambda b, i: (b, i, 0))
    return pl.pallas_call(
        _in_proj_body,
        out_shape=(jax.ShapeDtypeStruct((B, T, RW_SHIFT_W), F32),
                   jax.ShapeDtypeStruct((B, T, ML_MAIN_W), F32),
                   jax.ShapeDtypeStruct((B, T, GATE_W), F32)),
        grid=(B, T // tm),
        in_specs=[row(D_MODEL), _const_spec((1, D_MODEL)), _const_spec(w_rw.shape), _const_spec(w_ml.shape),
                  _const_spec(w_if.shape)],
        out_specs=(row(RW_SHIFT_W), row(ML_MAIN_W), row(GATE_W)),
        compiler_params=pltpu.CompilerParams(dimension_semantics=("arbitrary", "arbitrary"),
                                             vmem_limit_bytes=VMEM_LIMIT),
        name="in_proj",
    )(x, g, w_rw, w_ml, w_if)


def _rwkv_chunk(L, at, bt, kt, rt, b_end, k_end, v, decay_end, S, bd32):
    H = RW_HALF_HEADS
    g = at.shape[0]
    lane_head = _iota((1, 1, RW_HALF), 2) >> 6
    hmask = [(lane_head == h).astype(F32) for h in range(H)]
    stack = lambda x: jnp.concatenate([x * hmask[h] for h in range(H)], axis=1).astype(BF16)

    n = _bmm_nt(stack(at), bt).reshape(g * H, L, L)
    n = jnp.where(_iota((1, L, L), 1) > _iota((1, L, L), 2), n, 0.0)
    tinv = n + (_iota((1, L, L), 1) == _iota((1, L, L), 2)).astype(F32)
    p = n
    for _ in range(max(L.bit_length() - 2, 0)):
        p = _bmm3(p, p)
        tinv = tinv + _bmm3(tinv, p)
    tinv = tinv.reshape(g, H, L, L)

    v_stack = stack(v)
    ar = jnp.concatenate([at, rt], axis=1).astype(BF16)
    att = _bmm_nt(ar, jnp.concatenate([stack(bt), stack(kt)], axis=1))
    s_col = _iota((1, L, H * L), 2) & (L - 1)
    a_ak = jnp.where(s_col < _iota((1, L, H * L), 1), att[:, :L, H * L:], 0.0)
    incl = (_iota((1, L, 2 * H * L), 2) & (L - 1)) <= _iota((1, L, 2 * H * L), 1)
    a_r = jnp.where(incl, att[:, L:, :], 0.0)

    from_state = _bmm_nt(ar, S)
    wmat = (from_state[:, :L] + _bmm_nn(a_ak, v_stack)).astype(BF16)
    u = hmask[0] * _bmm_nn(tinv[:, 0], wmat)
    for h in range(1, H):
        u = u + hmask[h] * _bmm_nn(tinv[:, h], wmat)
    y = from_state[:, L:] + _bmm_nn(a_r, jnp.concatenate([stack(u), v_stack], axis=1))
    upd = _bmm_tn(jnp.concatenate([u, v], axis=1), jnp.concatenate([b_end, k_end], axis=1))
    return y, S * decay_end + upd * bd32


def _rwkv_body(L, nc, nb, z_ref, sh0_ref, s0_ref, mu_ref, w0_ref, a0_ref, wc_ref, gup_ref, kk_ref, ka_ref, rk_ref,
               gng_ref, gnb_ref, bd16_ref, bd32_ref, y_ref, sout_ref, shout_ref, s_scr, prev_scr):
    c = pl.program_id(1)
    D, DH = D_RW, RW_HALF
    bd16 = bd16_ref[...]
    bd32 = bd32_ref[...]

    def head_sum(x, terms):
        m = x.shape[1]
        x2 = x.reshape(nb * m, D)
        r = jnp.concatenate([_sel_right(x2[:, hf * DH:(hf + 1) * DH], bd16, terms) for hf in range(2)], axis=1)
        return r.reshape(nb, m, D)

    halves = lambda x: jnp.stack([x[:, :, :DH], x[:, :, DH:]], axis=1).reshape(2 * nb, x.shape[1], DH)

    @pl.when(c == 0)
    def _():
        prev_scr[...] = sh0_ref[...]
        spread = (_iota((RW_HD, DH), 0) == (_iota((RW_HD, DH), 1) & (RW_HD - 1))).astype(BF16)
        s0 = _sel_right(s0_ref[...].reshape(nb * D, RW_HD), spread, 3)
        s_scr[...] = s0.reshape(2 * nb, DH, DH) * bd32

    z = z_ref[...]
    prev = jnp.where(_iota(z.shape, 1) == 0, prev_scr[...], pltpu.roll(z, 1, 1))
    prev_scr[...] = z[:, L - 1:L, :]
    zs = z + (prev - z) * mu_ref[...]
    r = zs[:, :, 0:D]
    kraw = zs[:, :, D:2 * D]
    v = zs[:, :, 2 * D:3 * D]
    xl = zs[:, :, 3 * D:3 * D + RW_LORA_W]
    gl = zs[:, :, 3 * D + RW_LORA_W:]
    xl = jnp.where(_iota(xl.shape, 2) < RW_LORA_W // 2, jnp.tanh(xl), xl)
    xl_hi, xl_lo = _split(xl, 2)
    lora = jnp.dot(jnp.concatenate([xl_hi, xl_lo, xl_hi], axis=2).reshape(nb * L, 3 * RW_LORA_W), wc_ref[...],
                   preferred_element_type=F32).reshape(nb, L, 2 * D)
    w = -_softplus(-(w0_ref[...] + lora[:, :, :D])) - 0.5
    logw = -jnp.exp(w)
    a = _sigmoid(a0_ref[...] + lora[:, :, D:])
    g = jnp.dot(_sigmoid(gl).reshape(nb * L, RW_G_LORA).astype(BF16), gup_ref[...],
                preferred_element_type=F32).reshape(nb, L, D)
    kk = kraw * kk_ref[...]
    k = kraw * (1.0 + (a - 1.0) * ka_ref[...])
    sums = head_sum(jnp.concatenate([kk * kk, r * k * rk_ref[...]], axis=1), 1)
    kk = kk * lax.rsqrt(jnp.maximum(sums[:, :L], 1e-24))
    bonus = sums[:, L:]

    cl = _cumsum_rows(logw)
    cl_last = cl[:, L - 1:L, :]
    e_neg = jnp.exp(-cl)
    e_end = jnp.exp(cl_last - cl)
    kka = kk * a
    y_g, s_new = _rwkv_chunk(L, halves(-kk * jnp.exp(cl - logw)), halves(kka * e_neg), halves(k * e_neg),
                             halves(r * jnp.exp(cl)), halves(kka * e_end), halves(k * e_end), halves(v),
                             halves(jnp.exp(cl_last)), s_scr[...], bd32)
    s_scr[...] = s_new
    y_g = y_g.reshape(nb, 2, L, DH)
    y = jnp.concatenate([y_g[:, 0], y_g[:, 1]], axis=2)

    stats = head_sum(jnp.concatenate([y, y * y], axis=1), 2) * (1.0 / RW_HD)
    mean = stats[:, :L]
    var = stats[:, L:] - mean * mean
    yn = (y - mean) * lax.rsqrt(var + RW_GN_EPS) * gng_ref[...] + gnb_ref[...]
    y_ref[...] = (yn + bonus * v) * g

    @pl.when(c == nc - 1)
    def _():
        gather = ((_iota((DH, RW_HD), 0) & (RW_HD - 1)) == _iota((DH, RW_HD), 1)).astype(BF16)
        sout_ref[...] = _sel_right(s_new.reshape(2 * nb * DH, DH), gather, 3).reshape(nb, D, RW_HD)
        shout_ref[...] = z[:, L - 1:L, :]


def _rwkv(z, shift0, s0, params, L, nb):
    B, T, _ = z.shape
    nc = T // L
    per_b = lambda shp: pl.BlockSpec((nb,) + shp, lambda b, c: (b, 0, 0))
    bd = (_iota((RW_HALF, RW_HALF), 0) >> 6) == (_iota((RW_HALF, RW_HALF), 1) >> 6)
    consts = tuple(params) + (bd.astype(BF16), bd.astype(F32))
    return pl.pallas_call(
        functools.partial(_rwkv_body, L, nc, nb),
        out_shape=(jax.ShapeDtypeStruct((B, T, D_RW), F32),
                   jax.ShapeDtypeStruct((B, D_RW, RW_HD), F32),
                   jax.ShapeDtypeStruct((B, 1, RW_SHIFT_W), F32)),
        grid=(B // nb, nc),
        in_specs=[pl.BlockSpec((nb, L, RW_SHIFT_W), lambda b, c: (b, c, 0)),
                  per_b((1, RW_SHIFT_W)), per_b((D_RW, RW_HD))] + [_const_spec(p.shape) for p in consts],
        out_specs=(pl.BlockSpec((nb, L, D_RW), lambda b, c: (b, c, 0)),
                   per_b((D_RW, RW_HD)), per_b((1, RW_SHIFT_W))),
        scratch_shapes=[pltpu.VMEM((2 * nb, RW_HALF, RW_HALF), F32), pltpu.VMEM((nb, 1, RW_SHIFT_W), F32)],
        compiler_params=pltpu.CompilerParams(dimension_semantics=("arbitrary", "arbitrary"),
                                             vmem_limit_bytes=VMEM_LIMIT),
        name="rwkv",
    )(z, shift0, s0, *consts)


def _mlstm_body(L, nc, nb, n_pad, zm_ref, zif_ref, cv0_ref, c0_ref, nm0_ref, cw_ref, cb_ref, ifb_ref, gn_ref,
                y_ref, cout_ref, nmout_ref, cvout_ref, c_scr, nm_scr, cv_scr):
    c = pl.program_id(1)
    H, HD, D = ML_HEADS, ML_HD, D_ML
    G = nb * H

    @pl.when(c == 0)
    def _():
        cv_scr[...] = cv0_ref[...]
        c_scr[...] = c0_ref[...]
        nm_scr[...] = nm0_ref[...]

    heads = lambda x: jnp.stack([x[:, :, h * HD:(h + 1) * HD] for h in range(H)], axis=1).reshape(G, x.shape[1], HD)
    lane_pick = lambda x, lo: jnp.stack([x[:, :, lo + h:lo + h + 1] for h in range(H)], axis=1).reshape(G, x.shape[1], 1)
    row_pick = lambda x, lo: jnp.stack([x[:, lo + h:lo + h + 1, :] for h in range(H)], axis=1).reshape(G, 1, x.shape[2])

    zm = zm_ref[...]
    raw = zm[:, :, :2 * D]
    v = zm[:, :, 2 * D:3 * D]
    o = zm[:, :, 3 * D:]
    ext = jnp.concatenate([cv_scr[...], raw], axis=1)
    cv_scr[...] = ext[:, L:, :]
    cw = cw_ref[...]
    qk = cb_ref[...] + raw * cw[CONV_W - 1:CONV_W, :]
    for s in range(1, CONV_W):
        qk = qk + pltpu.roll(ext, s, 1)[:, SUBLANES:, :] * cw[CONV_W - 1 - s:CONV_W - s, :]
    qk = qk * _sigmoid(qk)
    qh = heads(qk[:, :, :D])
    kh = heads(qk[:, :, D:] * (HD ** -0.5))
    vh = heads(v)

    gi = zif_ref[...] + ifb_ref[...]
    lane = _iota(gi.shape, 2)
    lf = jnp.minimum(gi, 0.0) - jnp.log(1.0 + jnp.exp(-jnp.abs(gi)))
    gcol = jnp.where(lane < H, gi, jnp.where(lane < 2 * H, lf, 0.0))
    causal = _iota((1, L, L), 1) >= _iota((1, L, L), 2)
    if n_pad:
        pad_col = (c * L + _iota((1, L, 1), 1)) < n_pad
        gcol = jnp.where(pad_col, 0.0, gcol)
        causal = causal & ((c * L + _iota((1, 1, L), 2)) >= n_pad)
    b_col = _cumsum_rows(gcol)
    grow = jnp.swapaxes(gcol, 1, 2)
    b_row = jnp.swapaxes(b_col, 1, 2)
    bc, ic = lane_pick(b_col, H), lane_pick(gcol, 0)
    br, ir = row_pick(b_row, H), row_pick(grow, 0)

    nm = nm_scr[...]
    ch = c_scr[...].reshape(G, HD, HD)
    nh = row_pick(nm, 0)
    m_prev = lane_pick(nm[:, H:H + 1, :], 0)

    dlog = jnp.where(causal, bc - br + ir, -jnp.inf)
    inter = bc + m_prev
    m_t = jnp.maximum(inter, jnp.max(dlog, axis=-1, keepdims=True))
    amat = jnp.exp(dlog - m_t)
    sc = jnp.exp(inter - m_t)
    aqk = amat * _bmm_nt(qh, kh)
    num = _bmm_nn(aqk, vh) + sc * _bmm_nt(qh, ch)
    den = jnp.sum(aqk, axis=-1, keepdims=True) + sc * jnp.sum(qh * nh, axis=-1, keepdims=True)
    hh = num / jnp.maximum(jnp.abs(den), jnp.exp(-m_t))
    m_new = m_t[:, L - 1:L, :]
    b_last = bc[:, L - 1:L, :]
    wc = jnp.exp(b_last - bc + ic - m_new)
    if n_pad:
        wc = jnp.where(pad_col, 0.0, wc)
    dec = jnp.exp(b_last + m_prev - m_new)
    c_scr[...] = (dec * ch + _bmm_tn(vh * wc, kh)).reshape(nb, D, HD)
    n_new = (dec * nh + jnp.sum(kh * wc, axis=1, keepdims=True)).reshape(nb, H, 1, HD)
    m_new = m_new.reshape(nb, H, 1, 1)
    m_row = jnp.zeros((nb, 1, HD), F32)
    for h in range(H):
        m_row = jnp.where(_iota(m_row.shape, 2) == h, m_new[:, h], m_row)
    nm_scr[...] = jnp.concatenate([n_new[:, h] for h in range(H)] + [m_row, nm[:, H + 1:, :]], axis=1)

    mu = jnp.mean(hh, axis=-1, keepdims=True)
    xc = hh - mu
    var = jnp.mean(xc * xc, axis=-1, keepdims=True)
    gn = jnp.concatenate([gn_ref[:, h * HD:(h + 1) * HD][None] for h in range(H)] * nb, axis=0)
    out = (xc * lax.rsqrt(var + ML_GN_EPS) * gn * _sigmoid(heads(o))).reshape(nb, H, L, HD)
    y_ref[...] = jnp.concatenate([out[:, h] for h in range(H)], axis=2)

    @pl.when(c == nc - 1)
    def _():
        cout_ref[...] = c_scr[...]
        nmout_ref[...] = nm_scr[...]
        cvout_ref[...] = cv_scr[...]


def _mlstm(zm, zif, cv0, c0, nm0, params, L, n_pad, nb):
    B, T, _ = zm.shape
    nc = T // L
    per_b = lambda shp: pl.BlockSpec((nb,) + shp, lambda b, c: (b, 0, 0))
    chunk = lambda w: pl.BlockSpec((nb, L, w), lambda b, c: (b, c, 0))
    state_shapes = ((SUBLANES, 2 * D_ML), (D_ML, ML_HD), (SUBLANES, ML_HD))
    return pl.pallas_call(
        functools.partial(_mlstm_body, L, nc, nb, n_pad),
        out_shape=(jax.ShapeDtypeStruct((B, T, D_ML), F32),
                   jax.ShapeDtypeStruct((B,) + state_shapes[1], F32),
                   jax.ShapeDtypeStruct((B,) + state_shapes[2], F32),
                   jax.ShapeDtypeStruct((B,) + state_shapes[0], F32)),
        grid=(B // nb, nc),
        in_specs=[chunk(ML_MAIN_W), chunk(GATE_W), per_b(state_shapes[0]), per_b(state_shapes[1]),
                  per_b(state_shapes[2])] + [_const_spec(p.shape) for p in params],
        out_specs=(chunk(D_ML), per_b(state_shapes[1]), per_b(state_shapes[2]), per_b(state_shapes[0])),
        scratch_shapes=[pltpu.VMEM((nb,) + state_shapes[1], F32), pltpu.VMEM((nb,) + state_shapes[2], F32),
                        pltpu.VMEM((nb,) + state_shapes[0], F32)],
        compiler_params=pltpu.CompilerParams(dimension_semantics=("arbitrary", "arbitrary"),
                                             vmem_limit_bytes=VMEM_LIMIT),
        name="mlstm",
    )(zm, zif, cv0, c0, nm0, *params)


def _merge_body(x_ref, ya_ref, yb_ref, pre_ref, post_ref, wg_ref, pa_ref, pb_ref, wo_ref, h_ref):
    x = x_ref[0]
    u = _rms(x, pre_ref[...]).astype(BF16)
    gate = _sigmoid(jnp.dot(u, wg_ref[...], preferred_element_type=F32))
    pa = jnp.dot(ya_ref[0].astype(BF16), pa_ref[...], preferred_element_type=F32)
    pb = jnp.dot(yb_ref[0].astype(BF16), pb_ref[...], preferred_element_type=F32)
    merged = gate[:, :D_MODEL] * pa + gate[:, D_MODEL:] * pb
    o = jnp.dot(merged.astype(BF16), wo_ref[...], preferred_element_type=F32)
    h_ref[0] = x + _rms(o, post_ref[...])


def _merge(x, ya, yb, pre, post, wg, pa, pb, wo, tm):
    B, T, _ = x.shape
    row = lambda w: pl.BlockSpec((1, tm, w), lambda b, i: (b, i, 0))
    consts = (pre, post, wg, pa, pb, wo)
    return pl.pallas_call(
        _merge_body,
        out_shape=jax.ShapeDtypeStruct((B, T, D_MODEL), F32),
        grid=(B, T // tm),
        in_specs=[row(D_MODEL), row(D_RW), row(D_ML)] + [_const_spec(p.shape) for p in consts],
        out_specs=row(D_MODEL),
        compiler_params=pltpu.CompilerParams(dimension_semantics=("arbitrary", "arbitrary"),
                                             vmem_limit_bytes=VMEM_LIMIT),
        name="merge",
    )(x, ya, yb, *consts)


FF_SPLIT = 4


def _ffn_body(tm, n_pad, h_ref, pre_ref, post_ref, wu_ref, wd_ref, o_ref):
    h = h_ref[0]
    u = _rms(h, pre_ref[...]).astype(BF16)
    step = D_FF // FF_SPLIT
    f = None
    for j in range(FF_SPLIT):
        t = jnp.maximum(jnp.dot(u, wu_ref[:, j * step:(j + 1) * step], preferred_element_type=F32), 0.0)
        part = jnp.dot((t * t).astype(BF16), wd_ref[j * step:(j + 1) * step, :], preferred_element_type=F32)
        f = part if f is None else f + part
    out = h + _rms(f, post_ref[...])
    if n_pad:
        out = jnp.where(pl.program_id(1) * tm + _iota((tm, 1), 0) < n_pad, 0.0, out)
    o_ref[0] = out


def _ffn(h, pre, post, wu, wd, tm, n_pad):
    B, T, _ = h.shape
    row = pl.BlockSpec((1, tm, D_MODEL), lambda b, i: (b, i, 0))
    consts = (pre, post, wu, wd)
    return pl.pallas_call(
        functools.partial(_ffn_body, tm, n_pad),
        out_shape=jax.ShapeDtypeStruct((B, T, D_MODEL), F32),
        grid=(B, T // tm),
        in_specs=[row] + [_const_spec(p.shape) for p in consts],
        out_specs=row,
        compiler_params=pltpu.CompilerParams(dimension_semantics=("arbitrary", "arbitrary"),
                                             vmem_limit_bytes=VMEM_LIMIT),
        name="ffn",
    )(h, *consts)


def _row_tile(T):
    for tm in (704, 512, 256, 128, 64, 32, 16, 8):
        if T % tm == 0:
            return tm
    raise ValueError(f"unsupported sequence length {T}")


def _layer(x, st, lp, L, n_pad):
    B, T, _ = x.shape
    S0, sh0, C0, n0, m0, cb0 = st
    long_seq = T > L
    flat = (lambda a: a.reshape(1, B * T, a.shape[-1])) if not n_pad else (lambda a: a)
    unflat = lambda a: a.reshape(B, T, a.shape[-1])
    xf = flat(x)
    tm = _row_tile(xf.shape[1])
    z_rw, z_ml, z_if = map(unflat, _in_proj(xf, lp['pre1'], lp['w_rw'], lp['w_ml'], lp['w_if'], tm))

    ya, S1, sh1 = _rwkv(z_rw, sh0[:, None, :], S0.reshape(B, D_RW, RW_HD), lp['rw_params'], L,
                        _group(B, RW_SEQS_LONG if long_seq else RW_SEQS_SHORT))

    cv0 = jnp.pad(cb0, ((0, 0), (SUBLANES - (CONV_W - 1), 0), (0, 0)))
    nm0 = jnp.concatenate([n0, jnp.pad(m0, ((0, 0), (0, ML_HD - ML_HEADS)))[:, None, :],
                           jnp.zeros((B, SUBLANES - ML_HEADS - 1, ML_HD), F32)], axis=1)
    yb, C1, nm1, cv1 = _mlstm(z_ml, z_if, cv0, C0.reshape(B, D_ML, ML_HD), nm0, lp['ml_params'], L, n_pad,
                              _group(B, ML_SEQS_LONG if long_seq else ML_SEQS_SHORT))

    h = _merge(xf, flat(ya), flat(yb), lp['pre1'], lp['post1'], lp['w_gate'], lp['p_a'], lp['p_b'], lp['w_out'], tm)
    out = unflat(_ffn(h, lp['pre2'], lp['post2'], lp['w_ff_up'], lp['w_ff_down'], tm, n_pad))
    new_state = (S1.reshape(B, RW_HEADS, RW_HD, RW_HD), sh1[:, 0, :], C1.reshape(B, ML_HEADS, ML_HD, ML_HD),
                 nm1[:, :ML_HEADS, :], nm1[:, ML_HEADS, :ML_HEADS], cv1[:, SUBLANES - (CONV_W - 1):, :])
    return out, new_state


def _layer_params(l, w_in, rw_mu, rw_w0, rw_w_up, rw_a0, rw_a_up, rw_g_up, rw_k_k, rw_k_a, rw_r_k, rw_gn_g,
                  rw_gn_b, ml_conv_w, ml_conv_b, ml_i_bias, ml_f_bias, ml_gn_g, p_a, p_b, w_out, pre1, post1,
                  pre2, post2, w_ff_up, w_ff_down):
    row = lambda a: a[l].reshape(1, -1).astype(F32)
    w = w_in[l]
    c_ml = RW_SHIFT_W
    c_if = c_ml + ML_MAIN_W
    c_gate = c_if + 2 * ML_HEADS
    half = RW_LORA_W // 2
    lora = jnp.zeros((RW_LORA_W, 2 * D_RW), F32)
    lora = lora.at[:half, :D_RW].set(rw_w_up[l]).at[half:, D_RW:].set(rw_a_up[l])
    lora_hi = lora.astype(BF16)
    lora_lo = (lora - lora_hi.astype(F32)).astype(BF16)
    lora = jnp.concatenate([lora_hi, lora_hi, lora_lo], axis=0)
    if_bias = jnp.zeros((1, GATE_W), F32)
    if_bias = if_bias.at[0, :ML_HEADS].set(ml_i_bias[l]).at[0, ML_HEADS:2 * ML_HEADS].set(ml_f_bias[l])
    return dict(
        pre1=row(pre1), post1=row(post1), pre2=row(pre2), post2=row(post2),
        w_rw=w[:, :c_ml].astype(BF16),
        w_ml=w[:, c_ml:c_if].astype(BF16),
        w_if=jnp.pad(w[:, c_if:c_gate], ((0, 0), (0, GATE_W - 2 * ML_HEADS))).astype(BF16),
        w_gate=w[:, c_gate:].astype(BF16),
        rw_params=(row(rw_mu), row(rw_w0), row(rw_a0), lora, rw_g_up[l].astype(BF16), row(rw_k_k), row(rw_k_a),
                   row(rw_r_k), row(rw_gn_g), row(rw_gn_b)),
        ml_params=(ml_conv_w[l].astype(F32), row(ml_conv_b), if_bias, row(ml_gn_g)),
        p_a=p_a[l].astype(BF16), p_b=p_b[l].astype(BF16), w_out=w_out[l].astype(BF16),
        w_ff_up=w_ff_up[l].astype(BF16), w_ff_down=w_ff_down[l].astype(BF16),
    )


def kernel(x_prompt, x_sample, state_rwkv_S, state_rwkv_shift, state_mlstm_C, state_mlstm_n, state_mlstm_m,
           state_mlstm_conv, meta_tokens, w_in, rw_mu, rw_w0, rw_w_up, rw_a0, rw_a_up, rw_g_up, rw_k_k, rw_k_a,
           rw_r_k, rw_gn_g, rw_gn_b, ml_conv_w, ml_conv_b, ml_i_bias, ml_f_bias, ml_gn_g, p_a, p_b, w_out, pre1,
           post1, pre2, post2, w_ff_up, w_ff_down):
    B, T, _ = x_prompt.shape
    dt = x_prompt.dtype
    depth = w_in.shape[0]
    L = PROMPT_CHUNK
    n_pad = (-(T + N_META)) % L
    xp = jnp.concatenate([jnp.zeros((B, n_pad, D_MODEL), dt),
                          jnp.broadcast_to(meta_tokens[None].astype(dt), (B, N_META, D_MODEL)), x_prompt], axis=1)
    xs = x_sample
    Bs, Ts, _ = xs.shape
    zero_state = (jnp.zeros((B, RW_HEADS, RW_HD, RW_HD), F32), jnp.zeros((B, RW_SHIFT_W), F32),
                  jnp.zeros((B, ML_HEADS, ML_HD, ML_HD), F32), jnp.zeros((B, ML_HEADS, ML_HD), F32),
                  jnp.zeros((B, ML_HEADS), F32), jnp.zeros((B, CONV_W - 1, 2 * D_ML), F32))
    p_states, s_states = [], []
    for l in range(depth):
        lp = _layer_params(l, w_in, rw_mu, rw_w0, rw_w_up, rw_a0, rw_a_up, rw_g_up, rw_k_k, rw_k_a, rw_r_k,
                           rw_gn_g, rw_gn_b, ml_conv_w, ml_conv_b, ml_i_bias, ml_f_bias, ml_gn_g, p_a, p_b, w_out,
                           pre1, post1, pre2, post2, w_ff_up, w_ff_down)
        xp, st_p = _layer(xp, zero_state, lp, L, n_pad)
        st_in = (state_rwkv_S[l], state_rwkv_shift[l], state_mlstm_C[l], state_mlstm_n[l], state_mlstm_m[l],
                 state_mlstm_conv[l])
        xs, st_s = _layer(xs, st_in, lp, Ts, 0)
        p_states.append(st_p)
        s_states.append(st_s)
    stk = lambda lst, i: jnp.stack([s[i] for s in lst]).astype(dt)
    y_prompt = xp[:, n_pad + N_META:]
    return (y_prompt, xs,
            stk(p_states, 0), stk(p_states, 1), stk(p_states, 2), stk(p_states, 3), stk(p_states, 4), stk(p_states, 5),
            stk(s_states, 0), stk(s_states, 1), stk(s_states, 2), stk(s_states, 3), stk(s_states, 4), stk(s_states, 5))
```

```python
import functools
import math

import jax
import jax.numpy as jnp
from jax import lax
from jax.experimental import pallas as pl
from jax.experimental.pallas import tpu as pltpu

F32 = jnp.float32
BF16 = jnp.bfloat16

D_MODEL = 1024
N_META = 16
RW_HEADS = 8
RW_HD = 64
D_RW = RW_HEADS * RW_HD
RW_HALF_HEADS = RW_HEADS // 2
RW_HALF = RW_HALF_HEADS * RW_HD
RW_LORA_W = 128
RW_G_LORA = 128
RW_SHIFT_W = 3 * D_RW + RW_LORA_W + RW_G_LORA
RW_GN_EPS = 64e-5
ML_HEADS = 4
ML_HD = 128
D_ML = ML_HEADS * ML_HD
CONV_W = 4
ML_MAIN_W = 4 * D_ML
ML_GN_EPS = 1e-5
GATE_W = 128
D_FF = 4 * D_MODEL
RMS_EPS = 1e-6
SUBLANES = 8

PROMPT_CHUNK = 64
VMEM_LIMIT = 56 * 2**20
RW_SEQS_LONG, RW_SEQS_SHORT = 8, 16
ML_SEQS_LONG, ML_SEQS_SHORT = 8, 16


def _bmm_nn(a, b):
    return jnp.einsum('gmk,gkn->gmn', a.astype(BF16), b.astype(BF16), preferred_element_type=F32)


def _bmm_nt(a, b):
    return jnp.einsum('gmk,gnk->gmn', a.astype(BF16), b.astype(BF16), preferred_element_type=F32)


def _bmm_tn(a, b):
    return jnp.einsum('gkm,gkn->gmn', a.astype(BF16), b.astype(BF16), preferred_element_type=F32)


def _unit_lower_inverse(n):
    L = n.shape[-1]
    t = n + (_iota((1, L, L), 1) == _iota((1, L, L), 2)).astype(F32)
    p = n
    for _ in range(max(L.bit_length() - 2, 0)):
        p = _bmm_nn(p, p)
        t = t + _bmm_nn(t, p)
    return t


def _split(x, terms):
    parts = []
    for _ in range(terms - 1):
        p = x.astype(BF16)
        parts.append(p)
        x = x - p.astype(F32)
    parts.append(x.astype(BF16))
    return parts


def _sel_right(x, sel, terms):
    m, k = x.shape
    pieces = _split(x, terms)
    if terms == 1:
        return jnp.dot(pieces[0], sel, preferred_element_type=F32)
    if k % 128 == 0:
        return jnp.dot(jnp.concatenate(pieces, axis=1), jnp.concatenate([sel] * terms, axis=0),
                       preferred_element_type=F32)
    r = jnp.dot(jnp.concatenate(pieces, axis=0), sel, preferred_element_type=F32)
    return sum(r[i * m:(i + 1) * m] for i in range(terms))


def _cumsum_rows(x):
    row = _iota(x.shape, 1)
    s = 1
    while s < x.shape[1]:
        x = x + jnp.where(row >= s, pltpu.roll(x, s, 1), 0.0)
        s *= 2
    return x


def _sigmoid(x):
    return 1.0 / (1.0 + jnp.exp(-x))


def _softplus(x):
    return jnp.maximum(x, 0.0) + jnp.log(1.0 + jnp.exp(-jnp.abs(x)))


def _iota(shape, dim):
    return lax.broadcasted_iota(jnp.int32, shape, dim)


def _rms(x, g):
    return x * lax.rsqrt(jnp.mean(x * x, axis=-1, keepdims=True) + RMS_EPS) * g


def _const_spec(shape):
    nd = len(shape)
    return pl.BlockSpec(shape, lambda *_: (0,) * nd, pipeline_mode=pl.Buffered(1))


def _group(n, target):
    return max(d for d in range(1, target + 1) if n % d == 0)


def _in_proj_body(x_ref, g_ref, wrw_ref, wml_ref, wif_ref, zrw_ref, zml_ref, zif_ref):
    u = _rms(x_ref[0], g_ref[...]).astype(BF16)
    zrw_ref[0] = jnp.dot(u, wrw_ref[...], preferred_element_type=F32)
    zml_ref[0] = jnp.dot(u, wml_ref[...], preferred_element_type=F32)
    zif_ref[0] = jnp.dot(u, wif_ref[...], preferred_element_type=F32)


def _in_proj(x, g, w_rw, w_ml, w_if, tm):
    B, T, _ = x.shape
    row = lambda w: pl.BlockSpec((1, tm, w), lambda b, i: (b, i, 0))
    return pl.pallas_call(
        _in_proj_body,
        out_shape=(jax.ShapeDtypeStruct((B, T, RW_SHIFT_W), F32),
                   jax.ShapeDtypeStruct((B, T, ML_MAIN_W), F32),
                   jax.ShapeDtypeStruct((B, T, GATE_W), F32)),
        grid=(B, T // tm),
        in_specs=[row(D_MODEL), _const_spec((1, D_MODEL)), _const_spec(w_rw.shape), _const_spec(w_ml.shape),
                  _const_spec(w_if.shape)],
        out_specs=(row(RW_SHIFT_W), row(ML_MAIN_W), row(GATE_W)),
        compiler_params=pltpu.CompilerParams(dimension_semantics=("arbitrary", "arbitrary"),
                                             vmem_limit_bytes=VMEM_LIMIT),
        name="in_proj",
    )(x, g, w_rw, w_ml, w_if)


def _rwkv_chunk(L, at, bt, kt, rt, b_end, k_end, v, decay_end, S, bd32):
    H = RW_HALF_HEADS
    g = at.shape[0]
    lane_head = _iota((1, 1, RW_HALF), 2) >> 6
    hmask = [(lane_head == h).astype(F32) for h in range(H)]
    stack = lambda x: jnp.concatenate([x * hmask[h] for h in range(H)], axis=1).astype(BF16)

    n = _bmm_nt(stack(at), bt).reshape(g * H, L, L)
    n = jnp.where(_iota((1, L, L), 1) > _iota((1, L, L), 2), n, 0.0)
    tinv = _unit_lower_inverse(n).reshape(g, H, L, L)

    v_stack = stack(v)
    ar = jnp.concatenate([at, rt], axis=1).astype(BF16)
    att = _bmm_nt(ar, jnp.concatenate([stack(bt), stack(kt)], axis=1))
    s_col = _iota((1, L, H * L), 2) & (L - 1)
    strict = s_col < _iota((1, L, H * L), 1)
    a_ab = jnp.where(strict, att[:, :L, :H * L], 0.0)
    a_ak = jnp.where(strict, att[:, :L, H * L:], 0.0)
    incl = (_iota((1, L, 2 * H * L), 2) & (L - 1)) <= _iota((1, L, 2 * H * L), 1)
    a_r = jnp.where(incl, att[:, L:, :], 0.0)

    from_state = _bmm_nt(ar, S)
    wmat = from_state[:, :L] + _bmm_nn(a_ak, v_stack)

    def solve(rhs):
        rhs = rhs.astype(BF16)
        x = hmask[0] * _bmm_nn(tinv[:, 0], rhs)
        for h in range(1, H):
            x = x + hmask[h] * _bmm_nn(tinv[:, h], rhs)
        return x

    u = solve(wmat)
    ab_hi, ab_lo = _split(a_ab, 2)
    u_hi, u_lo = _split(u, 2)
    u_hi_stack = stack(u_hi)
    nu = (jnp.einsum('gmk,gkn->gmn', jnp.concatenate([ab_hi, ab_lo], axis=2),
                     jnp.concatenate([u_hi_stack, u_hi_stack], axis=1), preferred_element_type=F32)
          + jnp.einsum('gmk,gkn->gmn', ab_hi, stack(u_lo), preferred_element_type=F32))
    u = u + solve(wmat - u + nu)
    y = from_state[:, L:] + _bmm_nn(a_r, jnp.concatenate([stack(u), v_stack], axis=1))
    upd = _bmm_tn(jnp.concatenate([u, v], axis=1), jnp.concatenate([b_end, k_end], axis=1))
    return y, S * decay_end + upd * bd32


def _rwkv_body(L, nc, nb, has_state, z_ref, *refs):
    if has_state:
        sh0_ref, s0_ref, *refs = refs
    (mu_ref, w0_ref, a0_ref, wc_ref, gup_ref, kk_ref, ka_ref, rk_ref, gng_ref, gnb_ref, bd16_ref, bd32_ref,
     y_ref, sout_ref, shout_ref, s_scr, prev_scr) = refs
    c = pl.program_id(1)
    D, DH = D_RW, RW_HALF
    bd16 = bd16_ref[...]
    bd32 = bd32_ref[...]

    def head_sum(x, terms):
        m = x.shape[1]
        x2 = x.reshape(nb * m, D)
        r = jnp.concatenate([_sel_right(x2[:, hf * DH:(hf + 1) * DH], bd16, terms) for hf in range(2)], axis=1)
        return r.reshape(nb, m, D)

    halves = lambda x: jnp.stack([x[:, :, :DH], x[:, :, DH:]], axis=1).reshape(2 * nb, x.shape[1], DH)

    @pl.when(c == 0)
    def _():
        if has_state:
            prev_scr[...] = sh0_ref[...]
            spread = (_iota((RW_HD, DH), 0) == (_iota((RW_HD, DH), 1) & (RW_HD - 1))).astype(BF16)
            s0 = _sel_right(s0_ref[...].reshape(nb * D, RW_HD), spread, 3)
            s_scr[...] = s0.reshape(2 * nb, DH, DH) * bd32
        else:
            prev_scr[...] = jnp.zeros(prev_scr.shape, F32)
            s_scr[...] = jnp.zeros(s_scr.shape, F32)

    z = z_ref[...]
    prev = jnp.where(_iota(z.shape, 1) == 0, prev_scr[...], pltpu.roll(z, 1, 1))
    prev_scr[...] = z[:, L - 1:L, :]
    zs = z + (prev - z) * mu_ref[...]
    r = zs[:, :, 0:D]
    kraw = zs[:, :, D:2 * D]
    v = zs[:, :, 2 * D:3 * D]
    xl = zs[:, :, 3 * D:3 * D + RW_LORA_W]
    gl = zs[:, :, 3 * D + RW_LORA_W:]
    xl = jnp.where(_iota(xl.shape, 2) < RW_LORA_W // 2, jnp.tanh(xl), xl)
    xl_hi, xl_lo = _split(xl, 2)
    lora = jnp.dot(jnp.concatenate([xl_hi, xl_lo, xl_hi], axis=2).reshape(nb * L, 3 * RW_LORA_W), wc_ref[...],
                   preferred_element_type=F32).reshape(nb, L, 2 * D)
    w = -_softplus(-(w0_ref[...] + lora[:, :, :D])) - 0.5
    logw = -jnp.exp(w)
    a = _sigmoid(a0_ref[...] + lora[:, :, D:])
    g = jnp.dot(_sigmoid(gl).reshape(nb * L, RW_G_LORA).astype(BF16), gup_ref[...],
                preferred_element_type=F32).reshape(nb, L, D)
    kk = kraw * kk_ref[...]
    k = kraw * (1.0 + (a - 1.0) * ka_ref[...])
    sums = head_sum(jnp.concatenate([kk * kk, r * k * rk_ref[...]], axis=1), 1)
    kk = kk * lax.rsqrt(jnp.maximum(sums[:, :L], 1e-24))
    bonus = sums[:, L:]

    cl = _cumsum_rows(logw)
    cl_last = cl[:, L - 1:L, :]
    e_neg = jnp.exp(-cl)
    e_end = jnp.exp(cl_last - cl)
    kka = kk * a
    y_g, s_new = _rwkv_chunk(L, halves(-kk * jnp.exp(cl - logw)), halves(kka * e_neg), halves(k * e_neg),
                             halves(r * jnp.exp(cl)), halves(kka * e_end), halves(k * e_end), halves(v),
                             halves(jnp.exp(cl_last)), s_scr[...], bd32)
    s_scr[...] = s_new
    y_g = y_g.reshape(nb, 2, L, DH)
    y = jnp.concatenate([y_g[:, 0], y_g[:, 1]], axis=2)

    stats = head_sum(jnp.concatenate([y, y * y], axis=1), 2) * (1.0 / RW_HD)
    mean = stats[:, :L]
    var = stats[:, L:] - mean * mean
    yn = (y - mean) * lax.rsqrt(var + RW_GN_EPS) * gng_ref[...] + gnb_ref[...]
    y_ref[...] = (yn + bonus * v) * g

    @pl.when(c == nc - 1)
    def _():
        gather = ((_iota((DH, RW_HD), 0) & (RW_HD - 1)) == _iota((DH, RW_HD), 1)).astype(BF16)
        sout_ref[...] = _sel_right(s_new.reshape(2 * nb * DH, DH), gather, 3).reshape(nb, D, RW_HD)
        shout_ref[...] = z[:, L - 1:L, :]


def _rwkv(z, state, params, L, nb):
    B, T, _ = z.shape
    nc = T // L
    per_b = lambda shp: pl.BlockSpec((nb,) + shp, lambda b, c: (b, 0, 0))
    bd = (_iota((RW_HALF, RW_HALF), 0) >> 6) == (_iota((RW_HALF, RW_HALF), 1) >> 6)
    consts = tuple(params) + (bd.astype(BF16), bd.astype(F32))
    state_args, state_specs = (), []
    if state is not None:
        layer, shift0, s0 = state
        of_layer = lambda shp: pl.BlockSpec((None, nb) + shp, lambda b, c: (layer, b, 0, 0))
        state_args, state_specs = (shift0, s0), [of_layer((1, RW_SHIFT_W)), of_layer((D_RW, RW_HD))]
    return pl.pallas_call(
        functools.partial(_rwkv_body, L, nc, nb, state is not None),
        out_shape=(jax.ShapeDtypeStruct((B, T, D_RW), F32),
                   jax.ShapeDtypeStruct((B, D_RW, RW_HD), F32),
                   jax.ShapeDtypeStruct((B, 1, RW_SHIFT_W), F32)),
        grid=(B // nb, nc),
        in_specs=[pl.BlockSpec((nb, L, RW_SHIFT_W), lambda b, c: (b, c, 0))] + state_specs
                 + [_const_spec(p.shape) for p in consts],
        out_specs=(pl.BlockSpec((nb, L, D_RW), lambda b, c: (b, c, 0)),
                   per_b((D_RW, RW_HD)), per_b((1, RW_SHIFT_W))),
        scratch_shapes=[pltpu.VMEM((2 * nb, RW_HALF, RW_HALF), F32), pltpu.VMEM((nb, 1, RW_SHIFT_W), F32)],
        compiler_params=pltpu.CompilerParams(dimension_semantics=("arbitrary", "arbitrary"),
                                             vmem_limit_bytes=VMEM_LIMIT),
        name="rwkv",
    )(z, *state_args, *consts)


def _mlstm_body(L, nc, nb, n_pad, has_state, zm_ref, zif_ref, *refs):
    if has_state:
        cv0_ref, c0_ref, nm0_ref, *refs = refs
    cw_ref, cb_ref, ifb_ref, gn_ref, y_ref, cout_ref, nmout_ref, cvout_ref, c_scr, nm_scr, cv_scr = refs
    c = pl.program_id(1)
    H, HD, D = ML_HEADS, ML_HD, D_ML
    G = nb * H

    @pl.when(c == 0)
    def _():
        if has_state:
            cv_scr[...] = cv0_ref[...]
            c_scr[...] = c0_ref[...]
            nm_scr[...] = nm0_ref[...]
        else:
            cv_scr[...] = jnp.zeros(cv_scr.shape, F32)
            c_scr[...] = jnp.zeros(c_scr.shape, F32)
            nm_scr[...] = jnp.zeros(nm_scr.shape, F32)

    heads = lambda x: jnp.stack([x[:, :, h * HD:(h + 1) * HD] for h in range(H)], axis=1).reshape(G, x.shape[1], HD)
    lane_pick = lambda x, lo: jnp.stack([x[:, :, lo + h:lo + h + 1] for h in range(H)], axis=1).reshape(G, x.shape[1], 1)
    row_pick = lambda x, lo: jnp.stack([x[:, lo + h:lo + h + 1, :] for h in range(H)], axis=1).reshape(G, 1, x.shape[2])

    zm = zm_ref[...]
    raw = zm[:, :, :2 * D]
    v = zm[:, :, 2 * D:3 * D]
    o = zm[:, :, 3 * D:]
    ext = jnp.concatenate([cv_scr[...], raw], axis=1)
    cv_scr[...] = ext[:, L:, :]
    cw = cw_ref[...]
    qk = cb_ref[...] + raw * cw[CONV_W - 1:CONV_W, :]
    for s in range(1, CONV_W):
        qk = qk + pltpu.roll(ext, s, 1)[:, SUBLANES:, :] * cw[CONV_W - 1 - s:CONV_W - s, :]
    qk = qk * _sigmoid(qk)
    qh = heads(qk[:, :, :D])
    kh = heads(qk[:, :, D:] * (HD ** -0.5))
    vh = heads(v)

    gi = zif_ref[...] + ifb_ref[...]
    lane = _iota(gi.shape, 2)
    lf = jnp.minimum(gi, 0.0) - jnp.log(1.0 + jnp.exp(-jnp.abs(gi)))
    gcol = jnp.where(lane < H, gi, jnp.where(lane < 2 * H, lf, 0.0))
    causal = _iota((1, L, L), 1) >= _iota((1, L, L), 2)
    if n_pad:
        pad_col = (c * L + _iota((1, L, 1), 1)) < n_pad
        gcol = jnp.where(pad_col, 0.0, gcol)
        causal = causal & ((c * L + _iota((1, 1, L), 2)) >= n_pad)
    b_col = _cumsum_rows(gcol)
    grow = jnp.swapaxes(gcol, 1, 2)
    b_row = jnp.swapaxes(b_col, 1, 2)
    bc, ic = lane_pick(b_col, H), lane_pick(gcol, 0)
    br, ir = row_pick(b_row, H), row_pick(grow, 0)

    nm = nm_scr[...]
    ch = c_scr[...].reshape(G, HD, HD)
    nh = row_pick(nm, 0)
    m_prev = lane_pick(nm[:, H:H + 1, :], 0)

    dlog = jnp.where(causal, bc - br + ir, -jnp.inf)
    inter = bc + m_prev
    m_t = jnp.maximum(inter, jnp.max(dlog, axis=-1, keepdims=True))
    amat = jnp.exp(dlog - m_t)
    sc = jnp.exp(inter - m_t)
    aqk = amat * _bmm_nt(qh, kh)
    num = _bmm_nn(aqk, vh) + sc * _bmm_nt(qh, ch)
    den = jnp.sum(aqk, axis=-1, keepdims=True) + sc * jnp.sum(qh * nh, axis=-1, keepdims=True)
    hh = num / jnp.maximum(jnp.abs(den), jnp.exp(-m_t))
    m_new = m_t[:, L - 1:L, :]
    b_last = bc[:, L - 1:L, :]
    wc = jnp.exp(b_last - bc + ic - m_new)
    if n_pad:
        wc = jnp.where(pad_col, 0.0, wc)
    dec = jnp.exp(b_last + m_prev - m_new)
    c_scr[...] = (dec * ch + _bmm_tn(vh * wc, kh)).reshape(nb, D, HD)
    n_new = (dec * nh + jnp.sum(kh * wc, axis=1, keepdims=True)).reshape(nb, H, 1, HD)
    m_new = m_new.reshape(nb, H, 1, 1)
    m_row = jnp.zeros((nb, 1, HD), F32)
    for h in range(H):
        m_row = jnp.where(_iota(m_row.shape, 2) == h, m_new[:, h], m_row)
    nm_scr[...] = jnp.concatenate([n_new[:, h] for h in range(H)] + [m_row, nm[:, H + 1:, :]], axis=1)

    mu = jnp.mean(hh, axis=-1, keepdims=True)
    xc = hh - mu
    var = jnp.mean(xc * xc, axis=-1, keepdims=True)
    gn = jnp.concatenate([gn_ref[:, h * HD:(h + 1) * HD][None] for h in range(H)] * nb, axis=0)
    out = (xc * lax.rsqrt(var + ML_GN_EPS) * gn * _sigmoid(heads(o))).reshape(nb, H, L, HD)
    y_ref[...] = jnp.concatenate([out[:, h] for h in range(H)], axis=2)

    @pl.when(c == nc - 1)
    def _():
        cout_ref[...] = c_scr[...]
        nmout_ref[...] = nm_scr[...]
        cvout_ref[...] = cv_scr[...]


def _mlstm(zm, zif, state, params, L, n_pad, nb):
    B, T, _ = zm.shape
    nc = T // L
    per_b = lambda shp: pl.BlockSpec((nb,) + shp, lambda b, c: (b, 0, 0))
    chunk = lambda w: pl.BlockSpec((nb, L, w), lambda b, c: (b, c, 0))
    state_shapes = ((SUBLANES, 2 * D_ML), (D_ML, ML_HD), (SUBLANES, ML_HD))
    state_args, state_specs = (), []
    if state is not None:
        layer, cv0, c0, nm0 = state
        state_args = (cv0, c0, nm0)
        state_specs = [per_b(state_shapes[0]),
                       pl.BlockSpec((None, nb) + state_shapes[1], lambda b, c: (layer, b, 0, 0)),
                       per_b(state_shapes[2])]
    return pl.pallas_call(
        functools.partial(_mlstm_body, L, nc, nb, n_pad, state is not None),
        out_shape=(jax.ShapeDtypeStruct((B, T, D_ML), F32),
                   jax.ShapeDtypeStruct((B,) + state_shapes[1], F32),
                   jax.ShapeDtypeStruct((B,) + state_shapes[2], F32),
                   jax.ShapeDtypeStruct((B,) + state_shapes[0], F32)),
        grid=(B // nb, nc),
        in_specs=[chunk(ML_MAIN_W), chunk(GATE_W)] + state_specs + [_const_spec(p.shape) for p in params],
        out_specs=(chunk(D_ML), per_b(state_shapes[1]), per_b(state_shapes[2]), per_b(state_shapes[0])),
        scratch_shapes=[pltpu.VMEM((nb,) + state_shapes[1], F32), pltpu.VMEM((nb,) + state_shapes[2], F32),
                        pltpu.VMEM((nb,) + state_shapes[0], F32)],
        compiler_params=pltpu.CompilerParams(dimension_semantics=("arbitrary", "arbitrary"),
                                             vmem_limit_bytes=VMEM_LIMIT),
        name="mlstm",
    )(zm, zif, *state_args, *params)


def _merge_body(x_ref, ya_ref, yb_ref, pre_ref, post_ref, wg_ref, pa_ref, pb_ref, wo_ref, h_ref):
    x = x_ref[0]
    u = _rms(x, pre_ref[...]).astype(BF16)
    gate = _sigmoid(jnp.dot(u, wg_ref[...], preferred_element_type=F32))
    pa = jnp.dot(ya_ref[0].astype(BF16), pa_ref[...], preferred_element_type=F32)
    pb = jnp.dot(yb_ref[0].astype(BF16), pb_ref[...], preferred_element_type=F32)
    merged = gate[:, :D_MODEL] * pa + gate[:, D_MODEL:] * pb
    o = jnp.dot(merged.astype(BF16), wo_ref[...], preferred_element_type=F32)
    h_ref[0] = x + _rms(o, post_ref[...])


def _merge(x, ya, yb, pre, post, wg, pa, pb, wo, tm):
    B, T, _ = x.shape
    row = lambda w: pl.BlockSpec((1, tm, w), lambda b, i: (b, i, 0))
    consts = (pre, post, wg, pa, pb, wo)
    return pl.pallas_call(
        _merge_body,
        out_shape=jax.ShapeDtypeStruct((B, T, D_MODEL), F32),
        grid=(B, T // tm),
        in_specs=[row(D_MODEL), row(D_RW), row(D_ML)] + [_const_spec(p.shape) for p in consts],
        out_specs=row(D_MODEL),
        compiler_params=pltpu.CompilerParams(dimension_semantics=("arbitrary", "arbitrary"),
                                             vmem_limit_bytes=VMEM_LIMIT),
        name="merge",
    )(x, ya, yb, *consts)


FF_SPLIT = 4


def _ffn_body(tm, n_pad, h_ref, pre_ref, post_ref, wu_ref, wd_ref, o_ref):
    h = h_ref[0]
    u = _rms(h, pre_ref[...]).astype(BF16)
    step = D_FF // FF_SPLIT
    f = None
    for j in range(FF_SPLIT):
        t = jnp.maximum(jnp.dot(u, wu_ref[:, j * step:(j + 1) * step], preferred_element_type=F32), 0.0)
        part = jnp.dot((t * t).astype(BF16), wd_ref[j * step:(j + 1) * step, :], preferred_element_type=F32)
        f = part if f is None else f + part
    out = h + _rms(f, post_ref[...])
    if n_pad:
        out = jnp.where(pl.program_id(1) * tm + _iota((tm, 1), 0) < n_pad, 0.0, out)
    o_ref[0] = out


def _ffn(h, pre, post, wu, wd, n_pad, drop):
    B, T, _ = h.shape
    tm = _row_tile(T - drop)
    row = pl.BlockSpec((1, tm, D_MODEL), lambda b, i: (b, i, 0))
    in_row = row
    if drop:
        in_row = pl.BlockSpec((pl.Element(1), pl.Element(tm), pl.Element(D_MODEL)),
                              lambda b, i: (b, pl.multiple_of(drop + i * tm, math.gcd(drop, tm)), 0))
    consts = (pre, post, wu, wd)
    return pl.pallas_call(
        functools.partial(_ffn_body, tm, 0 if drop else n_pad),
        out_shape=jax.ShapeDtypeStruct((B, T - drop, D_MODEL), F32),
        grid=(B, (T - drop) // tm),
        in_specs=[in_row] + [_const_spec(p.shape) for p in consts],
        out_specs=row,
        compiler_params=pltpu.CompilerParams(dimension_semantics=("arbitrary", "arbitrary"),
                                             vmem_limit_bytes=VMEM_LIMIT),
        name="ffn",
    )(h, *consts)


MAX_ROW_TILE = 768


def _row_tile(T):
    return max(tm for tm in range(SUBLANES, MAX_ROW_TILE + 1, SUBLANES) if T % tm == 0)


def _layer(x, st, lp, L, n_pad, drop=0):
    B, T, _ = x.shape
    long_seq = T > L
    flat = (lambda a: a.reshape(1, B * T, a.shape[-1])) if not n_pad else (lambda a: a)
    unflat = lambda a: a.reshape(B, T, a.shape[-1])
    xf = flat(x)
    tm = _row_tile(xf.shape[1])
    z_rw, z_ml, z_if = map(unflat, _in_proj(xf, lp['pre1'], lp['w_rw'], lp['w_ml'], lp['w_if'], tm))

    rw_state = ml_state = None
    if st is not None:
        l, S, sh, C, n, m, cb = st
        depth = S.shape[0]
        rw_state = (l, sh.reshape(depth, B, 1, RW_SHIFT_W), S.reshape(depth, B, D_RW, RW_HD))
        cv0 = jnp.pad(cb[l], ((0, 0), (SUBLANES - (CONV_W - 1), 0), (0, 0)))
        nm0 = jnp.concatenate([n[l], jnp.pad(m[l], ((0, 0), (0, ML_HD - ML_HEADS)))[:, None, :],
                               jnp.zeros((B, SUBLANES - ML_HEADS - 1, ML_HD), F32)], axis=1)
        ml_state = (l, cv0, C.reshape(depth, B, D_ML, ML_HD), nm0)
    ya, S1, sh1 = _rwkv(z_rw, rw_state, lp['rw_params'], L, _group(B, RW_SEQS_LONG if long_seq else RW_SEQS_SHORT))
    yb, C1, nm1, cv1 = _mlstm(z_ml, z_if, ml_state, lp['ml_params'], L, n_pad,
                              _group(B, ML_SEQS_LONG if long_seq else ML_SEQS_SHORT))

    h = _merge(xf, flat(ya), flat(yb), lp['pre1'], lp['post1'], lp['w_gate'], lp['p_a'], lp['p_b'], lp['w_out'], tm)
    out = _ffn(h, lp['pre2'], lp['post2'], lp['w_ff_up'], lp['w_ff_down'], n_pad, drop)
    out = out if drop else unflat(out)
    new_state = (S1.reshape(B, RW_HEADS, RW_HD, RW_HD), sh1[:, 0, :], C1.reshape(B, ML_HEADS, ML_HD, ML_HD),
                 nm1[:, :ML_HEADS, :], nm1[:, ML_HEADS, :ML_HEADS], cv1[:, SUBLANES - (CONV_W - 1):, :])
    return out, new_state


def _layer_params(l, w_in, rw_mu, rw_w0, rw_w_up, rw_a0, rw_a_up, rw_g_up, rw_k_k, rw_k_a, rw_r_k, rw_gn_g,
                  rw_gn_b, ml_conv_w, ml_conv_b, ml_i_bias, ml_f_bias, ml_gn_g, p_a, p_b, w_out, pre1, post1,
                  pre2, post2, w_ff_up, w_ff_down):
    row = lambda a: a[l].reshape(1, -1).astype(F32)
    w = w_in[l]
    c_ml = RW_SHIFT_W
    c_if = c_ml + ML_MAIN_W
    c_gate = c_if + 2 * ML_HEADS
    half = RW_LORA_W // 2
    lora = jnp.zeros((RW_LORA_W, 2 * D_RW), F32)
    lora = lora.at[:half, :D_RW].set(rw_w_up[l]).at[half:, D_RW:].set(rw_a_up[l])
    lora_hi = lora.astype(BF16)
    lora_lo = (lora - lora_hi.astype(F32)).astype(BF16)
    lora = jnp.concatenate([lora_hi, lora_hi, lora_lo], axis=0)
    if_bias = jnp.zeros((1, GATE_W), F32)
    if_bias = if_bias.at[0, :ML_HEADS].set(ml_i_bias[l]).at[0, ML_HEADS:2 * ML_HEADS].set(ml_f_bias[l])
    return dict(
        pre1=row(pre1), post1=row(post1), pre2=row(pre2), post2=row(post2),
        w_rw=w[:, :c_ml].astype(BF16),
        w_ml=w[:, c_ml:c_if].astype(BF16),
        w_if=jnp.pad(w[:, c_if:c_gate], ((0, 0), (0, GATE_W - 2 * ML_HEADS))).astype(BF16),
        w_gate=w[:, c_gate:].astype(BF16),
        rw_params=(row(rw_mu), row(rw_w0), row(rw_a0), lora, rw_g_up[l].astype(BF16), row(rw_k_k), row(rw_k_a),
                   row(rw_r_k), row(rw_gn_g), row(rw_gn_b)),
        ml_params=(ml_conv_w[l].astype(F32), row(ml_conv_b), if_bias, row(ml_gn_g)),
        p_a=p_a[l].astype(BF16), p_b=p_b[l].astype(BF16), w_out=w_out[l].astype(BF16),
        w_ff_up=w_ff_up[l].astype(BF16), w_ff_down=w_ff_down[l].astype(BF16),
    )


def kernel(x_prompt, x_sample, state_rwkv_S, state_rwkv_shift, state_mlstm_C, state_mlstm_n, state_mlstm_m,
           state_mlstm_conv, meta_tokens, w_in, rw_mu, rw_w0, rw_w_up, rw_a0, rw_a_up, rw_g_up, rw_k_k, rw_k_a,
           rw_r_k, rw_gn_g, rw_gn_b, ml_conv_w, ml_conv_b, ml_i_bias, ml_f_bias, ml_gn_g, p_a, p_b, w_out, pre1,
           post1, pre2, post2, w_ff_up, w_ff_down):
    B, T, _ = x_prompt.shape
    dt = x_prompt.dtype
    depth = w_in.shape[0]
    L = PROMPT_CHUNK
    n_pad = (-(T + N_META)) % L
    xp = jnp.concatenate([jnp.zeros((B, n_pad, D_MODEL), dt),
                          jnp.broadcast_to(meta_tokens[None].astype(dt), (B, N_META, D_MODEL)), x_prompt], axis=1)
    xs = x_sample
    Bs, Ts, _ = xs.shape
    p_states, s_states = [], []
    for l in range(depth):
        lp = _layer_params(l, w_in, rw_mu, rw_w0, rw_w_up, rw_a0, rw_a_up, rw_g_up, rw_k_k, rw_k_a, rw_r_k,
                           rw_gn_g, rw_gn_b, ml_conv_w, ml_conv_b, ml_i_bias, ml_f_bias, ml_gn_g, p_a, p_b, w_out,
                           pre1, post1, pre2, post2, w_ff_up, w_ff_down)
        xp, st_p = _layer(xp, None, lp, L, n_pad, drop=n_pad + N_META if l == depth - 1 else 0)
        st_in = (l, state_rwkv_S, state_rwkv_shift, state_mlstm_C, state_mlstm_n, state_mlstm_m, state_mlstm_conv)
        xs, st_s = _layer(xs, st_in, lp, Ts, 0)
        p_states.append(st_p)
        s_states.append(st_s)
    stk = lambda lst, i: jnp.stack([s[i] for s in lst]).astype(dt)
    return (xp, xs,
            stk(p_states, 0), stk(p_states, 1), stk(p_states, 2), stk(p_states, 3), stk(p_states, 4), stk(p_states, 5),
            stk(s_states, 0), stk(s_states, 1), stk(s_states, 2), stk(s_states, 3), stk(s_states, 4), stk(s_states, 5))
```

```python
import functools
import math

import jax
import jax.numpy as jnp
from jax import lax
from jax.experimental import pallas as pl
from jax.experimental.pallas import tpu as pltpu

F32 = jnp.float32
BF16 = jnp.bfloat16

D_MODEL = 1024
N_META = 16
RW_HEADS = 8
RW_HD = 64
D_RW = RW_HEADS * RW_HD
RW_HALF_HEADS = RW_HEADS // 2
RW_HALF = RW_HALF_HEADS * RW_HD
RW_LORA_W = 128
RW_G_LORA = 128
RW_SHIFT_W = 3 * D_RW + RW_LORA_W + RW_G_LORA
RW_GN_EPS = 64e-5
ML_HEADS = 4
ML_HD = 128
D_ML = ML_HEADS * ML_HD
CONV_W = 4
ML_MAIN_W = 4 * D_ML
ML_GN_EPS = 1e-5
GATE_W = 128
D_FF = 4 * D_MODEL
RMS_EPS = 1e-6
SUBLANES = 8

PROMPT_CHUNK = 64
VMEM_LIMIT = 56 * 2**20
RW_SEQS_LONG, RW_SEQS_SHORT = 8, 16
ML_SEQS_LONG, ML_SEQS_SHORT = 8, 16


def _bmm_nn(a, b):
    return jnp.einsum('gmk,gkn->gmn', a.astype(BF16), b.astype(BF16), preferred_element_type=F32)


def _bmm_nt(a, b):
    return jnp.einsum('gmk,gnk->gmn', a.astype(BF16), b.astype(BF16), preferred_element_type=F32)


def _bmm_tn(a, b):
    return jnp.einsum('gkm,gkn->gmn', a.astype(BF16), b.astype(BF16), preferred_element_type=F32)


def _unit_lower_inverse(n):
    L = n.shape[-1]
    t = n + (_iota((1, L, L), 1) == _iota((1, L, L), 2)).astype(F32)
    p = n
    for _ in range(max(L.bit_length() - 2, 0)):
        p = _bmm_nn(p, p)
        t = t + _bmm_nn(t, p)
    return t


def _split(x, terms):
    parts = []
    for _ in range(terms - 1):
        p = x.astype(BF16)
        parts.append(p)
        x = x - p.astype(F32)
    parts.append(x.astype(BF16))
    return parts


def _sel_right(x, sel, terms):
    m, k = x.shape
    pieces = _split(x, terms)
    if terms == 1:
        return jnp.dot(pieces[0], sel, preferred_element_type=F32)
    if k % 128 == 0:
        return jnp.dot(jnp.concatenate(pieces, axis=1), jnp.concatenate([sel] * terms, axis=0),
                       preferred_element_type=F32)
    r = jnp.dot(jnp.concatenate(pieces, axis=0), sel, preferred_element_type=F32)
    return sum(r[i * m:(i + 1) * m] for i in range(terms))


def _cumsum_rows(x):
    row = _iota(x.shape, 1)
    s = 1
    while s < x.shape[1]:
        x = x + jnp.where(row >= s, pltpu.roll(x, s, 1), 0.0)
        s *= 2
    return x


def _sigmoid(x):
    return 1.0 / (1.0 + jnp.exp(-x))


def _iota(shape, dim):
    return lax.broadcasted_iota(jnp.int32, shape, dim)


def _rms(x, g):
    return x * lax.rsqrt(jnp.mean(x * x, axis=-1, keepdims=True) + RMS_EPS) * g


def _const_spec(shape):
    nd = len(shape)
    return pl.BlockSpec(shape, lambda *_: (0,) * nd, pipeline_mode=pl.Buffered(1))


def _group(n, target):
    return max(d for d in range(1, target + 1) if n % d == 0)


def _in_proj_body(x_ref, g_ref, wrw_ref, wml_ref, wif_ref, zrw_ref, zml_ref, zif_ref):
    u = _rms(x_ref[0], g_ref[...]).astype(BF16)
    zrw_ref[0] = jnp.dot(u, wrw_ref[...], preferred_element_type=F32)
    zml_ref[0] = jnp.dot(u, wml_ref[...], preferred_element_type=F32)
    zif_ref[0] = jnp.dot(u, wif_ref[...], preferred_element_type=F32)


def _in_proj(x, g, w_rw, w_ml, w_if, tm):
    B, T, _ = x.shape
    row = lambda w: pl.BlockSpec((1, tm, w), lambda b, i: (b, i, 0))
    return pl.pallas_call(
        _in_proj_body,
        out_shape=(jax.ShapeDtypeStruct((B, T, RW_SHIFT_W), F32),
                   jax.ShapeDtypeStruct((B, T, ML_MAIN_W), F32),
                   jax.ShapeDtypeStruct((B, T, GATE_W), F32)),
        grid=(B, T // tm),
        in_specs=[row(D_MODEL), _const_spec((1, D_MODEL)), _const_spec(w_rw.shape), _const_spec(w_ml.shape),
                  _const_spec(w_if.shape)],
        out_specs=(row(RW_SHIFT_W), row(ML_MAIN_W), row(GATE_W)),
        compiler_params=pltpu.CompilerParams(dimension_semantics=("arbitrary", "arbitrary"),
                                             vmem_limit_bytes=VMEM_LIMIT),
        name="in_proj",
    )(x, g, w_rw, w_ml, w_if)


def _rwkv_chunk(L, at, bt, kt, rt, b_end, k_end, v, decay_end, S, bd32):
    H = RW_HALF_HEADS
    g = at.shape[0]
    lane_head = _iota((1, 1, RW_HALF), 2) >> 6
    hmask = [(lane_head == h).astype(F32) for h in range(H)]
    if L % 16 == 0:
        hmask16 = [m.astype(BF16) for m in hmask]
        stack = lambda x: jnp.concatenate([x.astype(BF16) * hmask16[h] for h in range(H)], axis=1)
    else:
        stack = lambda x: jnp.concatenate([x * hmask[h] for h in range(H)], axis=1).astype(BF16)

    n = _bmm_nt(stack(at), bt).reshape(g * H, L, L)
    n = jnp.where(_iota((1, L, L), 1) > _iota((1, L, L), 2), n, 0.0)
    tinv = _unit_lower_inverse(n).reshape(g, H, L, L)

    v_stack = stack(v)
    ar = jnp.concatenate([at, rt], axis=1).astype(BF16)
    att = _bmm_nt(ar, jnp.concatenate([stack(bt), stack(kt)], axis=1))
    s_col = _iota((1, L, H * L), 2) & (L - 1)
    strict = s_col < _iota((1, L, H * L), 1)
    a_ab = jnp.where(strict, att[:, :L, :H * L], 0.0)
    a_ak = jnp.where(strict, att[:, :L, H * L:], 0.0)
    incl = (_iota((1, L, 2 * H * L), 2) & (L - 1)) <= _iota((1, L, 2 * H * L), 1)
    a_r = jnp.where(incl, att[:, L:, :], 0.0)

    from_state = _bmm_nt(ar, S)
    wmat = from_state[:, :L] + _bmm_nn(a_ak, v_stack)

    def solve(rhs):
        rhs = rhs.astype(BF16)
        x = hmask[0] * _bmm_nn(tinv[:, 0], rhs)
        for h in range(1, H):
            x = x + hmask[h] * _bmm_nn(tinv[:, h], rhs)
        return x

    u = solve(wmat)
    ab_hi, ab_lo = _split(a_ab, 2)
    u_hi, u_lo = _split(u, 2)
    u_hi_stack = stack(u_hi)
    nu = (jnp.einsum('gmk,gkn->gmn', jnp.concatenate([ab_hi, ab_lo], axis=2),
                     jnp.concatenate([u_hi_stack, u_hi_stack], axis=1), preferred_element_type=F32)
          + jnp.einsum('gmk,gkn->gmn', ab_hi, stack(u_lo), preferred_element_type=F32))
    u = u + solve(wmat - u + nu)
    y = from_state[:, L:] + _bmm_nn(a_r, jnp.concatenate([stack(u), v_stack], axis=1))
    upd = _bmm_tn(jnp.concatenate([u, v], axis=1), jnp.concatenate([b_end, k_end], axis=1))
    return y, S * decay_end + upd * bd32


def _rwkv_body(L, nc, nb, has_state, z_ref, *refs):
    if has_state:
        sh0_ref, s0_ref, *refs = refs
    (mu_ref, w0_ref, a0_ref, wc_ref, gup_ref, kk_ref, ka_ref, rk_ref, gng_ref, gnb_ref, bd16_ref, bd32_ref,
     y_ref, sout_ref, shout_ref, s_scr, prev_scr) = refs
    c = pl.program_id(1)
    D, DH = D_RW, RW_HALF
    bd16 = bd16_ref[...]
    bd32 = bd32_ref[...]

    def head_sum(x, terms):
        m = x.shape[1]
        x2 = x.reshape(nb * m, D)
        r = jnp.concatenate([_sel_right(x2[:, hf * DH:(hf + 1) * DH], bd16, terms) for hf in range(2)], axis=1)
        return r.reshape(nb, m, D)

    halves = lambda x: jnp.stack([x[:, :, :DH], x[:, :, DH:]], axis=1).reshape(2 * nb, x.shape[1], DH)

    @pl.when(c == 0)
    def _():
        if has_state:
            prev_scr[...] = sh0_ref[...]
            s0 = s0_ref[...].reshape(2 * nb, DH, RW_HD)
            s_scr[...] = jnp.concatenate([s0] * RW_HALF_HEADS, axis=2) * bd32
        else:
            prev_scr[...] = jnp.zeros(prev_scr.shape, F32)
            s_scr[...] = jnp.zeros(s_scr.shape, F32)

    z = z_ref[...]
    prev = jnp.where(_iota(z.shape, 1) == 0, prev_scr[...], pltpu.roll(z, 1, 1))
    prev_scr[...] = z[:, L - 1:L, :]
    zs = z + (prev - z) * mu_ref[...]
    r = zs[:, :, 0:D]
    kraw = zs[:, :, D:2 * D]
    v = zs[:, :, 2 * D:3 * D]
    xl = zs[:, :, 3 * D:3 * D + RW_LORA_W]
    gl = zs[:, :, 3 * D + RW_LORA_W:]
    xl = jnp.where(_iota(xl.shape, 2) < RW_LORA_W // 2, jnp.tanh(xl), xl)
    xl_hi, xl_lo = _split(xl, 2)
    lora = jnp.dot(jnp.concatenate([xl_hi, xl_lo, xl_hi], axis=2).reshape(nb * L, 3 * RW_LORA_W), wc_ref[...],
                   preferred_element_type=F32).reshape(nb, L, 2 * D)
    logw = -math.exp(-0.5) * _sigmoid(w0_ref[...] + lora[:, :, :D])
    a = _sigmoid(a0_ref[...] + lora[:, :, D:])
    g = jnp.dot(_sigmoid(gl).reshape(nb * L, RW_G_LORA).astype(BF16), gup_ref[...],
                preferred_element_type=F32).reshape(nb, L, D)
    kk = kraw * kk_ref[...]
    k = kraw * (1.0 + (a - 1.0) * ka_ref[...])
    sums = head_sum(jnp.concatenate([kk * kk, r * k * rk_ref[...]], axis=1), 1)
    kk = kk * lax.rsqrt(jnp.maximum(sums[:, :L], 1e-24))
    bonus = sums[:, L:]

    cl = _cumsum_rows(logw)
    cl_last = cl[:, L - 1:L, :]
    e_neg = jnp.exp(-cl)
    e_end = jnp.exp(cl_last - cl)
    kka = kk * a
    y_g, s_new = _rwkv_chunk(L, halves(-kk * jnp.exp(cl - logw)), halves(kka * e_neg), halves(k * e_neg),
                             halves(r * jnp.exp(cl)), halves(kka * e_end), halves(k * e_end), halves(v),
                             halves(jnp.exp(cl_last)), s_scr[...], bd32)
    s_scr[...] = s_new
    y_g = y_g.reshape(nb, 2, L, DH)
    y = jnp.concatenate([y_g[:, 0], y_g[:, 1]], axis=2)

    stats = head_sum(jnp.concatenate([y, y * y], axis=1), 2) * (1.0 / RW_HD)
    mean = stats[:, :L]
    var = stats[:, L:] - mean * mean
    yn = (y - mean) * lax.rsqrt(var + RW_GN_EPS) * gng_ref[...] + gnb_ref[...]
    y_ref[...] = (yn + bonus * v) * g

    @pl.when(c == nc - 1)
    def _():
        compact = sum(s_new[:, :, h * RW_HD:(h + 1) * RW_HD] for h in range(RW_HALF_HEADS))
        sout_ref[...] = compact.reshape(nb, D, RW_HD)
        shout_ref[...] = z[:, L - 1:L, :]


def _rwkv(z, state, params, L, nb):
    B, T, _ = z.shape
    nc = T // L
    per_b = lambda shp: pl.BlockSpec((nb,) + shp, lambda b, c: (b, 0, 0))
    bd = (_iota((RW_HALF, RW_HALF), 0) >> 6) == (_iota((RW_HALF, RW_HALF), 1) >> 6)
    consts = tuple(params) + (bd.astype(BF16), bd.astype(F32))
    state_args, state_specs = (), []
    if state is not None:
        layer, shift0, s0 = state
        of_layer = lambda shp: pl.BlockSpec((None, nb) + shp, lambda b, c: (layer, b, 0, 0))
        state_args, state_specs = (shift0, s0), [of_layer((1, RW_SHIFT_W)), of_layer((D_RW, RW_HD))]
    return pl.pallas_call(
        functools.partial(_rwkv_body, L, nc, nb, state is not None),
        out_shape=(jax.ShapeDtypeStruct((B, T, D_RW), F32),
                   jax.ShapeDtypeStruct((B, D_RW, RW_HD), F32),
                   jax.ShapeDtypeStruct((B, 1, RW_SHIFT_W), F32)),
        grid=(B // nb, nc),
        in_specs=[pl.BlockSpec((nb, L, RW_SHIFT_W), lambda b, c: (b, c, 0))] + state_specs
                 + [_const_spec(p.shape) for p in consts],
        out_specs=(pl.BlockSpec((nb, L, D_RW), lambda b, c: (b, c, 0)),
                   per_b((D_RW, RW_HD)), per_b((1, RW_SHIFT_W))),
        scratch_shapes=[pltpu.VMEM((2 * nb, RW_HALF, RW_HALF), F32), pltpu.VMEM((nb, 1, RW_SHIFT_W), F32)],
        compiler_params=pltpu.CompilerParams(dimension_semantics=("arbitrary", "arbitrary"),
                                             vmem_limit_bytes=VMEM_LIMIT),
        name="rwkv",
    )(z, *state_args, *consts)


def _mlstm_body(L, nc, nb, n_pad, has_state, zm_ref, zif_ref, *refs):
    if has_state:
        cv0_ref, c0_ref, nm0_ref, *refs = refs
    cw_ref, cb_ref, ifb_ref, gn_ref, y_ref, cout_ref, nmout_ref, cvout_ref, c_scr, nm_scr, cv_scr = refs
    c = pl.program_id(1)
    H, HD, D = ML_HEADS, ML_HD, D_ML
    G = nb * H

    @pl.when(c == 0)
    def _():
        if has_state:
            cv_scr[...] = cv0_ref[...]
            c_scr[...] = c0_ref[...]
            nm_scr[...] = nm0_ref[...]
        else:
            cv_scr[...] = jnp.zeros(cv_scr.shape, F32)
            c_scr[...] = jnp.zeros(c_scr.shape, F32)
            nm_scr[...] = jnp.zeros(nm_scr.shape, F32)

    heads = lambda x: jnp.stack([x[:, :, h * HD:(h + 1) * HD] for h in range(H)], axis=1).reshape(G, x.shape[1], HD)
    lane_pick = lambda x, lo: jnp.stack([x[:, :, lo + h:lo + h + 1] for h in range(H)], axis=1).reshape(G, x.shape[1], 1)
    row_pick = lambda x, lo: jnp.stack([x[:, lo + h:lo + h + 1, :] for h in range(H)], axis=1).reshape(G, 1, x.shape[2])

    zm = zm_ref[...]
    raw = zm[:, :, :2 * D]
    v = zm[:, :, 2 * D:3 * D]
    o = zm[:, :, 3 * D:]
    ext = jnp.concatenate([cv_scr[...], raw], axis=1)
    cv_scr[...] = ext[:, L:, :]
    cw = cw_ref[...]
    qk = cb_ref[...] + raw * cw[CONV_W - 1:CONV_W, :]
    for s in range(1, CONV_W):
        qk = qk + pltpu.roll(ext, s, 1)[:, SUBLANES:, :] * cw[CONV_W - 1 - s:CONV_W - s, :]
    qk = qk * _sigmoid(qk)
    qh = heads(qk[:, :, :D])
    kh = heads(qk[:, :, D:] * (HD ** -0.5))
    vh = heads(v)

    gi = zif_ref[...] + ifb_ref[...]
    lane = _iota(gi.shape, 2)
    lf = jnp.minimum(gi, 0.0) - jnp.log(1.0 + jnp.exp(-jnp.abs(gi)))
    gcol = jnp.where(lane < H, gi, jnp.where(lane < 2 * H, lf, 0.0))
    causal = _iota((1, L, L), 1) >= _iota((1, L, L), 2)
    if n_pad:
        pad_col = (c * L + _iota((1, L, 1), 1)) < n_pad
        gcol = jnp.where(pad_col, 0.0, gcol)
        causal = causal & ((c * L + _iota((1, 1, L), 2)) >= n_pad)
    b_col = _cumsum_rows(gcol)
    grow = jnp.swapaxes(gcol, 1, 2)
    b_row = jnp.swapaxes(b_col, 1, 2)
    bc, ic = lane_pick(b_col, H), lane_pick(gcol, 0)
    br, ir = row_pick(b_row, H), row_pick(grow, 0)

    nm = nm_scr[...]
    ch = c_scr[...].reshape(G, HD, HD)
    nh = row_pick(nm, 0)
    m_prev = lane_pick(nm[:, H:H + 1, :], 0)

    dlog = jnp.where(causal, bc - br + ir, -jnp.inf)
    inter = bc + m_prev
    m_t = jnp.maximum(inter, jnp.max(dlog, axis=-1, keepdims=True))
    amat = jnp.exp(dlog - m_t)
    sc = jnp.exp(inter - m_t)
    aqk = amat * _bmm_nt(qh, kh)
    num = _bmm_nn(aqk, vh) + sc * _bmm_nt(qh, ch)
    den = jnp.sum(aqk, axis=-1, keepdims=True) + sc * jnp.sum(qh * nh, axis=-1, keepdims=True)
    hh = num / jnp.maximum(jnp.abs(den), jnp.exp(-m_t))
    m_new = m_t[:, L - 1:L, :]
    b_last = bc[:, L - 1:L, :]
    wc = jnp.exp(b_last - bc + ic - m_new)
    if n_pad:
        wc = jnp.where(pad_col, 0.0, wc)
    dec = jnp.exp(b_last + m_prev - m_new)
    c_scr[...] = (dec * ch + _bmm_tn(vh * wc, kh)).reshape(nb, D, HD)
    n_new = (dec * nh + jnp.sum(kh * wc, axis=1, keepdims=True)).reshape(nb, H, 1, HD)
    m_new = m_new.reshape(nb, H, 1, 1)
    m_row = jnp.zeros((nb, 1, HD), F32)
    for h in range(H):
        m_row = jnp.where(_iota(m_row.shape, 2) == h, m_new[:, h], m_row)
    nm_scr[...] = jnp.concatenate([n_new[:, h] for h in range(H)] + [m_row, nm[:, H + 1:, :]], axis=1)

    mu = jnp.mean(hh, axis=-1, keepdims=True)
    xc = hh - mu
    var = jnp.mean(xc * xc, axis=-1, keepdims=True)
    gn = jnp.concatenate([gn_ref[:, h * HD:(h + 1) * HD][None] for h in range(H)] * nb, axis=0)
    out = (xc * lax.rsqrt(var + ML_GN_EPS) * gn * _sigmoid(heads(o))).reshape(nb, H, L, HD)
    y_ref[...] = jnp.concatenate([out[:, h] for h in range(H)], axis=2)

    @pl.when(c == nc - 1)
    def _():
        cout_ref[...] = c_scr[...]
        nmout_ref[...] = nm_scr[...]
        cvout_ref[...] = cv_scr[...]


def _mlstm(zm, zif, state, params, L, n_pad, nb):
    B, T, _ = zm.shape
    nc = T // L
    per_b = lambda shp: pl.BlockSpec((nb,) + shp, lambda b, c: (b, 0, 0))
    chunk = lambda w: pl.BlockSpec((nb, L, w), lambda b, c: (b, c, 0))
    state_shapes = ((SUBLANES, 2 * D_ML), (D_ML, ML_HD), (SUBLANES, ML_HD))
    state_args, state_specs = (), []
    if state is not None:
        layer, cv0, c0, nm0 = state
        state_args = (cv0, c0, nm0)
        state_specs = [per_b(state_shapes[0]),
                       pl.BlockSpec((None, nb) + state_shapes[1], lambda b, c: (layer, b, 0, 0)),
                       per_b(state_shapes[2])]
    return pl.pallas_call(
        functools.partial(_mlstm_body, L, nc, nb, n_pad, state is not None),
        out_shape=(jax.ShapeDtypeStruct((B, T, D_ML), F32),
                   jax.ShapeDtypeStruct((B,) + state_shapes[1], F32),
                   jax.ShapeDtypeStruct((B,) + state_shapes[2], F32),
                   jax.ShapeDtypeStruct((B,) + state_shapes[0], F32)),
        grid=(B // nb, nc),
        in_specs=[chunk(ML_MAIN_W), chunk(GATE_W)] + state_specs + [_const_spec(p.shape) for p in params],
        out_specs=(chunk(D_ML), per_b(state_shapes[1]), per_b(state_shapes[2]), per_b(state_shapes[0])),
        scratch_shapes=[pltpu.VMEM((nb,) + state_shapes[1], F32), pltpu.VMEM((nb,) + state_shapes[2], F32),
                        pltpu.VMEM((nb,) + state_shapes[0], F32)],
        compiler_params=pltpu.CompilerParams(dimension_semantics=("arbitrary", "arbitrary"),
                                             vmem_limit_bytes=VMEM_LIMIT),
        name="mlstm",
    )(zm, zif, *state_args, *params)


def _merge_body(x_ref, ya_ref, yb_ref, pre_ref, post_ref, wg_ref, pa_ref, pb_ref, wo_ref, h_ref):
    x = x_ref[0]
    u = _rms(x, pre_ref[...]).astype(BF16)
    gate = _sigmoid(jnp.dot(u, wg_ref[...], preferred_element_type=F32))
    pa = jnp.dot(ya_ref[0].astype(BF16), pa_ref[...], preferred_element_type=F32)
    pb = jnp.dot(yb_ref[0].astype(BF16), pb_ref[...], preferred_element_type=F32)
    merged = gate[:, :D_MODEL] * pa + gate[:, D_MODEL:] * pb
    o = jnp.dot(merged.astype(BF16), wo_ref[...], preferred_element_type=F32)
    h_ref[0] = x + _rms(o, post_ref[...])


def _merge(x, ya, yb, pre, post, wg, pa, pb, wo, tm):
    B, T, _ = x.shape
    row = lambda w: pl.BlockSpec((1, tm, w), lambda b, i: (b, i, 0))
    consts = (pre, post, wg, pa, pb, wo)
    return pl.pallas_call(
        _merge_body,
        out_shape=jax.ShapeDtypeStruct((B, T, D_MODEL), F32),
        grid=(B, T // tm),
        in_specs=[row(D_MODEL), row(D_RW), row(D_ML)] + [_const_spec(p.shape) for p in consts],
        out_specs=row(D_MODEL),
        compiler_params=pltpu.CompilerParams(dimension_semantics=("arbitrary", "arbitrary"),
                                             vmem_limit_bytes=VMEM_LIMIT),
        name="merge",
    )(x, ya, yb, *consts)


FF_SPLIT = 4


def _ffn_body(tm, n_pad, h_ref, pre_ref, post_ref, wu_ref, wd_ref, o_ref):
    h = h_ref[0]
    u = _rms(h, pre_ref[...]).astype(BF16)
    step = D_FF // FF_SPLIT
    f = None
    for j in range(FF_SPLIT):
        t = jnp.maximum(jnp.dot(u, wu_ref[:, j * step:(j + 1) * step], preferred_element_type=F32), 0.0)
        part = jnp.dot((t * t).astype(BF16), wd_ref[j * step:(j + 1) * step, :], preferred_element_type=F32)
        f = part if f is None else f + part
    out = h + _rms(f, post_ref[...])
    if n_pad:
        out = jnp.where(pl.program_id(1) * tm + _iota((tm, 1), 0) < n_pad, 0.0, out)
    o_ref[0] = out


def _ffn(h, pre, post, wu, wd, n_pad, drop):
    B, T, _ = h.shape
    tm = _row_tile(T - drop)
    row = pl.BlockSpec((1, tm, D_MODEL), lambda b, i: (b, i, 0))
    in_row = row
    if drop:
        in_row = pl.BlockSpec((pl.Element(1), pl.Element(tm), pl.Element(D_MODEL)),
                              lambda b, i: (b, pl.multiple_of(drop + i * tm, math.gcd(drop, tm)), 0))
    consts = (pre, post, wu, wd)
    return pl.pallas_call(
        functools.partial(_ffn_body, tm, 0 if drop else n_pad),
        out_shape=jax.ShapeDtypeStruct((B, T - drop, D_MODEL), F32),
        grid=(B, (T - drop) // tm),
        in_specs=[in_row] + [_const_spec(p.shape) for p in consts],
        out_specs=row,
        compiler_params=pltpu.CompilerParams(dimension_semantics=("arbitrary", "arbitrary"),
                                             vmem_limit_bytes=VMEM_LIMIT),
        name="ffn",
    )(h, *consts)


MAX_ROW_TILE = 768


def _row_tile(T):
    return max(tm for tm in range(SUBLANES, MAX_ROW_TILE + 1, SUBLANES) if T % tm == 0)


def _layer(x, st, lp, L, n_pad, drop=0):
    B, T, _ = x.shape
    long_seq = T > L
    flat = (lambda a: a.reshape(1, B * T, a.shape[-1])) if not n_pad else (lambda a: a)
    unflat = lambda a: a.reshape(B, T, a.shape[-1])
    xf = flat(x)
    tm = _row_tile(xf.shape[1])
    z_rw, z_ml, z_if = map(unflat, _in_proj(xf, lp['pre1'], lp['w_rw'], lp['w_ml'], lp['w_if'], tm))

    rw_state = ml_state = None
    if st is not None:
        l, S, sh, C, n, m, cb = st
        depth = S.shape[0]
        rw_state = (l, sh.reshape(depth, B, 1, RW_SHIFT_W), S.reshape(depth, B, D_RW, RW_HD))
        cv0 = jnp.pad(cb[l], ((0, 0), (SUBLANES - (CONV_W - 1), 0), (0, 0)))
        nm0 = jnp.concatenate([n[l], jnp.pad(m[l], ((0, 0), (0, ML_HD - ML_HEADS)))[:, None, :],
                               jnp.zeros((B, SUBLANES - ML_HEADS - 1, ML_HD), F32)], axis=1)
        ml_state = (l, cv0, C.reshape(depth, B, D_ML, ML_HD), nm0)
    ya, S1, sh1 = _rwkv(z_rw, rw_state, lp['rw_params'], L, _group(B, RW_SEQS_LONG if long_seq else RW_SEQS_SHORT))
    yb, C1, nm1, cv1 = _mlstm(z_ml, z_if, ml_state, lp['ml_params'], L, n_pad,
                              _group(B, ML_SEQS_LONG if long_seq else ML_SEQS_SHORT))

    h = _merge(xf, flat(ya), flat(yb), lp['pre1'], lp['post1'], lp['w_gate'], lp['p_a'], lp['p_b'], lp['w_out'], tm)
    out = _ffn(h, lp['pre2'], lp['post2'], lp['w_ff_up'], lp['w_ff_down'], n_pad, drop)
    out = out if drop else unflat(out)
    new_state = (S1.reshape(B, RW_HEADS, RW_HD, RW_HD), sh1[:, 0, :], C1.reshape(B, ML_HEADS, ML_HD, ML_HD),
                 nm1[:, :ML_HEADS, :], nm1[:, ML_HEADS, :ML_HEADS], cv1[:, SUBLANES - (CONV_W - 1):, :])
    return out, new_state


def _layer_params(l, w_in, rw_mu, rw_w0, rw_w_up, rw_a0, rw_a_up, rw_g_up, rw_k_k, rw_k_a, rw_r_k, rw_gn_g,
                  rw_gn_b, ml_conv_w, ml_conv_b, ml_i_bias, ml_f_bias, ml_gn_g, p_a, p_b, w_out, pre1, post1,
                  pre2, post2, w_ff_up, w_ff_down):
    row = lambda a: a[l].reshape(1, -1).astype(F32)
    w = w_in[l]
    c_ml = RW_SHIFT_W
    c_if = c_ml + ML_MAIN_W
    c_gate = c_if + 2 * ML_HEADS
    half = RW_LORA_W // 2
    lora = jnp.zeros((RW_LORA_W, 2 * D_RW), F32)
    lora = lora.at[:half, :D_RW].set(rw_w_up[l]).at[half:, D_RW:].set(rw_a_up[l])
    lora_hi = lora.astype(BF16)
    lora_lo = (lora - lora_hi.astype(F32)).astype(BF16)
    lora = jnp.concatenate([lora_hi, lora_hi, lora_lo], axis=0)
    if_bias = jnp.zeros((1, GATE_W), F32)
    if_bias = if_bias.at[0, :ML_HEADS].set(ml_i_bias[l]).at[0, ML_HEADS:2 * ML_HEADS].set(ml_f_bias[l])
    return dict(
        pre1=row(pre1), post1=row(post1), pre2=row(pre2), post2=row(post2),
        w_rw=w[:, :c_ml].astype(BF16),
        w_ml=w[:, c_ml:c_if].astype(BF16),
        w_if=jnp.pad(w[:, c_if:c_gate], ((0, 0), (0, GATE_W - 2 * ML_HEADS))).astype(BF16),
        w_gate=w[:, c_gate:].astype(BF16),
        rw_params=(row(rw_mu), row(rw_w0), row(rw_a0), lora, rw_g_up[l].astype(BF16), row(rw_k_k), row(rw_k_a),
                   row(rw_r_k), row(rw_gn_g), row(rw_gn_b)),
        ml_params=(ml_conv_w[l].astype(F32), row(ml_conv_b), if_bias, row(ml_gn_g)),
        p_a=p_a[l].astype(BF16), p_b=p_b[l].astype(BF16), w_out=w_out[l].astype(BF16),
        w_ff_up=w_ff_up[l].astype(BF16), w_ff_down=w_ff_down[l].astype(BF16),
    )


def kernel(x_prompt, x_sample, state_rwkv_S, state_rwkv_shift, state_mlstm_C, state_mlstm_n, state_mlstm_m,
           state_mlstm_conv, meta_tokens, w_in, rw_mu, rw_w0, rw_w_up, rw_a0, rw_a_up, rw_g_up, rw_k_k, rw_k_a,
           rw_r_k, rw_gn_g, rw_gn_b, ml_conv_w, ml_conv_b, ml_i_bias, ml_f_bias, ml_gn_g, p_a, p_b, w_out, pre1,
           post1, pre2, post2, w_ff_up, w_ff_down):
    B, T, _ = x_prompt.shape
    dt = x_prompt.dtype
    depth = w_in.shape[0]
    L = PROMPT_CHUNK
    n_pad = (-(T + N_META)) % L
    xp = jnp.concatenate([jnp.zeros((B, n_pad, D_MODEL), dt),
                          jnp.broadcast_to(meta_tokens[None].astype(dt), (B, N_META, D_MODEL)), x_prompt], axis=1)
    xs = x_sample
    Bs, Ts, _ = xs.shape
    p_states, s_states = [], []
    for l in range(depth):
        lp = _layer_params(l, w_in, rw_mu, rw_w0, rw_w_up, rw_a0, rw_a_up, rw_g_up, rw_k_k, rw_k_a, rw_r_k,
                           rw_gn_g, rw_gn_b, ml_conv_w, ml_conv_b, ml_i_bias, ml_f_bias, ml_gn_g, p_a, p_b, w_out,
                           pre1, post1, pre2, post2, w_ff_up, w_ff_down)
        xp, st_p = _layer(xp, None, lp, L, n_pad, drop=n_pad + N_META if l == depth - 1 else 0)
        st_in = (l, state_rwkv_S, state_rwkv_shift, state_mlstm_C, state_mlstm_n, state_mlstm_m, state_mlstm_conv)
        xs, st_s = _layer(xs, st_in, lp, Ts, 0)
        p_states.append(st_p)
        s_states.append(st_s)
    stk = lambda lst, i: jnp.stack([s[i] for s in lst]).astype(dt)
    return (xp, xs,
            stk(p_states, 0), stk(p_states, 1), stk(p_states, 2), stk(p_states, 3), stk(p_states, 4), stk(p_states, 5),
            stk(s_states, 0), stk(s_states, 1), stk(s_states, 2), stk(s_states, 3), stk(s_states, 4), stk(s_states, 5))
```

```python
import functools
import math

import jax
import jax.numpy as jnp
from jax import lax
from jax.experimental import pallas as pl
from jax.experimental.pallas import tpu as pltpu

F32 = jnp.float32
BF16 = jnp.bfloat16

D_MODEL = 1024
N_META = 16
RW_HEADS = 8
RW_HD = 64
D_RW = RW_HEADS * RW_HD
RW_HALF_HEADS = RW_HEADS // 2
RW_HALF = RW_HALF_HEADS * RW_HD
RW_LORA_W = 128
RW_G_LORA = 128
RW_SHIFT_W = 3 * D_RW + RW_LORA_W + RW_G_LORA
RW_GN_EPS = 64e-5
ML_HEADS = 4
ML_HD = 128
D_ML = ML_HEADS * ML_HD
CONV_W = 4
ML_MAIN_W = 4 * D_ML
ML_GN_EPS = 1e-5
GATE_W = 128
D_FF = 4 * D_MODEL
RMS_EPS = 1e-6
SUBLANES = 8

PROMPT_CHUNK = 64
VMEM_LIMIT = 56 * 2**20
RW_SEQS_LONG, RW_SEQS_SHORT = 8, 16
ML_SEQS_LONG, ML_SEQS_SHORT = 8, 16


def _bmm_nn(a, b):
    return jnp.einsum('gmk,gkn->gmn', a.astype(BF16), b.astype(BF16), preferred_element_type=F32)


def _bmm_nt(a, b):
    return jnp.einsum('gmk,gnk->gmn', a.astype(BF16), b.astype(BF16), preferred_element_type=F32)


def _bmm_tn(a, b):
    return jnp.einsum('gkm,gkn->gmn', a.astype(BF16), b.astype(BF16), preferred_element_type=F32)


def _unit_lower_inverse(n):
    L = n.shape[-1]
    t = n + (_iota((1, L, L), 1) == _iota((1, L, L), 2)).astype(F32)
    p = n
    for _ in range(max(L.bit_length() - 2, 0)):
        p = _bmm_nn(p, p)
        t = t + _bmm_nn(t, p)
    return t


def _split(x, terms):
    parts = []
    for _ in range(terms - 1):
        p = x.astype(BF16)
        parts.append(p)
        x = x - p.astype(F32)
    parts.append(x.astype(BF16))
    return parts


def _sel_right(x, sel, terms):
    m, k = x.shape
    pieces = _split(x, terms)
    if terms == 1:
        return jnp.dot(pieces[0], sel, preferred_element_type=F32)
    if k % 128 == 0:
        return jnp.dot(jnp.concatenate(pieces, axis=1), jnp.concatenate([sel] * terms, axis=0),
                       preferred_element_type=F32)
    r = jnp.dot(jnp.concatenate(pieces, axis=0), sel, preferred_element_type=F32)
    return sum(r[i * m:(i + 1) * m] for i in range(terms))


def _cumsum_rows(x):
    row = _iota(x.shape, 1)
    s = 1
    while s < x.shape[1]:
        x = x + jnp.where(row >= s, pltpu.roll(x, s, 1), 0.0)
        s *= 2
    return x


def _sigmoid(x):
    return 1.0 / (1.0 + jnp.exp(-x))


def _iota(shape, dim):
    return lax.broadcasted_iota(jnp.int32, shape, dim)


def _rms(x, g):
    return x * lax.rsqrt(jnp.mean(x * x, axis=-1, keepdims=True) + RMS_EPS) * g


def _const_spec(shape):
    nd = len(shape)
    return pl.BlockSpec(shape, lambda *_: (0,) * nd, pipeline_mode=pl.Buffered(1))


def _group(n, target):
    return max(d for d in range(1, target + 1) if n % d == 0)


def _in_proj_body(x_ref, g_ref, wrw_ref, wml_ref, wif_ref, zrw_ref, zml_ref, zif_ref):
    u = _rms(x_ref[0], g_ref[...]).astype(BF16)
    zrw_ref[0] = jnp.dot(u, wrw_ref[...], preferred_element_type=F32)
    zml_ref[0] = jnp.dot(u, wml_ref[...], preferred_element_type=F32)
    zif_ref[0] = jnp.dot(u, wif_ref[...], preferred_element_type=F32)


def _in_proj(x, g, w_rw, w_ml, w_if, tm):
    B, T, _ = x.shape
    row = lambda w: pl.BlockSpec((1, tm, w), lambda b, i: (b, i, 0))
    return pl.pallas_call(
        _in_proj_body,
        out_shape=(jax.ShapeDtypeStruct((B, T, RW_SHIFT_W), F32),
                   jax.ShapeDtypeStruct((B, T, ML_MAIN_W), F32),
                   jax.ShapeDtypeStruct((B, T, GATE_W), F32)),
        grid=(B, T // tm),
        in_specs=[row(D_MODEL), _const_spec((1, D_MODEL)), _const_spec(w_rw.shape), _const_spec(w_ml.shape),
                  _const_spec(w_if.shape)],
        out_specs=(row(RW_SHIFT_W), row(ML_MAIN_W), row(GATE_W)),
        compiler_params=pltpu.CompilerParams(dimension_semantics=("arbitrary", "arbitrary"),
                                             vmem_limit_bytes=VMEM_LIMIT),
        name="in_proj",
    )(x, g, w_rw, w_ml, w_if)


def _rwkv_chunk(L, at, bt, kt, rt, b_end, k_end, v, decay_end, S, bd32):
    H = RW_HALF_HEADS
    g = at.shape[0]
    lane_head = _iota((1, 1, RW_HALF), 2) >> 6
    hmask = [(lane_head == h).astype(F32) for h in range(H)]
    if L % 16 == 0:
        hmask16 = [m.astype(BF16) for m in hmask]
        stack = lambda x: jnp.concatenate([x.astype(BF16) * hmask16[h] for h in range(H)], axis=1)
    else:
        stack = lambda x: jnp.concatenate([x * hmask[h] for h in range(H)], axis=1).astype(BF16)

    n = _bmm_nt(stack(at), bt).reshape(g * H, L, L)
    n = jnp.where(_iota((1, L, L), 1) > _iota((1, L, L), 2), n, 0.0)
    tinv = _unit_lower_inverse(n).reshape(g, H, L, L)

    v_stack = stack(v)
    ar = jnp.concatenate([at, rt], axis=1).astype(BF16)
    att = _bmm_nt(ar, jnp.concatenate([stack(bt), stack(kt)], axis=1))
    s_col = _iota((1, L, H * L), 2) & (L - 1)
    strict = s_col < _iota((1, L, H * L), 1)
    a_ab = jnp.where(strict, att[:, :L, :H * L], 0.0)
    a_ak = jnp.where(strict, att[:, :L, H * L:], 0.0)
    incl = (_iota((1, L, 2 * H * L), 2) & (L - 1)) <= _iota((1, L, 2 * H * L), 1)
    a_r = jnp.where(incl, att[:, L:, :], 0.0)

    from_state = _bmm_nt(ar, S)
    wmat = from_state[:, :L] + _bmm_nn(a_ak, v_stack)

    def solve(rhs):
        rhs = rhs.astype(BF16)
        x = hmask[0] * _bmm_nn(tinv[:, 0], rhs)
        for h in range(1, H):
            x = x + hmask[h] * _bmm_nn(tinv[:, h], rhs)
        return x

    u = solve(wmat)
    ab_hi, ab_lo = _split(a_ab, 2)
    u_hi, u_lo = _split(u, 2)
    u_hi_stack = stack(u_hi)
    nu = (jnp.einsum('gmk,gkn->gmn', jnp.concatenate([ab_hi, ab_lo], axis=2),
                     jnp.concatenate([u_hi_stack, u_hi_stack], axis=1), preferred_element_type=F32)
          + jnp.einsum('gmk,gkn->gmn', ab_hi, stack(u_lo), preferred_element_type=F32))
    u = u + solve(wmat - u + nu)
    y = from_state[:, L:] + _bmm_nn(a_r, jnp.concatenate([stack(u), v_stack], axis=1))
    upd = _bmm_tn(jnp.concatenate([u, v], axis=1), jnp.concatenate([b_end, k_end], axis=1))
    return y, S * decay_end + upd * bd32


def _rwkv_body(L, nc, nb, has_state, z_ref, *refs):
    if has_state:
        sh0_ref, s0_ref, *refs = refs
    (mu_ref, w0_ref, a0_ref, wc_ref, gup_ref, kk_ref, ka_ref, rk_ref, gng_ref, gnb_ref, bd16_ref, bd32_ref,
     y_ref, sout_ref, shout_ref, s_scr, prev_scr) = refs
    c = pl.program_id(1)
    D, DH = D_RW, RW_HALF
    bd16 = bd16_ref[...]
    bd32 = bd32_ref[...]

    def head_sum(x, terms):
        m = x.shape[1]
        x2 = x.reshape(nb * m, D)
        r = jnp.concatenate([_sel_right(x2[:, hf * DH:(hf + 1) * DH], bd16, terms) for hf in range(2)], axis=1)
        return r.reshape(nb, m, D)

    halves = lambda x: jnp.stack([x[:, :, :DH], x[:, :, DH:]], axis=1).reshape(2 * nb, x.shape[1], DH)

    @pl.when(c == 0)
    def _():
        if has_state:
            prev_scr[...] = sh0_ref[...]
            s0 = s0_ref[...].reshape(2 * nb, DH, RW_HD)
            s_scr[...] = jnp.concatenate([s0] * RW_HALF_HEADS, axis=2) * bd32
        else:
            prev_scr[...] = jnp.zeros(prev_scr.shape, F32)
            s_scr[...] = jnp.zeros(s_scr.shape, F32)

    z = z_ref[...]
    prev = jnp.where(_iota(z.shape, 1) == 0, prev_scr[...], pltpu.roll(z, 1, 1))
    prev_scr[...] = z[:, L - 1:L, :]
    zs = z + (prev - z) * mu_ref[...]
    r = zs[:, :, 0:D]
    kraw = zs[:, :, D:2 * D]
    v = zs[:, :, 2 * D:3 * D]
    xl = zs[:, :, 3 * D:3 * D + RW_LORA_W]
    gl = zs[:, :, 3 * D + RW_LORA_W:]
    xl = jnp.where(_iota(xl.shape, 2) < RW_LORA_W // 2, jnp.tanh(xl), xl)
    xl_hi, xl_lo = _split(xl, 2)
    lora = jnp.dot(jnp.concatenate([xl_hi, xl_lo, xl_hi], axis=2).reshape(nb * L, 3 * RW_LORA_W), wc_ref[...],
                   preferred_element_type=F32).reshape(nb, L, 2 * D)
    logw = -math.exp(-0.5) * _sigmoid(w0_ref[...] + lora[:, :, :D])
    a = _sigmoid(a0_ref[...] + lora[:, :, D:])
    g = jnp.dot(_sigmoid(gl).reshape(nb * L, RW_G_LORA).astype(BF16), gup_ref[...],
                preferred_element_type=F32).reshape(nb, L, D)
    kk = kraw * kk_ref[...]
    k = kraw * (1.0 + (a - 1.0) * ka_ref[...])
    sums = head_sum(jnp.concatenate([kk * kk, r * k * rk_ref[...]], axis=1), 1)
    kk = kk * lax.rsqrt(jnp.maximum(sums[:, :L], 1e-24))
    bonus = sums[:, L:]

    cl = _cumsum_rows(logw)
    cl_last = cl[:, L - 1:L, :]
    e_neg = jnp.exp(-cl)
    e_end = jnp.exp(cl_last - cl)
    kka = kk * a
    y_g, s_new = _rwkv_chunk(L, halves(-kk * jnp.exp(cl - logw)), halves(kka * e_neg), halves(k * e_neg),
                             halves(r * jnp.exp(cl)), halves(kka * e_end), halves(k * e_end), halves(v),
                             halves(jnp.exp(cl_last)), s_scr[...], bd32)
    s_scr[...] = s_new
    y_g = y_g.reshape(nb, 2, L, DH)
    y = jnp.concatenate([y_g[:, 0], y_g[:, 1]], axis=2)

    stats = head_sum(jnp.concatenate([y, y * y], axis=1), 2) * (1.0 / RW_HD)
    mean = stats[:, :L]
    var = jnp.maximum(stats[:, L:] - mean * mean, 0.0)
    yn = (y - mean) * lax.rsqrt(var + RW_GN_EPS) * gng_ref[...] + gnb_ref[...]
    y_ref[...] = (yn + bonus * v) * g

    @pl.when(c == nc - 1)
    def _():
        compact = sum(s_new[:, :, h * RW_HD:(h + 1) * RW_HD] for h in range(RW_HALF_HEADS))
        sout_ref[...] = compact.reshape(nb, D, RW_HD)
        shout_ref[...] = z[:, L - 1:L, :]


def _rwkv(z, state, params, L, nb):
    B, T, _ = z.shape
    nc = T // L
    per_b = lambda shp: pl.BlockSpec((nb,) + shp, lambda b, c: (b, 0, 0))
    bd = (_iota((RW_HALF, RW_HALF), 0) >> 6) == (_iota((RW_HALF, RW_HALF), 1) >> 6)
    consts = tuple(params) + (bd.astype(BF16), bd.astype(F32))
    state_args, state_specs = (), []
    if state is not None:
        layer, shift0, s0 = state
        of_layer = lambda shp: pl.BlockSpec((None, nb) + shp, lambda b, c: (layer, b, 0, 0))
        state_args, state_specs = (shift0, s0), [of_layer((1, RW_SHIFT_W)), of_layer((D_RW, RW_HD))]
    return pl.pallas_call(
        functools.partial(_rwkv_body, L, nc, nb, state is not None),
        out_shape=(jax.ShapeDtypeStruct((B, T, D_RW), F32),
                   jax.ShapeDtypeStruct((B, D_RW, RW_HD), F32),
                   jax.ShapeDtypeStruct((B, 1, RW_SHIFT_W), F32)),
        grid=(B // nb, nc),
        in_specs=[pl.BlockSpec((nb, L, RW_SHIFT_W), lambda b, c: (b, c, 0))] + state_specs
                 + [_const_spec(p.shape) for p in consts],
        out_specs=(pl.BlockSpec((nb, L, D_RW), lambda b, c: (b, c, 0)),
                   per_b((D_RW, RW_HD)), per_b((1, RW_SHIFT_W))),
        scratch_shapes=[pltpu.VMEM((2 * nb, RW_HALF, RW_HALF), F32), pltpu.VMEM((nb, 1, RW_SHIFT_W), F32)],
        compiler_params=pltpu.CompilerParams(dimension_semantics=("arbitrary", "arbitrary"),
                                             vmem_limit_bytes=VMEM_LIMIT),
        name="rwkv",
    )(z, *state_args, *consts)


def _mlstm_body(L, nc, nb, has_state, zm_ref, zif_ref, *refs):
    if has_state:
        cv0_ref, c0_ref, nm0_ref, *refs = refs
    cw_ref, cb_ref, ifb_ref, gn_ref, y_ref, cout_ref, nmout_ref, cvout_ref, c_scr, nm_scr, cv_scr = refs
    c = pl.program_id(1)
    H, HD, D = ML_HEADS, ML_HD, D_ML
    G = nb * H

    @pl.when(c == 0)
    def _():
        if has_state:
            cv_scr[...] = cv0_ref[...]
            c_scr[...] = c0_ref[...]
            nm_scr[...] = nm0_ref[...]
        else:
            cv_scr[...] = jnp.zeros(cv_scr.shape, F32)
            c_scr[...] = jnp.zeros(c_scr.shape, F32)
            nm_scr[...] = jnp.zeros(nm_scr.shape, F32)

    heads = lambda x: jnp.stack([x[:, :, h * HD:(h + 1) * HD] for h in range(H)], axis=1).reshape(G, x.shape[1], HD)
    lane_pick = lambda x, lo: jnp.stack([x[:, :, lo + h:lo + h + 1] for h in range(H)], axis=1).reshape(G, x.shape[1], 1)
    row_pick = lambda x, lo: jnp.stack([x[:, lo + h:lo + h + 1, :] for h in range(H)], axis=1).reshape(G, 1, x.shape[2])

    zm = zm_ref[...]
    raw = zm[:, :, :2 * D]
    v = zm[:, :, 2 * D:3 * D]
    o = zm[:, :, 3 * D:]
    ext = jnp.concatenate([cv_scr[...], raw], axis=1)
    cv_scr[...] = ext[:, L:, :]
    cw = cw_ref[...]
    qk = cb_ref[...] + raw * cw[CONV_W - 1:CONV_W, :]
    for s in range(1, CONV_W):
        qk = qk + pltpu.roll(ext, s, 1)[:, SUBLANES:, :] * cw[CONV_W - 1 - s:CONV_W - s, :]
    qk = qk * _sigmoid(qk)
    qh = heads(qk[:, :, :D])
    kh = heads(qk[:, :, D:] * (HD ** -0.5))
    vh = heads(v)

    gi = zif_ref[...] + ifb_ref[...]
    lane = _iota(gi.shape, 2)
    lf = jnp.minimum(gi, 0.0) - jnp.log(1.0 + jnp.exp(-jnp.abs(gi)))
    gcol = jnp.where(lane < H, gi, jnp.where(lane < 2 * H, lf, 0.0))
    causal = _iota((1, L, L), 1) >= _iota((1, L, L), 2)
    b_col = _cumsum_rows(gcol)
    grow = jnp.swapaxes(gcol, 1, 2)
    b_row = jnp.swapaxes(b_col, 1, 2)
    bc, ic = lane_pick(b_col, H), lane_pick(gcol, 0)
    br, ir = row_pick(b_row, H), row_pick(grow, 0)

    nm = nm_scr[...]
    ch = c_scr[...].reshape(G, HD, HD)
    nh = row_pick(nm, 0)
    m_prev = lane_pick(nm[:, H:H + 1, :], 0)

    dlog = jnp.where(causal, bc - br + ir, -jnp.inf)
    inter = bc + m_prev
    m_t = jnp.maximum(inter, jnp.max(dlog, axis=-1, keepdims=True))
    amat = jnp.exp(dlog - m_t)
    sc = jnp.exp(inter - m_t)
    aqk = amat * _bmm_nt(qh, kh)
    num = _bmm_nn(aqk, vh) + sc * _bmm_nt(qh, ch)
    den = jnp.sum(aqk, axis=-1, keepdims=True) + sc * jnp.sum(qh * nh, axis=-1, keepdims=True)
    hh = num / jnp.maximum(jnp.abs(den), jnp.exp(-m_t))
    m_new = m_t[:, L - 1:L, :]
    b_last = bc[:, L - 1:L, :]
    wc = jnp.exp(b_last - bc + ic - m_new)
    dec = jnp.exp(b_last + m_prev - m_new)
    c_scr[...] = (dec * ch + _bmm_tn(vh * wc, kh)).reshape(nb, D, HD)
    n_new = (dec * nh + jnp.sum(kh * wc, axis=1, keepdims=True)).reshape(nb, H, 1, HD)
    m_new = m_new.reshape(nb, H, 1, 1)
    m_row = jnp.zeros((nb, 1, HD), F32)
    for h in range(H):
        m_row = jnp.where(_iota(m_row.shape, 2) == h, m_new[:, h], m_row)
    nm_scr[...] = jnp.concatenate([n_new[:, h] for h in range(H)] + [m_row, nm[:, H + 1:, :]], axis=1)

    mu = jnp.mean(hh, axis=-1, keepdims=True)
    xc = hh - mu
    var = jnp.mean(xc * xc, axis=-1, keepdims=True)
    gn = jnp.concatenate([gn_ref[:, h * HD:(h + 1) * HD][None] for h in range(H)] * nb, axis=0)
    out = (xc * lax.rsqrt(var + ML_GN_EPS) * gn * _sigmoid(heads(o))).reshape(nb, H, L, HD)
    y_ref[...] = jnp.concatenate([out[:, h] for h in range(H)], axis=2)

    @pl.when(c == nc - 1)
    def _():
        cout_ref[...] = c_scr[...]
        nmout_ref[...] = nm_scr[...]
        cvout_ref[...] = cv_scr[...]


def _mlstm(zm, zif, state, params, L, nb):
    B, T, _ = zm.shape
    nc = T // L
    per_b = lambda shp: pl.BlockSpec((nb,) + shp, lambda b, c: (b, 0, 0))
    chunk = lambda w: pl.BlockSpec((nb, L, w), lambda b, c: (b, c, 0))
    state_shapes = ((SUBLANES, 2 * D_ML), (D_ML, ML_HD), (SUBLANES, ML_HD))
    state_args, state_specs = (), []
    if state is not None:
        layer, cv0, c0, nm0 = state
        state_args = (cv0, c0, nm0)
        state_specs = [per_b(state_shapes[0]),
                       pl.BlockSpec((None, nb) + state_shapes[1], lambda b, c: (layer, b, 0, 0)),
                       per_b(state_shapes[2])]
    return pl.pallas_call(
        functools.partial(_mlstm_body, L, nc, nb, state is not None),
        out_shape=(jax.ShapeDtypeStruct((B, T, D_ML), F32),
                   jax.ShapeDtypeStruct((B,) + state_shapes[1], F32),
                   jax.ShapeDtypeStruct((B,) + state_shapes[2], F32),
                   jax.ShapeDtypeStruct((B,) + state_shapes[0], F32)),
        grid=(B // nb, nc),
        in_specs=[chunk(ML_MAIN_W), chunk(GATE_W)] + state_specs + [_const_spec(p.shape) for p in params],
        out_specs=(chunk(D_ML), per_b(state_shapes[1]), per_b(state_shapes[2]), per_b(state_shapes[0])),
        scratch_shapes=[pltpu.VMEM((nb,) + state_shapes[1], F32), pltpu.VMEM((nb,) + state_shapes[2], F32),
                        pltpu.VMEM((nb,) + state_shapes[0], F32)],
        compiler_params=pltpu.CompilerParams(dimension_semantics=("arbitrary", "arbitrary"),
                                             vmem_limit_bytes=VMEM_LIMIT),
        name="mlstm",
    )(zm, zif, *state_args, *params)


def _merge_body(x_ref, ya_ref, yb_ref, pre_ref, post_ref, wg_ref, pa_ref, pb_ref, wo_ref, h_ref):
    x = x_ref[0]
    u = _rms(x, pre_ref[...]).astype(BF16)
    gate = _sigmoid(jnp.dot(u, wg_ref[...], preferred_element_type=F32))
    pa = jnp.dot(ya_ref[0].astype(BF16), pa_ref[...], preferred_element_type=F32)
    pb = jnp.dot(yb_ref[0].astype(BF16), pb_ref[...], preferred_element_type=F32)
    merged = gate[:, :D_MODEL] * pa + gate[:, D_MODEL:] * pb
    o = jnp.dot(merged.astype(BF16), wo_ref[...], preferred_element_type=F32)
    h_ref[0] = x + _rms(o, post_ref[...])


def _merge(x, ya, yb, pre, post, wg, pa, pb, wo, tm):
    B, T, _ = x.shape
    row = lambda w: pl.BlockSpec((1, tm, w), lambda b, i: (b, i, 0))
    consts = (pre, post, wg, pa, pb, wo)
    return pl.pallas_call(
        _merge_body,
        out_shape=jax.ShapeDtypeStruct((B, T, D_MODEL), F32),
        grid=(B, T // tm),
        in_specs=[row(D_MODEL), row(D_RW), row(D_ML)] + [_const_spec(p.shape) for p in consts],
        out_specs=row(D_MODEL),
        compiler_params=pltpu.CompilerParams(dimension_semantics=("arbitrary", "arbitrary"),
                                             vmem_limit_bytes=VMEM_LIMIT),
        name="merge",
    )(x, ya, yb, *consts)


FF_SPLIT = 4


def _ffn_body(h_ref, pre_ref, post_ref, wu_ref, wd_ref, o_ref):
    h = h_ref[0]
    u = _rms(h, pre_ref[...]).astype(BF16)
    step = D_FF // FF_SPLIT
    f = None
    for j in range(FF_SPLIT):
        t = jnp.maximum(jnp.dot(u, wu_ref[:, j * step:(j + 1) * step], preferred_element_type=F32), 0.0)
        part = jnp.dot((t * t).astype(BF16), wd_ref[j * step:(j + 1) * step, :], preferred_element_type=F32)
        f = part if f is None else f + part
    o_ref[0] = h + _rms(f, post_ref[...])


def _ffn(h, pre, post, wu, wd, tm):
    B, T, _ = h.shape
    row = pl.BlockSpec((1, tm, D_MODEL), lambda b, i: (b, i, 0))
    consts = (pre, post, wu, wd)
    return pl.pallas_call(
        _ffn_body,
        out_shape=jax.ShapeDtypeStruct((B, T, D_MODEL), F32),
        grid=(B, T // tm),
        in_specs=[row] + [_const_spec(p.shape) for p in consts],
        out_specs=row,
        compiler_params=pltpu.CompilerParams(dimension_semantics=("arbitrary", "arbitrary"),
                                             vmem_limit_bytes=VMEM_LIMIT),
        name="ffn",
    )(h, *consts)


MAX_ROW_TILE = 768


def _row_tile(T):
    return max(tm for tm in range(SUBLANES, MAX_ROW_TILE + 1, SUBLANES) if T % tm == 0)


def _layer(x, st, lp, L):
    B, T, _ = x.shape
    long_seq = T > L
    flat = lambda a: a.reshape(1, B * T, a.shape[-1])
    unflat = lambda a: a.reshape(B, T, a.shape[-1])
    xf = flat(x)
    tm = _row_tile(B * T)
    z_rw, z_ml, z_if = map(unflat, _in_proj(xf, lp['pre1'], lp['w_rw'], lp['w_ml'], lp['w_if'], tm))

    rw_state = ml_state = None
    if st is not None:
        l, S, sh, C, n, m, cb = st
        depth = S.shape[0]
        rw_state = (l, sh.reshape(depth, B, 1, RW_SHIFT_W), S.reshape(depth, B, D_RW, RW_HD))
        cv0 = jnp.pad(cb[l], ((0, 0), (SUBLANES - (CONV_W - 1), 0), (0, 0)))
        nm0 = jnp.concatenate([n[l], jnp.pad(m[l], ((0, 0), (0, ML_HD - ML_HEADS)))[:, None, :],
                               jnp.zeros((B, SUBLANES - ML_HEADS - 1, ML_HD), F32)], axis=1)
        ml_state = (l, cv0, C.reshape(depth, B, D_ML, ML_HD), nm0)
    ya, S1, sh1 = _rwkv(z_rw, rw_state, lp['rw_params'], L, _group(B, RW_SEQS_LONG if long_seq else RW_SEQS_SHORT))
    yb, C1, nm1, cv1 = _mlstm(z_ml, z_if, ml_state, lp['ml_params'], L,
                              _group(B, ML_SEQS_LONG if long_seq else ML_SEQS_SHORT))

    h = _merge(xf, flat(ya), flat(yb), lp['pre1'], lp['post1'], lp['w_gate'], lp['p_a'], lp['p_b'], lp['w_out'], tm)
    out = unflat(_ffn(h, lp['pre2'], lp['post2'], lp['w_ff_up'], lp['w_ff_down'], tm))
    new_state = (S1.reshape(B, RW_HEADS, RW_HD, RW_HD), sh1[:, 0, :], C1.reshape(B, ML_HEADS, ML_HD, ML_HD),
                 nm1[:, :ML_HEADS, :], nm1[:, ML_HEADS, :ML_HEADS], cv1[:, SUBLANES - (CONV_W - 1):, :])
    return out, new_state


def _layer_params(l, w_in, rw_mu, rw_w0, rw_w_up, rw_a0, rw_a_up, rw_g_up, rw_k_k, rw_k_a, rw_r_k, rw_gn_g,
                  rw_gn_b, ml_conv_w, ml_conv_b, ml_i_bias, ml_f_bias, ml_gn_g, p_a, p_b, w_out, pre1, post1,
                  pre2, post2, w_ff_up, w_ff_down):
    row = lambda a: a[l].reshape(1, -1).astype(F32)
    w = w_in[l]
    c_ml = RW_SHIFT_W
    c_if = c_ml + ML_MAIN_W
    c_gate = c_if + 2 * ML_HEADS
    half = RW_LORA_W // 2
    lora = jnp.zeros((RW_LORA_W, 2 * D_RW), F32)
    lora = lora.at[:half, :D_RW].set(rw_w_up[l]).at[half:, D_RW:].set(rw_a_up[l])
    lora_hi = lora.astype(BF16)
    lora_lo = (lora - lora_hi.astype(F32)).astype(BF16)
    lora = jnp.concatenate([lora_hi, lora_hi, lora_lo], axis=0)
    if_bias = jnp.zeros((1, GATE_W), F32)
    if_bias = if_bias.at[0, :ML_HEADS].set(ml_i_bias[l]).at[0, ML_HEADS:2 * ML_HEADS].set(ml_f_bias[l])
    return dict(
        pre1=row(pre1), post1=row(post1), pre2=row(pre2), post2=row(post2),
        w_rw=w[:, :c_ml].astype(BF16),
        w_ml=w[:, c_ml:c_if].astype(BF16),
        w_if=jnp.pad(w[:, c_if:c_gate], ((0, 0), (0, GATE_W - 2 * ML_HEADS))).astype(BF16),
        w_gate=w[:, c_gate:].astype(BF16),
        rw_params=(row(rw_mu), row(rw_w0), row(rw_a0), lora, rw_g_up[l].astype(BF16), row(rw_k_k), row(rw_k_a),
                   row(rw_r_k), row(rw_gn_g), row(rw_gn_b)),
        ml_params=(ml_conv_w[l].astype(F32), row(ml_conv_b), if_bias, row(ml_gn_g)),
        p_a=p_a[l].astype(BF16), p_b=p_b[l].astype(BF16), w_out=w_out[l].astype(BF16),
        w_ff_up=w_ff_up[l].astype(BF16), w_ff_down=w_ff_down[l].astype(BF16),
    )


def kernel(x_prompt, x_sample, state_rwkv_S, state_rwkv_shift, state_mlstm_C, state_mlstm_n, state_mlstm_m,
           state_mlstm_conv, meta_tokens, w_in, rw_mu, rw_w0, rw_w_up, rw_a0, rw_a_up, rw_g_up, rw_k_k, rw_k_a,
           rw_r_k, rw_gn_g, rw_gn_b, ml_conv_w, ml_conv_b, ml_i_bias, ml_f_bias, ml_gn_g, p_a, p_b, w_out, pre1,
           post1, pre2, post2, w_ff_up, w_ff_down):
    B, T, _ = x_prompt.shape
    dt = x_prompt.dtype
    depth = w_in.shape[0]
    xm = meta_tokens[None].astype(dt)
    xp = x_prompt
    xs = x_sample
    p_states, s_states = [], []
    for l in range(depth):
        lp = _layer_params(l, w_in, rw_mu, rw_w0, rw_w_up, rw_a0, rw_a_up, rw_g_up, rw_k_k, rw_k_a, rw_r_k,
                           rw_gn_g, rw_gn_b, ml_conv_w, ml_conv_b, ml_i_bias, ml_f_bias, ml_gn_g, p_a, p_b, w_out,
                           pre1, post1, pre2, post2, w_ff_up, w_ff_down)
        xm, st_m = _layer(xm, None, lp, N_META)
        after_meta = tuple(jnp.broadcast_to(a, (1, B) + a.shape[1:]) for a in st_m)
        xp, st_p = _layer(xp, (0,) + after_meta, lp, PROMPT_CHUNK)
        st_in = (l, state_rwkv_S, state_rwkv_shift, state_mlstm_C, state_mlstm_n, state_mlstm_m, state_mlstm_conv)
        xs, st_s = _layer(xs, st_in, lp, xs.shape[1])
        p_states.append(st_p)
        s_states.append(st_s)
    stk = lambda lst, i: jnp.stack([s[i] for s in lst]).astype(dt)
    return (xp, xs,
            stk(p_states, 0), stk(p_states, 1), stk(p_states, 2), stk(p_states, 3), stk(p_states, 4), stk(p_states, 5),
            stk(s_states, 0), stk(s_states, 1), stk(s_states, 2), stk(s_states, 3), stk(s_states, 4), stk(s_states, 5))
```

```python
import functools
import math

import jax
import jax.numpy as jnp
from jax import lax
from jax.experimental import pallas as pl
from jax.experimental.pallas import tpu as pltpu

F32 = jnp.float32
BF16 = jnp.bfloat16

D_MODEL = 1024
N_META = 16
RW_HEADS = 8
RW_HD = 64
D_RW = RW_HEADS * RW_HD
RW_HALF_HEADS = RW_HEADS // 2
RW_HALF = RW_HALF_HEADS * RW_HD
RW_LORA_W = 128
RW_G_LORA = 128
RW_SHIFT_W = 3 * D_RW + RW_LORA_W + RW_G_LORA
RW_GN_EPS = 64e-5
ML_HEADS = 4
ML_HD = 128
D_ML = ML_HEADS * ML_HD
CONV_W = 4
ML_MAIN_W = 4 * D_ML
ML_GN_EPS = 1e-5
GATE_W = 128
D_FF = 4 * D_MODEL
RMS_EPS = 1e-6
SUBLANES = 8

PROMPT_CHUNK = 64
VMEM_LIMIT = 56 * 2**20
RW_SEQS_LONG, RW_SEQS_SHORT = 8, 16
ML_SEQS_LONG, ML_SEQS_SHORT = 8, 16


def _bmm_nn(a, b):
    return jnp.einsum('gmk,gkn->gmn', a.astype(BF16), b.astype(BF16), preferred_element_type=F32)


def _bmm_nt(a, b):
    return jnp.einsum('gmk,gnk->gmn', a.astype(BF16), b.astype(BF16), preferred_element_type=F32)


def _bmm_tn(a, b):
    return jnp.einsum('gkm,gkn->gmn', a.astype(BF16), b.astype(BF16), preferred_element_type=F32)


def _neumann_half(n):
    L = n.shape[-1]
    t = n + (_iota((1, L, L), 1) == _iota((1, L, L), 2)).astype(F32)
    p = n
    for _ in range(max(L.bit_length() - 3, 0)):
        p = _bmm_nn(p, p)
        t = t + _bmm_nn(t, p)
    return t


def _split(x, terms):
    parts = []
    for _ in range(terms - 1):
        p = x.astype(BF16)
        parts.append(p)
        x = x - p.astype(F32)
    parts.append(x.astype(BF16))
    return parts


def _sel_right(x, sel, terms):
    m, k = x.shape
    pieces = _split(x, terms)
    if terms == 1:
        return jnp.dot(pieces[0], sel, preferred_element_type=F32)
    if k % 128 == 0:
        return jnp.dot(jnp.concatenate(pieces, axis=1), jnp.concatenate([sel] * terms, axis=0),
                       preferred_element_type=F32)
    r = jnp.dot(jnp.concatenate(pieces, axis=0), sel, preferred_element_type=F32)
    return sum(r[i * m:(i + 1) * m] for i in range(terms))


def _cumsum_rows(x):
    row = _iota(x.shape, 1)
    s = 1
    while s < x.shape[1]:
        x = x + jnp.where(row >= s, pltpu.roll(x, s, 1), 0.0)
        s *= 2
    return x


def _sigmoid(x):
    return 1.0 / (1.0 + jnp.exp(-x))


def _iota(shape, dim):
    return lax.broadcasted_iota(jnp.int32, shape, dim)


def _rms(x, g):
    return x * lax.rsqrt(jnp.mean(x * x, axis=-1, keepdims=True) + RMS_EPS) * g


def _const_spec(shape):
    nd = len(shape)
    return pl.BlockSpec(shape, lambda *_: (0,) * nd, pipeline_mode=pl.Buffered(1))


def _group(n, target):
    return max(d for d in range(1, target + 1) if n % d == 0)


def _in_proj_body(x_ref, g_ref, wrw_ref, wml_ref, wif_ref, zrw_ref, zml_ref, zif_ref):
    u = _rms(x_ref[0], g_ref[...]).astype(BF16)
    zrw_ref[0] = jnp.dot(u, wrw_ref[...], preferred_element_type=F32)
    zml_ref[0] = jnp.dot(u, wml_ref[...], preferred_element_type=F32)
    zif_ref[0] = jnp.dot(u, wif_ref[...], preferred_element_type=F32)


def _in_proj(x, g, w_rw, w_ml, w_if, tm):
    B, T, _ = x.shape
    row = lambda w: pl.BlockSpec((1, tm, w), lambda b, i: (b, i, 0))
    return pl.pallas_call(
        _in_proj_body,
        out_shape=(jax.ShapeDtypeStruct((B, T, RW_SHIFT_W), F32),
                   jax.ShapeDtypeStruct((B, T, ML_MAIN_W), F32),
                   jax.ShapeDtypeStruct((B, T, GATE_W), F32)),
        grid=(B, T // tm),
        in_specs=[row(D_MODEL), _const_spec((1, D_MODEL)), _const_spec(w_rw.shape), _const_spec(w_ml.shape),
                  _const_spec(w_if.shape)],
        out_specs=(row(RW_SHIFT_W), row(ML_MAIN_W), row(GATE_W)),
        compiler_params=pltpu.CompilerParams(dimension_semantics=("arbitrary", "arbitrary"),
                                             vmem_limit_bytes=VMEM_LIMIT),
        name="in_proj",
    )(x, g, w_rw, w_ml, w_if)


def _rwkv_chunk(L, at, bt, kt, rt, b_end, k_end, v, decay_end, S, bd32):
    H = RW_HALF_HEADS
    g = at.shape[0]
    lane_head = _iota((1, 1, RW_HALF), 2) >> 6
    hmask = [(lane_head == h).astype(F32) for h in range(H)]
    if L % 16 == 0:
        hmask16 = [m.astype(BF16) for m in hmask]
        stack = lambda x: jnp.concatenate([x.astype(BF16) * hmask16[h] for h in range(H)], axis=1)
    else:
        stack = lambda x: jnp.concatenate([x * hmask[h] for h in range(H)], axis=1).astype(BF16)

    n = _bmm_nt(stack(at), bt).reshape(g * H, L, L)
    n = jnp.where(_iota((1, L, L), 1) > _iota((1, L, L), 2), n, 0.0)
    tinv = _neumann_half(n).reshape(g, H, L, L)

    v_stack = stack(v)
    ar = jnp.concatenate([at, rt], axis=1).astype(BF16)
    att = _bmm_nt(ar, jnp.concatenate([stack(bt), stack(kt)], axis=1))
    s_col = _iota((1, L, H * L), 2) & (L - 1)
    strict = s_col < _iota((1, L, H * L), 1)
    a_ab = jnp.where(strict, att[:, :L, :H * L], 0.0)
    a_ak = jnp.where(strict, att[:, :L, H * L:], 0.0)
    incl = (_iota((1, L, 2 * H * L), 2) & (L - 1)) <= _iota((1, L, 2 * H * L), 1)
    a_r = jnp.where(incl, att[:, L:, :], 0.0)

    from_state = _bmm_nt(ar, S)
    wmat = from_state[:, :L] + _bmm_nn(a_ak, v_stack)

    def solve(rhs):
        rhs = rhs.astype(BF16)
        x = hmask[0] * _bmm_nn(tinv[:, 0], rhs)
        for h in range(1, H):
            x = x + hmask[h] * _bmm_nn(tinv[:, h], rhs)
        return x

    u = solve(wmat)
    ab_hi, ab_lo = _split(a_ab, 2)
    u_hi, u_lo = _split(u, 2)
    u_hi_stack = stack(u_hi)
    nu = (jnp.einsum('gmk,gkn->gmn', jnp.concatenate([ab_hi, ab_lo], axis=2),
                     jnp.concatenate([u_hi_stack, u_hi_stack], axis=1), preferred_element_type=F32)
          + jnp.einsum('gmk,gkn->gmn', ab_hi, stack(u_lo), preferred_element_type=F32))
    u = u + solve(wmat - u + nu)
    y = from_state[:, L:] + _bmm_nn(a_r, jnp.concatenate([stack(u), v_stack], axis=1))
    upd = _bmm_tn(jnp.concatenate([u.astype(v.dtype), v], axis=1), jnp.concatenate([b_end, k_end], axis=1))
    return y, S * decay_end + upd * bd32


def _rwkv_body(L, nc, nb, has_state, z_ref, *refs):
    if has_state:
        sh0_ref, s0_ref, *refs = refs
    (mu_ref, w0_ref, a0_ref, wc_ref, gup_ref, kk_ref, ka_ref, rk_ref, gng_ref, gnb_ref, bd16_ref, bd32_ref,
     y_ref, sout_ref, shout_ref, s_scr, prev_scr) = refs
    c = pl.program_id(1)
    D, DH = D_RW, RW_HALF
    bd16 = bd16_ref[...]
    bd32 = bd32_ref[...]

    def head_sum(x, terms):
        m = x.shape[1]
        x2 = x.reshape(nb * m, D)
        r = jnp.concatenate([_sel_right(x2[:, hf * DH:(hf + 1) * DH], bd16, terms) for hf in range(2)], axis=1)
        return r.reshape(nb, m, D)

    halves = lambda x: jnp.stack([x[:, :, :DH], x[:, :, DH:]], axis=1).reshape(2 * nb, x.shape[1], DH)

    @pl.when(c == 0)
    def _():
        if has_state:
            prev_scr[...] = sh0_ref[...]
            s0 = s0_ref[...].reshape(2 * nb, DH, RW_HD)
            s_scr[...] = jnp.concatenate([s0] * RW_HALF_HEADS, axis=2) * bd32
        else:
            prev_scr[...] = jnp.zeros(prev_scr.shape, F32)
            s_scr[...] = jnp.zeros(s_scr.shape, F32)

    z = z_ref[...]
    prev = jnp.where(_iota(z.shape, 1) == 0, prev_scr[...], pltpu.roll(z, 1, 1))
    prev_scr[...] = z[:, L - 1:L, :]
    zs = z + (prev - z) * mu_ref[...]
    r = zs[:, :, 0:D]
    kraw = zs[:, :, D:2 * D]
    v = zs[:, :, 2 * D:3 * D]
    xl = zs[:, :, 3 * D:3 * D + RW_LORA_W]
    gl = zs[:, :, 3 * D + RW_LORA_W:]
    xl = jnp.where(_iota(xl.shape, 2) < RW_LORA_W // 2, jnp.tanh(xl), xl)
    xl_hi, xl_lo = _split(xl, 2)
    lora = jnp.dot(jnp.concatenate([xl_hi, xl_lo, xl_hi], axis=2).reshape(nb * L, 3 * RW_LORA_W), wc_ref[...],
                   preferred_element_type=F32).reshape(nb, L, 2 * D)
    logw = -math.exp(-0.5) * _sigmoid(w0_ref[...] + lora[:, :, :D])
    a = _sigmoid(a0_ref[...] + lora[:, :, D:])
    g = jnp.dot(_sigmoid(gl).reshape(nb * L, RW_G_LORA).astype(BF16), gup_ref[...],
                preferred_element_type=F32).reshape(nb, L, D)
    kk = kraw * kk_ref[...]
    k = kraw * (1.0 + (a - 1.0) * ka_ref[...])
    sums = head_sum(jnp.concatenate([kk * kk, r * k * rk_ref[...]], axis=1), 1)
    kk = kk * lax.rsqrt(jnp.maximum(sums[:, :L], 1e-24))
    bonus = sums[:, L:]

    cl = _cumsum_rows(logw)
    cl_last = cl[:, L - 1:L, :]
    e_neg = jnp.exp(-cl)
    e_end = jnp.exp(cl_last - cl)
    kka = kk * a
    narrow = (lambda x: x.astype(BF16)) if L % 16 == 0 else (lambda x: x)
    operands = [halves(narrow(o)) for o in (-kk * jnp.exp(cl - logw), kka * e_neg, k * e_neg, r * jnp.exp(cl),
                                            kka * e_end, k * e_end, v)]
    y_g, s_new = _rwkv_chunk(L, *operands, halves(jnp.exp(cl_last)), s_scr[...], bd32)
    s_scr[...] = s_new
    y_g = y_g.reshape(nb, 2, L, DH)
    y = jnp.concatenate([y_g[:, 0], y_g[:, 1]], axis=2)

    stats = head_sum(jnp.concatenate([y, y * y], axis=1), 2) * (1.0 / RW_HD)
    mean = stats[:, :L]
    var = jnp.maximum(stats[:, L:] - mean * mean, 0.0)
    yn = (y - mean) * lax.rsqrt(var + RW_GN_EPS) * gng_ref[...] + gnb_ref[...]
    y_ref[...] = (yn + bonus * v) * g

    @pl.when(c == nc - 1)
    def _():
        compact = sum(s_new[:, :, h * RW_HD:(h + 1) * RW_HD] for h in range(RW_HALF_HEADS))
        sout_ref[...] = compact.reshape(nb, D, RW_HD)
        shout_ref[...] = z[:, L - 1:L, :]


def _rwkv(z, state, params, L, nb):
    B, T, _ = z.shape
    nc = T // L
    per_b = lambda shp: pl.BlockSpec((nb,) + shp, lambda b, c: (b, 0, 0))
    bd = (_iota((RW_HALF, RW_HALF), 0) >> 6) == (_iota((RW_HALF, RW_HALF), 1) >> 6)
    consts = tuple(params) + (bd.astype(BF16), bd.astype(F32))
    state_args, state_specs = (), []
    if state is not None:
        layer, shift0, s0 = state
        of_layer = lambda shp: pl.BlockSpec((None, nb) + shp, lambda b, c: (layer, b, 0, 0))
        state_args, state_specs = (shift0, s0), [of_layer((1, RW_SHIFT_W)), of_layer((D_RW, RW_HD))]
    return pl.pallas_call(
        functools.partial(_rwkv_body, L, nc, nb, state is not None),
        out_shape=(jax.ShapeDtypeStruct((B, T, D_RW), F32),
                   jax.ShapeDtypeStruct((B, D_RW, RW_HD), F32),
                   jax.ShapeDtypeStruct((B, 1, RW_SHIFT_W), F32)),
        grid=(B // nb, nc),
        in_specs=[pl.BlockSpec((nb, L, RW_SHIFT_W), lambda b, c: (b, c, 0))] + state_specs
                 + [_const_spec(p.shape) for p in consts],
        out_specs=(pl.BlockSpec((nb, L, D_RW), lambda b, c: (b, c, 0)),
                   per_b((D_RW, RW_HD)), per_b((1, RW_SHIFT_W))),
        scratch_shapes=[pltpu.VMEM((2 * nb, RW_HALF, RW_HALF), F32), pltpu.VMEM((nb, 1, RW_SHIFT_W), F32)],
        compiler_params=pltpu.CompilerParams(dimension_semantics=("arbitrary", "arbitrary"),
                                             vmem_limit_bytes=VMEM_LIMIT),
        name="rwkv",
    )(z, *state_args, *consts)


def _mlstm_body(L, nc, nb, has_state, zm_ref, zif_ref, *refs):
    if has_state:
        cv0_ref, c0_ref, nm0_ref, *refs = refs
    cw_ref, cb_ref, ifb_ref, gn_ref, y_ref, cout_ref, nmout_ref, cvout_ref, c_scr, nm_scr, cv_scr = refs
    c = pl.program_id(1)
    H, HD, D = ML_HEADS, ML_HD, D_ML
    G = nb * H

    @pl.when(c == 0)
    def _():
        if has_state:
            cv_scr[...] = cv0_ref[...]
            c_scr[...] = c0_ref[...]
            nm_scr[...] = nm0_ref[...]
        else:
            cv_scr[...] = jnp.zeros(cv_scr.shape, F32)
            c_scr[...] = jnp.zeros(c_scr.shape, F32)
            nm_scr[...] = jnp.zeros(nm_scr.shape, F32)

    heads = lambda x: jnp.stack([x[:, :, h * HD:(h + 1) * HD] for h in range(H)], axis=1).reshape(G, x.shape[1], HD)
    lane_pick = lambda x, lo: jnp.stack([x[:, :, lo + h:lo + h + 1] for h in range(H)], axis=1).reshape(G, x.shape[1], 1)
    row_pick = lambda x, lo: jnp.stack([x[:, lo + h:lo + h + 1, :] for h in range(H)], axis=1).reshape(G, 1, x.shape[2])

    zm = zm_ref[...]
    raw = zm[:, :, :2 * D]
    v = zm[:, :, 2 * D:3 * D]
    o = zm[:, :, 3 * D:]
    ext = jnp.concatenate([cv_scr[...], raw], axis=1)
    cv_scr[...] = ext[:, L:, :]
    cw = cw_ref[...]
    qk = cb_ref[...] + raw * cw[CONV_W - 1:CONV_W, :]
    for s in range(1, CONV_W):
        qk = qk + pltpu.roll(ext, s, 1)[:, SUBLANES:, :] * cw[CONV_W - 1 - s:CONV_W - s, :]
    qk = qk * _sigmoid(qk)
    qh = heads(qk[:, :, :D])
    kh = heads(qk[:, :, D:] * (HD ** -0.5))
    vh = heads(v)

    gi = zif_ref[...] + ifb_ref[...]
    lane = _iota(gi.shape, 2)
    lf = jnp.minimum(gi, 0.0) - jnp.log(1.0 + jnp.exp(-jnp.abs(gi)))
    gcol = jnp.where(lane < H, gi, jnp.where(lane < 2 * H, lf, 0.0))
    causal = _iota((1, L, L), 1) >= _iota((1, L, L), 2)
    b_col = _cumsum_rows(gcol)
    grow = jnp.swapaxes(gcol, 1, 2)
    b_row = jnp.swapaxes(b_col, 1, 2)
    bc, ic = lane_pick(b_col, H), lane_pick(gcol, 0)
    br, ir = row_pick(b_row, H), row_pick(grow, 0)

    nm = nm_scr[...]
    ch = c_scr[...].reshape(G, HD, HD)
    nh = row_pick(nm, 0)
    m_prev = lane_pick(nm[:, H:H + 1, :], 0)

    dlog = jnp.where(causal, bc - br + ir, -jnp.inf)
    inter = bc + m_prev
    m_t = jnp.maximum(inter, jnp.max(dlog, axis=-1, keepdims=True))
    amat = jnp.exp(dlog - m_t)
    sc = jnp.exp(inter - m_t)
    aqk = amat * _bmm_nt(qh, kh)
    num = _bmm_nn(aqk, vh) + sc * _bmm_nt(qh, ch)
    den = jnp.sum(aqk, axis=-1, keepdims=True) + sc * jnp.sum(qh * nh, axis=-1, keepdims=True)
    hh = num / jnp.maximum(jnp.abs(den), jnp.exp(-m_t))
    m_new = m_t[:, L - 1:L, :]
    b_last = bc[:, L - 1:L, :]
    wc = jnp.exp(b_last - bc + ic - m_new)
    dec = jnp.exp(b_last + m_prev - m_new)
    c_scr[...] = (dec * ch + _bmm_tn(vh * wc, kh)).reshape(nb, D, HD)
    n_new = (dec * nh + jnp.sum(kh * wc, axis=1, keepdims=True)).reshape(nb, H, 1, HD)
    m_new = m_new.reshape(nb, H, 1, 1)
    m_row = jnp.zeros((nb, 1, HD), F32)
    for h in range(H):
        m_row = jnp.where(_iota(m_row.shape, 2) == h, m_new[:, h], m_row)
    nm_scr[...] = jnp.concatenate([n_new[:, h] for h in range(H)] + [m_row, nm[:, H + 1:, :]], axis=1)

    mu = jnp.mean(hh, axis=-1, keepdims=True)
    xc = hh - mu
    var = jnp.mean(xc * xc, axis=-1, keepdims=True)
    gn = jnp.concatenate([gn_ref[:, h * HD:(h + 1) * HD][None] for h in range(H)] * nb, axis=0)
    out = (xc * lax.rsqrt(var + ML_GN_EPS) * gn * _sigmoid(heads(o))).reshape(nb, H, L, HD)
    y_ref[...] = jnp.concatenate([out[:, h] for h in range(H)], axis=2)

    @pl.when(c == nc - 1)
    def _():
        cout_ref[...] = c_scr[...]
        nmout_ref[...] = nm_scr[...]
        cvout_ref[...] = cv_scr[...]


def _mlstm(zm, zif, state, params, L, nb):
    B, T, _ = zm.shape
    nc = T // L
    per_b = lambda shp: pl.BlockSpec((nb,) + shp, lambda b, c: (b, 0, 0))
    chunk = lambda w: pl.BlockSpec((nb, L, w), lambda b, c: (b, c, 0))
    state_shapes = ((SUBLANES, 2 * D_ML), (D_ML, ML_HD), (SUBLANES, ML_HD))
    state_args, state_specs = (), []
    if state is not None:
        layer, cv0, c0, nm0 = state
        state_args = (cv0, c0, nm0)
        state_specs = [per_b(state_shapes[0]),
                       pl.BlockSpec((None, nb) + state_shapes[1], lambda b, c: (layer, b, 0, 0)),
                       per_b(state_shapes[2])]
    return pl.pallas_call(
        functools.partial(_mlstm_body, L, nc, nb, state is not None),
        out_shape=(jax.ShapeDtypeStruct((B, T, D_ML), F32),
                   jax.ShapeDtypeStruct((B,) + state_shapes[1], F32),
                   jax.ShapeDtypeStruct((B,) + state_shapes[2], F32),
                   jax.ShapeDtypeStruct((B,) + state_shapes[0], F32)),
        grid=(B // nb, nc),
        in_specs=[chunk(ML_MAIN_W), chunk(GATE_W)] + state_specs + [_const_spec(p.shape) for p in params],
        out_specs=(chunk(D_ML), per_b(state_shapes[1]), per_b(state_shapes[2]), per_b(state_shapes[0])),
        scratch_shapes=[pltpu.VMEM((nb,) + state_shapes[1], F32), pltpu.VMEM((nb,) + state_shapes[2], F32),
                        pltpu.VMEM((nb,) + state_shapes[0], F32)],
        compiler_params=pltpu.CompilerParams(dimension_semantics=("arbitrary", "arbitrary"),
                                             vmem_limit_bytes=VMEM_LIMIT),
        name="mlstm",
    )(zm, zif, *state_args, *params)


def _merge_body(x_ref, ya_ref, yb_ref, pre_ref, post_ref, wg_ref, pa_ref, pb_ref, wo_ref, h_ref):
    x = x_ref[0]
    u = _rms(x, pre_ref[...]).astype(BF16)
    gate = _sigmoid(jnp.dot(u, wg_ref[...], preferred_element_type=F32))
    pa = jnp.dot(ya_ref[0].astype(BF16), pa_ref[...], preferred_element_type=F32)
    pb = jnp.dot(yb_ref[0].astype(BF16), pb_ref[...], preferred_element_type=F32)
    merged = gate[:, :D_MODEL] * pa + gate[:, D_MODEL:] * pb
    o = jnp.dot(merged.astype(BF16), wo_ref[...], preferred_element_type=F32)
    h_ref[0] = x + _rms(o, post_ref[...])


def _merge(x, ya, yb, pre, post, wg, pa, pb, wo, tm):
    B, T, _ = x.shape
    row = lambda w: pl.BlockSpec((1, tm, w), lambda b, i: (b, i, 0))
    consts = (pre, post, wg, pa, pb, wo)
    return pl.pallas_call(
        _merge_body,
        out_shape=jax.ShapeDtypeStruct((B, T, D_MODEL), F32),
        grid=(B, T // tm),
        in_specs=[row(D_MODEL), row(D_RW), row(D_ML)] + [_const_spec(p.shape) for p in consts],
        out_specs=row(D_MODEL),
        compiler_params=pltpu.CompilerParams(dimension_semantics=("arbitrary", "arbitrary"),
                                             vmem_limit_bytes=VMEM_LIMIT),
        name="merge",
    )(x, ya, yb, *consts)


FF_SPLIT = 4


def _ffn_body(h_ref, pre_ref, post_ref, wu_ref, wd_ref, o_ref):
    h = h_ref[0]
    u = _rms(h, pre_ref[...]).astype(BF16)
    step = D_FF // FF_SPLIT
    f = None
    for j in range(FF_SPLIT):
        t = jnp.maximum(jnp.dot(u, wu_ref[:, j * step:(j + 1) * step], preferred_element_type=F32), 0.0)
        part = jnp.dot((t * t).astype(BF16), wd_ref[j * step:(j + 1) * step, :], preferred_element_type=F32)
        f = part if f is None else f + part
    o_ref[0] = h + _rms(f, post_ref[...])


def _ffn(h, pre, post, wu, wd, tm):
    B, T, _ = h.shape
    row = pl.BlockSpec((1, tm, D_MODEL), lambda b, i: (b, i, 0))
    consts = (pre, post, wu, wd)
    return pl.pallas_call(
        _ffn_body,
        out_shape=jax.ShapeDtypeStruct((B, T, D_MODEL), F32),
        grid=(B, T // tm),
        in_specs=[row] + [_const_spec(p.shape) for p in consts],
        out_specs=row,
        compiler_params=pltpu.CompilerParams(dimension_semantics=("arbitrary", "arbitrary"),
                                             vmem_limit_bytes=VMEM_LIMIT),
        name="ffn",
    )(h, *consts)


MAX_ROW_TILE = 1024


def _row_tile(T):
    return max(tm for tm in range(SUBLANES, min(MAX_ROW_TILE, max(T // 2, SUBLANES)) + 1, SUBLANES) if T % tm == 0)


def _layer(x, st, lp, L):
    B, T, _ = x.shape
    long_seq = T > L
    flat = lambda a: a.reshape(1, B * T, a.shape[-1])
    unflat = lambda a: a.reshape(B, T, a.shape[-1])
    xf = flat(x)
    tm = _row_tile(B * T)
    z_rw, z_ml, z_if = map(unflat, _in_proj(xf, lp['pre1'], lp['w_rw'], lp['w_ml'], lp['w_if'], tm))

    rw_state = ml_state = None
    if st is not None:
        l, S, sh, C, n, m, cb = st
        depth = S.shape[0]
        rw_state = (l, sh.reshape(depth, B, 1, RW_SHIFT_W), S.reshape(depth, B, D_RW, RW_HD))
        cv0 = jnp.pad(cb[l], ((0, 0), (SUBLANES - (CONV_W - 1), 0), (0, 0)))
        nm0 = jnp.concatenate([n[l], jnp.pad(m[l], ((0, 0), (0, ML_HD - ML_HEADS)))[:, None, :],
                               jnp.zeros((B, SUBLANES - ML_HEADS - 1, ML_HD), F32)], axis=1)
        ml_state = (l, cv0, C.reshape(depth, B, D_ML, ML_HD), nm0)
    ya, S1, sh1 = _rwkv(z_rw, rw_state, lp['rw_params'], L, _group(B, RW_SEQS_LONG if long_seq else RW_SEQS_SHORT))
    yb, C1, nm1, cv1 = _mlstm(z_ml, z_if, ml_state, lp['ml_params'], L,
                              _group(B, ML_SEQS_LONG if long_seq else ML_SEQS_SHORT))

    h = _merge(xf, flat(ya), flat(yb), lp['pre1'], lp['post1'], lp['w_gate'], lp['p_a'], lp['p_b'], lp['w_out'], tm)
    out = unflat(_ffn(h, lp['pre2'], lp['post2'], lp['w_ff_up'], lp['w_ff_down'], tm))
    new_state = (S1.reshape(B, RW_HEADS, RW_HD, RW_HD), sh1[:, 0, :], C1.reshape(B, ML_HEADS, ML_HD, ML_HD),
                 nm1[:, :ML_HEADS, :], nm1[:, ML_HEADS, :ML_HEADS], cv1[:, SUBLANES - (CONV_W - 1):, :])
    return out, new_state


def _layer_params(l, w_in, rw_mu, rw_w0, rw_w_up, rw_a0, rw_a_up, rw_g_up, rw_k_k, rw_k_a, rw_r_k, rw_gn_g,
                  rw_gn_b, ml_conv_w, ml_conv_b, ml_i_bias, ml_f_bias, ml_gn_g, p_a, p_b, w_out, pre1, post1,
                  pre2, post2, w_ff_up, w_ff_down):
    row = lambda a: a[l].reshape(1, -1).astype(F32)
    w = w_in[l]
    c_ml = RW_SHIFT_W
    c_if = c_ml + ML_MAIN_W
    c_gate = c_if + 2 * ML_HEADS
    half = RW_LORA_W // 2
    lora = jnp.zeros((RW_LORA_W, 2 * D_RW), F32)
    lora = lora.at[:half, :D_RW].set(rw_w_up[l]).at[half:, D_RW:].set(rw_a_up[l])
    lora_hi = lora.astype(BF16)
    lora_lo = (lora - lora_hi.astype(F32)).astype(BF16)
    lora = jnp.concatenate([lora_hi, lora_hi, lora_lo], axis=0)
    if_bias = jnp.zeros((1, GATE_W), F32)
    if_bias = if_bias.at[0, :ML_HEADS].set(ml_i_bias[l]).at[0, ML_HEADS:2 * ML_HEADS].set(ml_f_bias[l])
    return dict(
        pre1=row(pre1), post1=row(post1), pre2=row(pre2), post2=row(post2),
        w_rw=w[:, :c_ml].astype(BF16),
        w_ml=w[:, c_ml:c_if].astype(BF16),
        w_if=jnp.pad(w[:, c_if:c_gate], ((0, 0), (0, GATE_W - 2 * ML_HEADS))).astype(BF16),
        w_gate=w[:, c_gate:].astype(BF16),
        rw_params=(row(rw_mu), row(rw_w0), row(rw_a0), lora, rw_g_up[l].astype(BF16), row(rw_k_k), row(rw_k_a),
                   row(rw_r_k), row(rw_gn_g), row(rw_gn_b)),
        ml_params=(ml_conv_w[l].astype(F32), row(ml_conv_b), if_bias, row(ml_gn_g)),
        p_a=p_a[l].astype(BF16), p_b=p_b[l].astype(BF16), w_out=w_out[l].astype(BF16),
        w_ff_up=w_ff_up[l].astype(BF16), w_ff_down=w_ff_down[l].astype(BF16),
    )


def kernel(x_prompt, x_sample, state_rwkv_S, state_rwkv_shift, state_mlstm_C, state_mlstm_n, state_mlstm_m,
           state_mlstm_conv, meta_tokens, w_in, rw_mu, rw_w0, rw_w_up, rw_a0, rw_a_up, rw_g_up, rw_k_k, rw_k_a,
           rw_r_k, rw_gn_g, rw_gn_b, ml_conv_w, ml_conv_b, ml_i_bias, ml_f_bias, ml_gn_g, p_a, p_b, w_out, pre1,
           post1, pre2, post2, w_ff_up, w_ff_down):
    B, T, _ = x_prompt.shape
    dt = x_prompt.dtype
    depth = w_in.shape[0]
    xm = meta_tokens[None].astype(dt)
    xp = x_prompt
    xs = x_sample
    p_states, s_states = [], []
    for l in range(depth):
        lp = _layer_params(l, w_in, rw_mu, rw_w0, rw_w_up, rw_a0, rw_a_up, rw_g_up, rw_k_k, rw_k_a, rw_r_k,
                           rw_gn_g, rw_gn_b, ml_conv_w, ml_conv_b, ml_i_bias, ml_f_bias, ml_gn_g, p_a, p_b, w_out,
                           pre1, post1, pre2, post2, w_ff_up, w_ff_down)
        xm, st_m = _layer(xm, None, lp, N_META)
        after_meta = tuple(jnp.broadcast_to(a, (1, B) + a.shape[1:]) for a in st_m)
        xp, st_p = _layer(xp, (0,) + after_meta, lp, PROMPT_CHUNK)
        st_in = (l, state_rwkv_S, state_rwkv_shift, state_mlstm_C, state_mlstm_n, state_mlstm_m, state_mlstm_conv)
        xs, st_s = _layer(xs, st_in, lp, xs.shape[1])
        p_states.append(st_p)
        s_states.append(st_s)
    stk = lambda lst, i: jnp.stack([s[i] for s in lst]).astype(dt)
    return (xp, xs,
            stk(p_states, 0), stk(p_states, 1), stk(p_states, 2), stk(p_states, 3), stk(p_states, 4), stk(p_states, 5),
            stk(s_states, 0), stk(s_states, 1), stk(s_states, 2), stk(s_states, 3), stk(s_states, 4), stk(s_states, 5))
```

```python
import functools
import math

import jax
import jax.numpy as jnp
from jax import lax
from jax.experimental import pallas as pl
from jax.experimental.pallas import tpu as pltpu

F32 = jnp.float32
BF16 = jnp.bfloat16

D_MODEL = 1024
N_META = 16
RW_HEADS = 8
RW_HD = 64
D_RW = RW_HEADS * RW_HD
RW_HALF_HEADS = RW_HEADS // 2
RW_HALF = RW_HALF_HEADS * RW_HD
RW_LORA_W = 128
RW_G_LORA = 128
RW_SHIFT_W = 3 * D_RW + RW_LORA_W + RW_G_LORA
RW_GN_EPS = 64e-5
ML_HEADS = 4
ML_HD = 128
D_ML = ML_HEADS * ML_HD
CONV_W = 4
ML_MAIN_W = 4 * D_ML
ML_GN_EPS = 1e-5
GATE_W = 128
D_FF = 4 * D_MODEL
RMS_EPS = 1e-6
SUBLANES = 8

PROMPT_CHUNK = 64
VMEM_LIMIT = 56 * 2**20
RW_SEQS_LONG, RW_SEQS_SHORT = 8, 16
ML_SEQS_LONG, ML_SEQS_SHORT = 8, 16


def _bmm_nn(a, b):
    return jnp.einsum('gmk,gkn->gmn', a.astype(BF16), b.astype(BF16), preferred_element_type=F32)


def _bmm_nt(a, b):
    return jnp.einsum('gmk,gnk->gmn', a.astype(BF16), b.astype(BF16), preferred_element_type=F32)


def _bmm_tn(a, b):
    return jnp.einsum('gkm,gkn->gmn', a.astype(BF16), b.astype(BF16), preferred_element_type=F32)


def _neumann_half(n):
    L = n.shape[-1]
    t = n + (_iota((1, L, L), 1) == _iota((1, L, L), 2)).astype(F32)
    p = n
    for _ in range(max(L.bit_length() - 3, 0)):
        p = _bmm_nn(p, p)
        t = t + _bmm_nn(t, p)
    return t


def _split(x, terms):
    parts = []
    for _ in range(terms - 1):
        p = x.astype(BF16)
        parts.append(p)
        x = x - p.astype(F32)
    parts.append(x.astype(BF16))
    return parts


def _sel_right(x, sel, terms):
    pieces = _split(x, terms)
    return jnp.dot(jnp.concatenate(pieces, axis=1), jnp.concatenate([sel] * terms, axis=0),
                   preferred_element_type=F32)


def _cumsum_rows(x):
    row = _iota(x.shape, 1)
    s = 1
    while s < x.shape[1]:
        x = x + jnp.where(row >= s, pltpu.roll(x, s, 1), 0.0)
        s *= 2
    return x


def _sigmoid(x):
    return 0.5 * jnp.tanh(0.5 * x) + 0.5


def _iota(shape, dim):
    return lax.broadcasted_iota(jnp.int32, shape, dim)


def _rms(x, g):
    return x * lax.rsqrt(jnp.mean(x * x, axis=-1, keepdims=True) + RMS_EPS) * g


def _const_spec(shape):
    nd = len(shape)
    return pl.BlockSpec(shape, lambda *_: (0,) * nd, pipeline_mode=pl.Buffered(1))


def _group(n, target):
    return max(d for d in range(1, target + 1) if n % d == 0)


def _in_proj_body(x_ref, g_ref, wrw_ref, wml_ref, wif_ref, zrw_ref, zml_ref, zif_ref):
    u = _rms(x_ref[0], g_ref[...]).astype(BF16)
    zrw_ref[0] = jnp.dot(u, wrw_ref[...], preferred_element_type=F32)
    zml_ref[0] = jnp.dot(u, wml_ref[...], preferred_element_type=F32)
    zif_ref[0] = jnp.dot(u, wif_ref[...], preferred_element_type=F32)


def _in_proj(x, g, w_rw, w_ml, w_if, tm):
    B, T, _ = x.shape
    row = lambda w: pl.BlockSpec((1, tm, w), lambda b, i: (b, i, 0))
    return pl.pallas_call(
        _in_proj_body,
        out_shape=(jax.ShapeDtypeStruct((B, T, RW_SHIFT_W), F32),
                   jax.ShapeDtypeStruct((B, T, ML_MAIN_W), F32),
                   jax.ShapeDtypeStruct((B, T, GATE_W), F32)),
        grid=(B, T // tm),
        in_specs=[row(D_MODEL), _const_spec((1, D_MODEL)), _const_spec(w_rw.shape), _const_spec(w_ml.shape),
                  _const_spec(w_if.shape)],
        out_specs=(row(RW_SHIFT_W), row(ML_MAIN_W), row(GATE_W)),
        compiler_params=pltpu.CompilerParams(dimension_semantics=("arbitrary", "arbitrary"),
                                             vmem_limit_bytes=VMEM_LIMIT),
        name="in_proj",
    )(x, g, w_rw, w_ml, w_if)


def _rwkv_chunk(L, at, bt, kt, rt, b_end, k_end, v, decay_end, S, bd32):
    H = RW_HALF_HEADS
    g = at.shape[0]
    lane_head = _iota((1, 1, RW_HALF), 2) >> 6
    hmask = [(lane_head == h).astype(F32) for h in range(H)]
    if L % 16 == 0:
        hmask16 = [m.astype(BF16) for m in hmask]
        stack = lambda x: jnp.concatenate([x.astype(BF16) * hmask16[h] for h in range(H)], axis=1)
    else:
        stack = lambda x: jnp.concatenate([x * hmask[h] for h in range(H)], axis=1).astype(BF16)

    n = _bmm_nt(stack(at), bt).reshape(g * H, L, L)
    n = jnp.where(_iota((1, L, L), 1) > _iota((1, L, L), 2), n, 0.0)
    tinv = _neumann_half(n).reshape(g, H, L, L)

    v_stack = stack(v)
    ar = jnp.concatenate([at, rt], axis=1).astype(BF16)
    att = _bmm_nt(ar, jnp.concatenate([stack(bt), stack(kt)], axis=1))
    s_col = _iota((1, L, H * L), 2) & (L - 1)
    strict = s_col < _iota((1, L, H * L), 1)
    a_ab = jnp.where(strict, att[:, :L, :H * L], 0.0)
    a_ak = jnp.where(strict, att[:, :L, H * L:], 0.0)
    incl = (_iota((1, L, 2 * H * L), 2) & (L - 1)) <= _iota((1, L, 2 * H * L), 1)
    a_r = jnp.where(incl, att[:, L:, :], 0.0)

    from_state = _bmm_nt(ar, S)
    wmat = from_state[:, :L] + _bmm_nn(a_ak, v_stack)

    def solve(rhs):
        rhs = rhs.astype(BF16)
        x = hmask[0] * _bmm_nn(tinv[:, 0], rhs)
        for h in range(1, H):
            x = x + hmask[h] * _bmm_nn(tinv[:, h], rhs)
        return x

    u = solve(wmat)
    ab_hi, ab_lo = _split(a_ab, 2)
    u_hi, u_lo = _split(u, 2)
    u_hi_stack = stack(u_hi)
    nu = (jnp.einsum('gmk,gkn->gmn', jnp.concatenate([ab_hi, ab_lo], axis=2),
                     jnp.concatenate([u_hi_stack, u_hi_stack], axis=1), preferred_element_type=F32)
          + jnp.einsum('gmk,gkn->gmn', ab_hi, stack(u_lo), preferred_element_type=F32))
    u = u + solve(wmat - u + nu)
    y = from_state[:, L:] + _bmm_nn(a_r, jnp.concatenate([stack(u), v_stack], axis=1))
    upd = _bmm_tn(jnp.concatenate([u.astype(v.dtype), v], axis=1), jnp.concatenate([b_end, k_end], axis=1))
    return y, S * decay_end + upd * bd32


def _rwkv_body(L, nc, nb, has_state, z_ref, *refs):
    if has_state:
        sh0_ref, s0_ref, *refs = refs
    (mu_ref, w0_ref, a0_ref, wc_ref, gup_ref, kk_ref, ka_ref, rk_ref, gng_ref, gnb_ref, bd16_ref, bd32_ref,
     y_ref, sout_ref, shout_ref, s_scr, prev_scr) = refs
    c = pl.program_id(1)
    D, DH = D_RW, RW_HALF
    bd16 = bd16_ref[...]
    bd32 = bd32_ref[...]

    def head_sum(x, terms):
        m = x.shape[1]
        x2 = x.reshape(nb * m, D)
        r = jnp.concatenate([_sel_right(x2[:, hf * DH:(hf + 1) * DH], bd16, terms) for hf in range(2)], axis=1)
        return r.reshape(nb, m, D)

    halves = lambda x: jnp.stack([x[:, :, :DH], x[:, :, DH:]], axis=1).reshape(2 * nb, x.shape[1], DH)

    @pl.when(c == 0)
    def _():
        if has_state:
            prev_scr[...] = sh0_ref[...]
            s0 = s0_ref[...].reshape(2 * nb, DH, RW_HD)
            s_scr[...] = jnp.concatenate([s0] * RW_HALF_HEADS, axis=2) * bd32
        else:
            prev_scr[...] = jnp.zeros(prev_scr.shape, F32)
            s_scr[...] = jnp.zeros(s_scr.shape, F32)

    z = z_ref[...]
    prev = jnp.where(_iota(z.shape, 1) == 0, prev_scr[...], pltpu.roll(z, 1, 1))
    prev_scr[...] = z[:, L - 1:L, :]
    zs = z + (prev - z) * mu_ref[...]
    r = zs[:, :, 0:D]
    kraw = zs[:, :, D:2 * D]
    v = zs[:, :, 2 * D:3 * D]
    xl = zs[:, :, 3 * D:3 * D + RW_LORA_W]
    gl = zs[:, :, 3 * D + RW_LORA_W:]
    xl = jnp.where(_iota(xl.shape, 2) < RW_LORA_W // 2, jnp.tanh(xl), xl)
    xl_hi, xl_lo = _split(xl, 2)
    lora = jnp.dot(jnp.concatenate([xl_hi, xl_lo, xl_hi], axis=2).reshape(nb * L, 3 * RW_LORA_W), wc_ref[...],
                   preferred_element_type=F32).reshape(nb, L, 2 * D)
    logw = -math.exp(-0.5) * _sigmoid(w0_ref[...] + lora[:, :, :D])
    a = _sigmoid(a0_ref[...] + lora[:, :, D:])
    g = jnp.dot(_sigmoid(gl).reshape(nb * L, RW_G_LORA).astype(BF16), gup_ref[...],
                preferred_element_type=F32).reshape(nb, L, D)
    kk = kraw * kk_ref[...]
    k = kraw * (1.0 + (a - 1.0) * ka_ref[...])
    sums = head_sum(jnp.concatenate([kk * kk, r * k * rk_ref[...]], axis=1), 1)
    kk = kk * lax.rsqrt(jnp.maximum(sums[:, :L], 1e-24))
    bonus = sums[:, L:]

    cl = _cumsum_rows(logw)
    cl_last = cl[:, L - 1:L, :]
    e_neg = jnp.exp(-cl)
    e_end = jnp.exp(cl_last - cl)
    kka = kk * a
    narrow = (lambda x: x.astype(BF16)) if L % 16 == 0 else (lambda x: x)
    operands = [halves(narrow(o)) for o in (-kk * jnp.exp(cl - logw), kka * e_neg, k * e_neg, r * jnp.exp(cl),
                                            kka * e_end, k * e_end, v)]
    y_g, s_new = _rwkv_chunk(L, *operands, halves(jnp.exp(cl_last)), s_scr[...], bd32)
    s_scr[...] = s_new
    y_g = y_g.reshape(nb, 2, L, DH)
    y = jnp.concatenate([y_g[:, 0], y_g[:, 1]], axis=2)

    stats = head_sum(jnp.concatenate([y, y * y], axis=1), 2) * (1.0 / RW_HD)
    mean = stats[:, :L]
    var = jnp.maximum(stats[:, L:] - mean * mean, 0.0)
    yn = (y - mean) * lax.rsqrt(var + RW_GN_EPS) * gng_ref[...] + gnb_ref[...]
    y_ref[...] = (yn + bonus * v) * g

    @pl.when(c == nc - 1)
    def _():
        compact = sum(s_new[:, :, h * RW_HD:(h + 1) * RW_HD] for h in range(RW_HALF_HEADS))
        sout_ref[...] = compact.reshape(nb, D, RW_HD)
        shout_ref[...] = z[:, L - 1:L, :]


def _rwkv(z, state, params, L, nb):
    B, T, _ = z.shape
    nc = T // L
    per_b = lambda shp: pl.BlockSpec((nb,) + shp, lambda b, c: (b, 0, 0))
    bd = (_iota((RW_HALF, RW_HALF), 0) >> 6) == (_iota((RW_HALF, RW_HALF), 1) >> 6)
    consts = tuple(params) + (bd.astype(BF16), bd.astype(F32))
    state_args, state_specs = (), []
    if state is not None:
        layer, shift0, s0 = state
        of_layer = lambda shp: pl.BlockSpec((None, nb) + shp, lambda b, c: (layer, b, 0, 0))
        state_args, state_specs = (shift0, s0), [of_layer((1, RW_SHIFT_W)), of_layer((D_RW, RW_HD))]
    return pl.pallas_call(
        functools.partial(_rwkv_body, L, nc, nb, state is not None),
        out_shape=(jax.ShapeDtypeStruct((B, T, D_RW), F32),
                   jax.ShapeDtypeStruct((B, D_RW, RW_HD), F32),
                   jax.ShapeDtypeStruct((B, 1, RW_SHIFT_W), F32)),
        grid=(B // nb, nc),
        in_specs=[pl.BlockSpec((nb, L, RW_SHIFT_W), lambda b, c: (b, c, 0))] + state_specs
                 + [_const_spec(p.shape) for p in consts],
        out_specs=(pl.BlockSpec((nb, L, D_RW), lambda b, c: (b, c, 0)),
                   per_b((D_RW, RW_HD)), per_b((1, RW_SHIFT_W))),
        scratch_shapes=[pltpu.VMEM((2 * nb, RW_HALF, RW_HALF), F32), pltpu.VMEM((nb, 1, RW_SHIFT_W), F32)],
        compiler_params=pltpu.CompilerParams(dimension_semantics=("arbitrary", "arbitrary"),
                                             vmem_limit_bytes=VMEM_LIMIT),
        name="rwkv",
    )(z, *state_args, *consts)


def _mlstm_body(L, nc, nb, has_state, zm_ref, zif_ref, *refs):
    if has_state:
        cv0_ref, c0_ref, nm0_ref, *refs = refs
    cw_ref, cb_ref, ifb_ref, gn_ref, y_ref, cout_ref, nmout_ref, cvout_ref, c_scr, nm_scr, cv_scr = refs
    c = pl.program_id(1)
    H, HD, D = ML_HEADS, ML_HD, D_ML
    G = nb * H

    @pl.when(c == 0)
    def _():
        if has_state:
            cv_scr[...] = cv0_ref[...]
            c_scr[...] = c0_ref[...]
            nm_scr[...] = nm0_ref[...]
        else:
            cv_scr[...] = jnp.zeros(cv_scr.shape, F32)
            c_scr[...] = jnp.zeros(c_scr.shape, F32)
            nm_scr[...] = jnp.zeros(nm_scr.shape, F32)

    heads = lambda x: jnp.stack([x[:, :, h * HD:(h + 1) * HD] for h in range(H)], axis=1).reshape(G, x.shape[1], HD)
    lane_pick = lambda x, lo: jnp.stack([x[:, :, lo + h:lo + h + 1] for h in range(H)], axis=1).reshape(G, x.shape[1], 1)
    row_pick = lambda x, lo: jnp.stack([x[:, lo + h:lo + h + 1, :] for h in range(H)], axis=1).reshape(G, 1, x.shape[2])

    zm = zm_ref[...]
    raw = zm[:, :, :2 * D]
    v = zm[:, :, 2 * D:3 * D]
    o = zm[:, :, 3 * D:]
    ext = jnp.concatenate([cv_scr[...], raw], axis=1)
    cv_scr[...] = ext[:, L:, :]
    cw = cw_ref[...]
    qk = cb_ref[...] + raw * cw[CONV_W - 1:CONV_W, :]
    for s in range(1, CONV_W):
        qk = qk + pltpu.roll(ext, s, 1)[:, SUBLANES:, :] * cw[CONV_W - 1 - s:CONV_W - s, :]
    qk = qk * _sigmoid(qk)
    qh = heads(qk[:, :, :D])
    kh = heads(qk[:, :, D:] * (HD ** -0.5))
    vh = heads(v)

    gi = zif_ref[...] + ifb_ref[...]
    lane = _iota(gi.shape, 2)
    lf = jnp.minimum(gi, 0.0) - jnp.log(1.0 + jnp.exp(-jnp.abs(gi)))
    gcol = jnp.where(lane < H, gi, jnp.where(lane < 2 * H, lf, 0.0))
    causal = _iota((1, L, L), 1) >= _iota((1, L, L), 2)
    b_col = _cumsum_rows(gcol)
    grow = jnp.swapaxes(gcol, 1, 2)
    b_row = jnp.swapaxes(b_col, 1, 2)
    bc, ic = lane_pick(b_col, H), lane_pick(gcol, 0)
    br, ir = row_pick(b_row, H), row_pick(grow, 0)

    nm = nm_scr[...]
    ch = c_scr[...].reshape(G, HD, HD)
    nh = row_pick(nm, 0)
    m_prev = lane_pick(nm[:, H:H + 1, :], 0)

    dlog = jnp.where(causal, bc - br + ir, -jnp.inf)
    inter = bc + m_prev
    m_t = jnp.maximum(inter, jnp.max(dlog, axis=-1, keepdims=True))
    amat = jnp.exp(dlog - m_t)
    sc = jnp.exp(inter - m_t)
    aqk = amat * _bmm_nt(qh, kh)
    num = _bmm_nn(aqk, vh) + sc * _bmm_nt(qh, ch)
    den = jnp.sum(aqk, axis=-1, keepdims=True) + sc * jnp.sum(qh * nh, axis=-1, keepdims=True)
    hh = num / jnp.maximum(jnp.abs(den), jnp.exp(-m_t))
    m_new = m_t[:, L - 1:L, :]
    b_last = bc[:, L - 1:L, :]
    wc = jnp.exp(b_last - bc + ic - m_new)
    dec = jnp.exp(b_last + m_prev - m_new)
    c_scr[...] = (dec * ch + _bmm_tn(vh * wc, kh)).reshape(nb, D, HD)
    n_new = (dec * nh + jnp.sum(kh * wc, axis=1, keepdims=True)).reshape(nb, H, 1, HD)
    m_new = m_new.reshape(nb, H, 1, 1)
    m_row = jnp.zeros((nb, 1, HD), F32)
    for h in range(H):
        m_row = jnp.where(_iota(m_row.shape, 2) == h, m_new[:, h], m_row)
    nm_scr[...] = jnp.concatenate([n_new[:, h] for h in range(H)] + [m_row, nm[:, H + 1:, :]], axis=1)

    mu = jnp.mean(hh, axis=-1, keepdims=True)
    xc = hh - mu
    var = jnp.mean(xc * xc, axis=-1, keepdims=True)
    gn = jnp.concatenate([gn_ref[:, h * HD:(h + 1) * HD][None] for h in range(H)] * nb, axis=0)
    out = (xc * lax.rsqrt(var + ML_GN_EPS) * gn * _sigmoid(heads(o))).reshape(nb, H, L, HD)
    y_ref[...] = jnp.concatenate([out[:, h] for h in range(H)], axis=2)

    @pl.when(c == nc - 1)
    def _():
        cout_ref[...] = c_scr[...]
        nmout_ref[...] = nm_scr[...]
        cvout_ref[...] = cv_scr[...]


def _mlstm(zm, zif, state, params, L, nb):
    B, T, _ = zm.shape
    nc = T // L
    per_b = lambda shp: pl.BlockSpec((nb,) + shp, lambda b, c: (b, 0, 0))
    chunk = lambda w: pl.BlockSpec((nb, L, w), lambda b, c: (b, c, 0))
    state_shapes = ((SUBLANES, 2 * D_ML), (D_ML, ML_HD), (SUBLANES, ML_HD))
    state_args, state_specs = (), []
    if state is not None:
        layer, cv0, c0, nm0 = state
        state_args = (cv0, c0, nm0)
        state_specs = [per_b(state_shapes[0]),
                       pl.BlockSpec((None, nb) + state_shapes[1], lambda b, c: (layer, b, 0, 0)),
                       per_b(state_shapes[2])]
    return pl.pallas_call(
        functools.partial(_mlstm_body, L, nc, nb, state is not None),
        out_shape=(jax.ShapeDtypeStruct((B, T, D_ML), F32),
                   jax.ShapeDtypeStruct((B,) + state_shapes[1], F32),
                   jax.ShapeDtypeStruct((B,) + state_shapes[2], F32),
                   jax.ShapeDtypeStruct((B,) + state_shapes[0], F32)),
        grid=(B // nb, nc),
        in_specs=[chunk(ML_MAIN_W), chunk(GATE_W)] + state_specs + [_const_spec(p.shape) for p in params],
        out_specs=(chunk(D_ML), per_b(state_shapes[1]), per_b(state_shapes[2]), per_b(state_shapes[0])),
        scratch_shapes=[pltpu.VMEM((nb,) + state_shapes[1], F32), pltpu.VMEM((nb,) + state_shapes[2], F32),
                        pltpu.VMEM((nb,) + state_shapes[0], F32)],
        compiler_params=pltpu.CompilerParams(dimension_semantics=("arbitrary", "arbitrary"),
                                             vmem_limit_bytes=VMEM_LIMIT),
        name="mlstm",
    )(zm, zif, *state_args, *params)


def _merge_body(x_ref, ya_ref, yb_ref, pre_ref, post_ref, wg_ref, pa_ref, pb_ref, wo_ref, h_ref):
    x = x_ref[0]
    u = _rms(x, pre_ref[...]).astype(BF16)
    gate = _sigmoid(jnp.dot(u, wg_ref[...], preferred_element_type=F32))
    pa = jnp.dot(ya_ref[0].astype(BF16), pa_ref[...], preferred_element_type=F32)
    pb = jnp.dot(yb_ref[0].astype(BF16), pb_ref[...], preferred_element_type=F32)
    merged = gate[:, :D_MODEL] * pa + gate[:, D_MODEL:] * pb
    o = jnp.dot(merged.astype(BF16), wo_ref[...], preferred_element_type=F32)
    h_ref[0] = x + _rms(o, post_ref[...])


def _merge(x, ya, yb, pre, post, wg, pa, pb, wo, tm):
    B, T, _ = x.shape
    row = lambda w: pl.BlockSpec((1, tm, w), lambda b, i: (b, i, 0))
    consts = (pre, post, wg, pa, pb, wo)
    return pl.pallas_call(
        _merge_body,
        out_shape=jax.ShapeDtypeStruct((B, T, D_MODEL), F32),
        grid=(B, T // tm),
        in_specs=[row(D_MODEL), row(D_RW), row(D_ML)] + [_const_spec(p.shape) for p in consts],
        out_specs=row(D_MODEL),
        compiler_params=pltpu.CompilerParams(dimension_semantics=("arbitrary", "arbitrary"),
                                             vmem_limit_bytes=VMEM_LIMIT),
        name="merge",
    )(x, ya, yb, *consts)


FF_SPLIT = 4


def _ffn_body(h_ref, pre_ref, post_ref, wu_ref, wd_ref, o_ref):
    h = h_ref[0]
    u = _rms(h, pre_ref[...]).astype(BF16)
    step = D_FF // FF_SPLIT
    f = None
    for j in range(FF_SPLIT):
        t = jnp.maximum(jnp.dot(u, wu_ref[:, j * step:(j + 1) * step], preferred_element_type=F32), 0.0)
        part = jnp.dot((t * t).astype(BF16), wd_ref[j * step:(j + 1) * step, :], preferred_element_type=F32)
        f = part if f is None else f + part
    o_ref[0] = h + _rms(f, post_ref[...])


def _ffn(h, pre, post, wu, wd, tm):
    B, T, _ = h.shape
    row = pl.BlockSpec((1, tm, D_MODEL), lambda b, i: (b, i, 0))
    consts = (pre, post, wu, wd)
    return pl.pallas_call(
        _ffn_body,
        out_shape=jax.ShapeDtypeStruct((B, T, D_MODEL), F32),
        grid=(B, T // tm),
        in_specs=[row] + [_const_spec(p.shape) for p in consts],
        out_specs=row,
        compiler_params=pltpu.CompilerParams(dimension_semantics=("arbitrary", "arbitrary"),
                                             vmem_limit_bytes=VMEM_LIMIT),
        name="ffn",
    )(h, *consts)


MAX_ROW_TILE = 1024


def _row_tile(T):
    return max(tm for tm in range(SUBLANES, min(MAX_ROW_TILE, max(T // 2, SUBLANES)) + 1, SUBLANES) if T % tm == 0)


def _layer(x, st, lp, L):
    B, T, _ = x.shape
    long_seq = T > L
    flat = lambda a: a.reshape(1, B * T, a.shape[-1])
    unflat = lambda a: a.reshape(B, T, a.shape[-1])
    xf = flat(x)
    tm = _row_tile(B * T)
    z_rw, z_ml, z_if = map(unflat, _in_proj(xf, lp['pre1'], lp['w_rw'], lp['w_ml'], lp['w_if'], tm))

    rw_state = ml_state = None
    if st is not None:
        l, S, sh, C, n, m, cb = st
        depth = S.shape[0]
        rw_state = (l, sh.reshape(depth, B, 1, RW_SHIFT_W), S.reshape(depth, B, D_RW, RW_HD))
        cv0 = jnp.pad(cb[l], ((0, 0), (SUBLANES - (CONV_W - 1), 0), (0, 0)))
        nm0 = jnp.concatenate([n[l], jnp.pad(m[l], ((0, 0), (0, ML_HD - ML_HEADS)))[:, None, :],
                               jnp.zeros((B, SUBLANES - ML_HEADS - 1, ML_HD), F32)], axis=1)
        ml_state = (l, cv0, C.reshape(depth, B, D_ML, ML_HD), nm0)
    ya, S1, sh1 = _rwkv(z_rw, rw_state, lp['rw_params'], L, _group(B, RW_SEQS_LONG if long_seq else RW_SEQS_SHORT))
    yb, C1, nm1, cv1 = _mlstm(z_ml, z_if, ml_state, lp['ml_params'], L,
                              _group(B, ML_SEQS_LONG if long_seq else ML_SEQS_SHORT))

    h = _merge(xf, flat(ya), flat(yb), lp['pre1'], lp['post1'], lp['w_gate'], lp['p_a'], lp['p_b'], lp['w_out'], tm)
    out = unflat(_ffn(h, lp['pre2'], lp['post2'], lp['w_ff_up'], lp['w_ff_down'], tm))
    new_state = (S1.reshape(B, RW_HEADS, RW_HD, RW_HD), sh1[:, 0, :], C1.reshape(B, ML_HEADS, ML_HD, ML_HD),
                 nm1[:, :ML_HEADS, :], nm1[:, ML_HEADS, :ML_HEADS], cv1[:, SUBLANES - (CONV_W - 1):, :])
    return out, new_state


def _layer_params(l, w_in, rw_mu, rw_w0, rw_w_up, rw_a0, rw_a_up, rw_g_up, rw_k_k, rw_k_a, rw_r_k, rw_gn_g,
                  rw_gn_b, ml_conv_w, ml_conv_b, ml_i_bias, ml_f_bias, ml_gn_g, p_a, p_b, w_out, pre1, post1,
                  pre2, post2, w_ff_up, w_ff_down):
    row = lambda a: a[l].reshape(1, -1).astype(F32)
    w = w_in[l]
    c_ml = RW_SHIFT_W
    c_if = c_ml + ML_MAIN_W
    c_gate = c_if + 2 * ML_HEADS
    half = RW_LORA_W // 2
    lora = jnp.zeros((RW_LORA_W, 2 * D_RW), F32)
    lora = lora.at[:half, :D_RW].set(rw_w_up[l]).at[half:, D_RW:].set(rw_a_up[l])
    lora_hi = lora.astype(BF16)
    lora_lo = (lora - lora_hi.astype(F32)).astype(BF16)
    lora = jnp.concatenate([lora_hi, lora_hi, lora_lo], axis=0)
    if_bias = jnp.zeros((1, GATE_W), F32)
    if_bias = if_bias.at[0, :ML_HEADS].set(ml_i_bias[l]).at[0, ML_HEADS:2 * ML_HEADS].set(ml_f_bias[l])
    return dict(
        pre1=row(pre1), post1=row(post1), pre2=row(pre2), post2=row(post2),
        w_rw=w[:, :c_ml].astype(BF16),
        w_ml=w[:, c_ml:c_if].astype(BF16),
        w_if=jnp.pad(w[:, c_if:c_gate], ((0, 0), (0, GATE_W - 2 * ML_HEADS))).astype(BF16),
        w_gate=w[:, c_gate:].astype(BF16),
        rw_params=(row(rw_mu), row(rw_w0), row(rw_a0), lora, rw_g_up[l].astype(BF16), row(rw_k_k), row(rw_k_a),
                   row(rw_r_k), row(rw_gn_g), row(rw_gn_b)),
        ml_params=(ml_conv_w[l].astype(F32), row(ml_conv_b), if_bias, row(ml_gn_g)),
        p_a=p_a[l].astype(BF16), p_b=p_b[l].astype(BF16), w_out=w_out[l].astype(BF16),
        w_ff_up=w_ff_up[l].astype(BF16), w_ff_down=w_ff_down[l].astype(BF16),
    )


def kernel(x_prompt, x_sample, state_rwkv_S, state_rwkv_shift, state_mlstm_C, state_mlstm_n, state_mlstm_m,
           state_mlstm_conv, meta_tokens, w_in, rw_mu, rw_w0, rw_w_up, rw_a0, rw_a_up, rw_g_up, rw_k_k, rw_k_a,
           rw_r_k, rw_gn_g, rw_gn_b, ml_conv_w, ml_conv_b, ml_i_bias, ml_f_bias, ml_gn_g, p_a, p_b, w_out, pre1,
           post1, pre2, post2, w_ff_up, w_ff_down):
    B = x_prompt.shape[0]
    dt = x_prompt.dtype
    depth = w_in.shape[0]
    xm = meta_tokens[None].astype(dt)
    xp = x_prompt
    xs = x_sample
    p_states, s_states = [], []
    for l in range(depth):
        lp = _layer_params(l, w_in, rw_mu, rw_w0, rw_w_up, rw_a0, rw_a_up, rw_g_up, rw_k_k, rw_k_a, rw_r_k,
                           rw_gn_g, rw_gn_b, ml_conv_w, ml_conv_b, ml_i_bias, ml_f_bias, ml_gn_g, p_a, p_b, w_out,
                           pre1, post1, pre2, post2, w_ff_up, w_ff_down)
        xm, st_m = _layer(xm, None, lp, N_META)
        after_meta = tuple(jnp.broadcast_to(a, (1, B) + a.shape[1:]) for a in st_m)
        xp, st_p = _layer(xp, (0,) + after_meta, lp, PROMPT_CHUNK)
        st_in = (l, state_rwkv_S, state_rwkv_shift, state_mlstm_C, state_mlstm_n, state_mlstm_m, state_mlstm_conv)
        xs, st_s = _layer(xs, st_in, lp, xs.shape[1])
        p_states.append(st_p)
        s_states.append(st_s)
    stk = lambda lst, i: jnp.stack([s[i] for s in lst]).astype(dt)
    return (xp, xs,
            stk(p_states, 0), stk(p_states, 1), stk(p_states, 2), stk(p_states, 3), stk(p_states, 4), stk(p_states, 5),
            stk(s_states, 0), stk(s_states, 1), stk(s_states, 2), stk(s_states, 3), stk(s_states, 4), stk(s_states, 5))
```

```python
import functools
import math

import jax
import jax.numpy as jnp
from jax import lax
from jax.experimental import pallas as pl
from jax.experimental.pallas import tpu as pltpu

F32 = jnp.float32
BF16 = jnp.bfloat16

D_MODEL = 1024
N_META = 16
RW_HEADS = 8
RW_HD = 64
D_RW = RW_HEADS * RW_HD
RW_HALF_HEADS = RW_HEADS // 2
RW_HALF = RW_HALF_HEADS * RW_HD
RW_LORA_W = 128
RW_G_LORA = 128
RW_SHIFT_W = 3 * D_RW + RW_LORA_W + RW_G_LORA
RW_GN_EPS = 64e-5
ML_HEADS = 4
ML_HD = 128
D_ML = ML_HEADS * ML_HD
CONV_W = 4
ML_MAIN_W = 4 * D_ML
ML_GN_EPS = 1e-5
GATE_W = 128
D_FF = 4 * D_MODEL
RMS_EPS = 1e-6
SUBLANES = 8

PROMPT_CHUNK = 64
VMEM_LIMIT = 56 * 2**20
RW_SEQS_LONG, RW_SEQS_SHORT = 8, 16
ML_SEQS_LONG, ML_SEQS_SHORT = 8, 16


def _bmm_nn(a, b):
    return jnp.einsum('gmk,gkn->gmn', a.astype(BF16), b.astype(BF16), preferred_element_type=F32)


def _bmm_nt(a, b):
    return jnp.einsum('gmk,gnk->gmn', a.astype(BF16), b.astype(BF16), preferred_element_type=F32)


def _bmm_tn(a, b):
    return jnp.einsum('gkm,gkn->gmn', a.astype(BF16), b.astype(BF16), preferred_element_type=F32)


def _neumann_half(n):
    L = n.shape[-1]
    t = n + (_iota((1, L, L), 1) == _iota((1, L, L), 2)).astype(F32)
    p = n
    for _ in range(max(L.bit_length() - 3, 0)):
        p = _bmm_nn(p, p)
        t = t + _bmm_nn(t, p)
    return t


def _split(x, terms):
    parts = []
    for _ in range(terms - 1):
        p = x.astype(BF16)
        parts.append(p)
        x = x - p.astype(F32)
    parts.append(x.astype(BF16))
    return parts


def _sel_right(x, sel, terms):
    pieces = _split(x, terms)
    return jnp.dot(jnp.concatenate(pieces, axis=1), jnp.concatenate([sel] * terms, axis=0),
                   preferred_element_type=F32)


def _cumsum_rows(x):
    row = _iota(x.shape, 1)
    s = 1
    while s < x.shape[1]:
        x = x + jnp.where(row >= s, pltpu.roll(x, s, 1), 0.0)
        s *= 2
    return x


def _sigmoid(x):
    return 0.5 * jnp.tanh(0.5 * x) + 0.5


def _iota(shape, dim):
    return lax.broadcasted_iota(jnp.int32, shape, dim)


def _rms(x, g):
    return x * lax.rsqrt(jnp.mean(x * x, axis=-1, keepdims=True) + RMS_EPS) * g


def _const_spec(shape):
    nd = len(shape)
    return pl.BlockSpec(shape, lambda *_: (0,) * nd, pipeline_mode=pl.Buffered(1))


def _resident(p):
    if not isinstance(p, tuple):
        return p, _const_spec(p.shape)
    stacked, layer = p
    nd = stacked.ndim - 1
    return stacked, pl.BlockSpec((None,) + stacked.shape[1:], lambda *_: (layer,) + (0,) * nd,
                                 pipeline_mode=pl.Buffered(1))


def _group(n, target):
    return max(d for d in range(1, target + 1) if n % d == 0)


def _in_proj_body(x_ref, g_ref, wrw_ref, wml_ref, wif_ref, zrw_ref, zml_ref, zif_ref):
    u = _rms(x_ref[0], g_ref[...]).astype(BF16)
    zrw_ref[0] = jnp.dot(u, wrw_ref[...], preferred_element_type=F32)
    zml_ref[0] = jnp.dot(u, wml_ref[...], preferred_element_type=F32)
    zif_ref[0] = jnp.dot(u, wif_ref[...], preferred_element_type=F32)


def _in_proj(x, g, w_rw, w_ml, w_if, tm):
    B, T, _ = x.shape
    row = lambda w: pl.BlockSpec((1, tm, w), lambda b, i: (b, i, 0))
    return pl.pallas_call(
        _in_proj_body,
        out_shape=(jax.ShapeDtypeStruct((B, T, RW_SHIFT_W), F32),
                   jax.ShapeDtypeStruct((B, T, ML_MAIN_W), F32),
                   jax.ShapeDtypeStruct((B, T, GATE_W), F32)),
        grid=(B, T // tm),
        in_specs=[row(D_MODEL), _const_spec((1, D_MODEL)), _const_spec(w_rw.shape), _const_spec(w_ml.shape),
                  _const_spec(w_if.shape)],
        out_specs=(row(RW_SHIFT_W), row(ML_MAIN_W), row(GATE_W)),
        compiler_params=pltpu.CompilerParams(dimension_semantics=("arbitrary", "arbitrary"),
                                             vmem_limit_bytes=VMEM_LIMIT),
        name="in_proj",
    )(x, g, w_rw, w_ml, w_if)


def _rwkv_chunk(L, at, bt, kt, rt, b_end, k_end, v, decay_end, S, bd32):
    H = RW_HALF_HEADS
    g = at.shape[0]
    lane_head = _iota((1, 1, RW_HALF), 2) >> 6
    hmask = [(lane_head == h).astype(F32) for h in range(H)]
    if L % 16 == 0:
        hmask16 = [m.astype(BF16) for m in hmask]
        stack = lambda x: jnp.concatenate([x.astype(BF16) * hmask16[h] for h in range(H)], axis=1)
    else:
        stack = lambda x: jnp.concatenate([x * hmask[h] for h in range(H)], axis=1).astype(BF16)

    n = _bmm_nt(stack(at), bt).reshape(g * H, L, L)
    n = jnp.where(_iota((1, L, L), 1) > _iota((1, L, L), 2), n, 0.0)
    tinv = _neumann_half(n).reshape(g, H, L, L)

    v_stack = stack(v)
    ar = jnp.concatenate([at, rt], axis=1).astype(BF16)
    att = _bmm_nt(ar, jnp.concatenate([stack(bt), stack(kt)], axis=1))
    s_col = _iota((1, L, H * L), 2) & (L - 1)
    strict = s_col < _iota((1, L, H * L), 1)
    a_ab = jnp.where(strict, att[:, :L, :H * L], 0.0)
    a_ak = jnp.where(strict, att[:, :L, H * L:], 0.0)
    incl = (_iota((1, L, 2 * H * L), 2) & (L - 1)) <= _iota((1, L, 2 * H * L), 1)
    a_r = jnp.where(incl, att[:, L:, :], 0.0)

    from_state = _bmm_nt(ar, S)
    wmat = from_state[:, :L] + _bmm_nn(a_ak, v_stack)

    def solve(rhs):
        rhs = rhs.astype(BF16)
        x = hmask[0] * _bmm_nn(tinv[:, 0], rhs)
        for h in range(1, H):
            x = x + hmask[h] * _bmm_nn(tinv[:, h], rhs)
        return x

    u = solve(wmat)
    ab_hi, ab_lo = _split(a_ab, 2)
    u_hi, u_lo = _split(u, 2)
    u_hi_stack = stack(u_hi)
    nu = (jnp.einsum('gmk,gkn->gmn', jnp.concatenate([ab_hi, ab_lo], axis=2),
                     jnp.concatenate([u_hi_stack, u_hi_stack], axis=1), preferred_element_type=F32)
          + jnp.einsum('gmk,gkn->gmn', ab_hi, stack(u_lo), preferred_element_type=F32))
    u = u + solve(wmat - u + nu)
    y = from_state[:, L:] + _bmm_nn(a_r, jnp.concatenate([stack(u), v_stack], axis=1))
    upd = _bmm_tn(jnp.concatenate([u.astype(v.dtype), v], axis=1), jnp.concatenate([b_end, k_end], axis=1))
    return y, S * decay_end + upd * bd32


def _rwkv_body(L, nc, nb, has_state, z_ref, *refs):
    if has_state:
        sh0_ref, s0_ref, *refs = refs
    (mu_ref, w0_ref, a0_ref, wc_ref, gup_ref, kk_ref, ka_ref, rk_ref, gng_ref, gnb_ref, bd16_ref, bd32_ref,
     y_ref, sout_ref, shout_ref, s_scr, prev_scr) = refs
    c = pl.program_id(1)
    D, DH = D_RW, RW_HALF
    bd16 = bd16_ref[...]
    bd32 = bd32_ref[...]

    def head_sum(x, terms):
        m = x.shape[1]
        x2 = x.reshape(nb * m, D)
        r = jnp.concatenate([_sel_right(x2[:, hf * DH:(hf + 1) * DH], bd16, terms) for hf in range(2)], axis=1)
        return r.reshape(nb, m, D)

    halves = lambda x: jnp.stack([x[:, :, :DH], x[:, :, DH:]], axis=1).reshape(2 * nb, x.shape[1], DH)

    @pl.when(c == 0)
    def _():
        if has_state:
            prev_scr[...] = sh0_ref[...]
            s0 = s0_ref[...].reshape(2 * nb, DH, RW_HD)
            s_scr[...] = jnp.concatenate([s0] * RW_HALF_HEADS, axis=2) * bd32
        else:
            prev_scr[...] = jnp.zeros(prev_scr.shape, F32)
            s_scr[...] = jnp.zeros(s_scr.shape, F32)

    z = z_ref[...]
    prev = jnp.where(_iota(z.shape, 1) == 0, prev_scr[...], pltpu.roll(z, 1, 1))
    prev_scr[...] = z[:, L - 1:L, :]
    zs = z + (prev - z) * mu_ref[...]
    r = zs[:, :, 0:D]
    kraw = zs[:, :, D:2 * D]
    v = zs[:, :, 2 * D:3 * D]
    xl = zs[:, :, 3 * D:3 * D + RW_LORA_W]
    gl = zs[:, :, 3 * D + RW_LORA_W:]
    xl = jnp.where(_iota(xl.shape, 2) < RW_LORA_W // 2, jnp.tanh(xl), xl)
    xl_hi, xl_lo = _split(xl, 2)
    lora = jnp.dot(jnp.concatenate([xl_hi, xl_lo, xl_hi], axis=2).reshape(nb * L, 3 * RW_LORA_W), wc_ref[...],
                   preferred_element_type=F32).reshape(nb, L, 2 * D)
    logw = -math.exp(-0.5) * _sigmoid(w0_ref[...] + lora[:, :, :D])
    a = _sigmoid(a0_ref[...] + lora[:, :, D:])
    g = jnp.dot(_sigmoid(gl).reshape(nb * L, RW_G_LORA).astype(BF16), gup_ref[...],
                preferred_element_type=F32).reshape(nb, L, D)
    kk = kraw * kk_ref[...]
    k = kraw * (1.0 + (a - 1.0) * ka_ref[...])
    sums = head_sum(jnp.concatenate([kk * kk, r * k * rk_ref[...]], axis=1), 1)
    kk = kk * lax.rsqrt(jnp.maximum(sums[:, :L], 1e-24))
    bonus = sums[:, L:]

    cl = _cumsum_rows(logw)
    cl_last = cl[:, L - 1:L, :]
    e_neg = jnp.exp(-cl)
    e_end = jnp.exp(cl_last - cl)
    kka = kk * a
    narrow = (lambda x: x.astype(BF16)) if L % 16 == 0 else (lambda x: x)
    operands = [halves(narrow(o)) for o in (-kk * jnp.exp(cl - logw), kka * e_neg, k * e_neg, r * jnp.exp(cl),
                                            kka * e_end, k * e_end, v)]
    y_g, s_new = _rwkv_chunk(L, *operands, halves(jnp.exp(cl_last)), s_scr[...], bd32)
    s_scr[...] = s_new
    y_g = y_g.reshape(nb, 2, L, DH)
    y = jnp.concatenate([y_g[:, 0], y_g[:, 1]], axis=2)

    stats = head_sum(jnp.concatenate([y, y * y], axis=1), 2) * (1.0 / RW_HD)
    mean = stats[:, :L]
    var = jnp.maximum(stats[:, L:] - mean * mean, 0.0)
    yn = (y - mean) * lax.rsqrt(var + RW_GN_EPS) * gng_ref[...] + gnb_ref[...]
    y_ref[...] = (yn + bonus * v) * g

    @pl.when(c == nc - 1)
    def _():
        compact = sum(s_new[:, :, h * RW_HD:(h + 1) * RW_HD] for h in range(RW_HALF_HEADS))
        sout_ref[...] = compact.reshape(nb, D, RW_HD)
        shout_ref[...] = z[:, L - 1:L, :]


def _rwkv(z, state, params, L, nb):
    B, T, _ = z.shape
    nc = T // L
    per_b = lambda shp: pl.BlockSpec((nb,) + shp, lambda b, c: (b, 0, 0))
    bd = (_iota((RW_HALF, RW_HALF), 0) >> 6) == (_iota((RW_HALF, RW_HALF), 1) >> 6)
    consts = tuple(params) + (bd.astype(BF16), bd.astype(F32))
    state_args, state_specs = (), []
    if state is not None:
        layer, shift0, s0 = state
        of_layer = lambda shp: pl.BlockSpec((None, nb) + shp, lambda b, c: (layer, b, 0, 0))
        state_args, state_specs = (shift0, s0), [of_layer((1, RW_SHIFT_W)), of_layer((D_RW, RW_HD))]
    return pl.pallas_call(
        functools.partial(_rwkv_body, L, nc, nb, state is not None),
        out_shape=(jax.ShapeDtypeStruct((B, T, D_RW), F32),
                   jax.ShapeDtypeStruct((B, D_RW, RW_HD), F32),
                   jax.ShapeDtypeStruct((B, 1, RW_SHIFT_W), F32)),
        grid=(B // nb, nc),
        in_specs=[pl.BlockSpec((nb, L, RW_SHIFT_W), lambda b, c: (b, c, 0))] + state_specs
                 + [_const_spec(p.shape) for p in consts],
        out_specs=(pl.BlockSpec((nb, L, D_RW), lambda b, c: (b, c, 0)),
                   per_b((D_RW, RW_HD)), per_b((1, RW_SHIFT_W))),
        scratch_shapes=[pltpu.VMEM((2 * nb, RW_HALF, RW_HALF), F32), pltpu.VMEM((nb, 1, RW_SHIFT_W), F32)],
        compiler_params=pltpu.CompilerParams(dimension_semantics=("arbitrary", "arbitrary"),
                                             vmem_limit_bytes=VMEM_LIMIT),
        name="rwkv",
    )(z, *state_args, *consts)


def _mlstm_body(L, nc, nb, has_state, zm_ref, zif_ref, *refs):
    if has_state:
        cv0_ref, c0_ref, nm0_ref, *refs = refs
    cw_ref, cb_ref, ifb_ref, gn_ref, y_ref, cout_ref, nmout_ref, cvout_ref, c_scr, nm_scr, cv_scr = refs
    c = pl.program_id(1)
    H, HD, D = ML_HEADS, ML_HD, D_ML
    G = nb * H

    @pl.when(c == 0)
    def _():
        if has_state:
            cv_scr[...] = cv0_ref[...]
            c_scr[...] = c0_ref[...]
            nm_scr[...] = nm0_ref[...]
        else:
            cv_scr[...] = jnp.zeros(cv_scr.shape, F32)
            c_scr[...] = jnp.zeros(c_scr.shape, F32)
            nm_scr[...] = jnp.zeros(nm_scr.shape, F32)

    heads = lambda x: jnp.stack([x[:, :, h * HD:(h + 1) * HD] for h in range(H)], axis=1).reshape(G, x.shape[1], HD)
    lane_pick = lambda x, lo: jnp.stack([x[:, :, lo + h:lo + h + 1] for h in range(H)], axis=1).reshape(G, x.shape[1], 1)
    row_pick = lambda x, lo: jnp.stack([x[:, lo + h:lo + h + 1, :] for h in range(H)], axis=1).reshape(G, 1, x.shape[2])

    zm = zm_ref[...]
    raw = zm[:, :, :2 * D]
    v = zm[:, :, 2 * D:3 * D]
    o = zm[:, :, 3 * D:]
    ext = jnp.concatenate([cv_scr[...], raw], axis=1)
    cv_scr[...] = ext[:, L:, :]
    cw = cw_ref[...]
    qk = cb_ref[...] + raw * cw[CONV_W - 1:CONV_W, :]
    for s in range(1, CONV_W):
        qk = qk + pltpu.roll(ext, s, 1)[:, SUBLANES:, :] * cw[CONV_W - 1 - s:CONV_W - s, :]
    qk = qk * _sigmoid(qk)
    qh = heads(qk[:, :, :D])
    kh = heads(qk[:, :, D:] * (HD ** -0.5))
    vh = heads(v)

    gi = zif_ref[...] + ifb_ref[...]
    lane = _iota(gi.shape, 2)
    lf = jnp.minimum(gi, 0.0) - jnp.log(1.0 + jnp.exp(-jnp.abs(gi)))
    gcol = jnp.where(lane < H, gi, jnp.where(lane < 2 * H, lf, 0.0))
    causal = _iota((1, L, L), 1) >= _iota((1, L, L), 2)
    b_col = _cumsum_rows(gcol)
    grow = jnp.swapaxes(gcol, 1, 2)
    b_row = jnp.swapaxes(b_col, 1, 2)
    bc, ic = lane_pick(b_col, H), lane_pick(gcol, 0)
    br, ir = row_pick(b_row, H), row_pick(grow, 0)

    nm = nm_scr[...]
    ch = c_scr[...].reshape(G, HD, HD)
    nh = row_pick(nm, 0)
    m_prev = lane_pick(nm[:, H:H + 1, :], 0)

    dlog = jnp.where(causal, bc - br + ir, -jnp.inf)
    inter = bc + m_prev
    m_t = jnp.maximum(inter, jnp.max(dlog, axis=-1, keepdims=True))
    amat = jnp.exp(dlog - m_t)
    sc = jnp.exp(inter - m_t)
    aqk = amat * _bmm_nt(qh, kh)
    num = _bmm_nn(aqk, vh) + sc * _bmm_nt(qh, ch)
    den = jnp.sum(aqk, axis=-1, keepdims=True) + sc * jnp.sum(qh * nh, axis=-1, keepdims=True)
    hh = num / jnp.maximum(jnp.abs(den), jnp.exp(-m_t))
    m_new = m_t[:, L - 1:L, :]
    b_last = bc[:, L - 1:L, :]
    wc = jnp.exp(b_last - bc + ic - m_new)
    dec = jnp.exp(b_last + m_prev - m_new)
    c_scr[...] = (dec * ch + _bmm_tn(vh * wc, kh)).reshape(nb, D, HD)
    n_new = (dec * nh + jnp.sum(kh * wc, axis=1, keepdims=True)).reshape(nb, H, 1, HD)
    m_new = m_new.reshape(nb, H, 1, 1)
    m_row = jnp.zeros((nb, 1, HD), F32)
    for h in range(H):
        m_row = jnp.where(_iota(m_row.shape, 2) == h, m_new[:, h], m_row)
    nm_scr[...] = jnp.concatenate([n_new[:, h] for h in range(H)] + [m_row, nm[:, H + 1:, :]], axis=1)

    mu = jnp.mean(hh, axis=-1, keepdims=True)
    xc = hh - mu
    var = jnp.mean(xc * xc, axis=-1, keepdims=True)
    gn = jnp.concatenate([gn_ref[:, h * HD:(h + 1) * HD][None] for h in range(H)] * nb, axis=0)
    out = (xc * lax.rsqrt(var + ML_GN_EPS) * gn * _sigmoid(heads(o))).reshape(nb, H, L, HD)
    y_ref[...] = jnp.concatenate([out[:, h] for h in range(H)], axis=2)

    @pl.when(c == nc - 1)
    def _():
        cout_ref[...] = c_scr[...]
        nmout_ref[...] = nm_scr[...]
        cvout_ref[...] = cv_scr[...]


def _mlstm(zm, zif, state, params, L, nb):
    B, T, _ = zm.shape
    nc = T // L
    per_b = lambda shp: pl.BlockSpec((nb,) + shp, lambda b, c: (b, 0, 0))
    chunk = lambda w: pl.BlockSpec((nb, L, w), lambda b, c: (b, c, 0))
    state_shapes = ((SUBLANES, 2 * D_ML), (D_ML, ML_HD), (SUBLANES, ML_HD))
    state_args, state_specs = (), []
    if state is not None:
        layer, cv0, c0, nm0 = state
        state_args = (cv0, c0, nm0)
        state_specs = [per_b(state_shapes[0]),
                       pl.BlockSpec((None, nb) + state_shapes[1], lambda b, c: (layer, b, 0, 0)),
                       per_b(state_shapes[2])]
    return pl.pallas_call(
        functools.partial(_mlstm_body, L, nc, nb, state is not None),
        out_shape=(jax.ShapeDtypeStruct((B, T, D_ML), F32),
                   jax.ShapeDtypeStruct((B,) + state_shapes[1], F32),
                   jax.ShapeDtypeStruct((B,) + state_shapes[2], F32),
                   jax.ShapeDtypeStruct((B,) + state_shapes[0], F32)),
        grid=(B // nb, nc),
        in_specs=[chunk(ML_MAIN_W), chunk(GATE_W)] + state_specs + [_const_spec(p.shape) for p in params],
        out_specs=(chunk(D_ML), per_b(state_shapes[1]), per_b(state_shapes[2]), per_b(state_shapes[0])),
        scratch_shapes=[pltpu.VMEM((nb,) + state_shapes[1], F32), pltpu.VMEM((nb,) + state_shapes[2], F32),
                        pltpu.VMEM((nb,) + state_shapes[0], F32)],
        compiler_params=pltpu.CompilerParams(dimension_semantics=("arbitrary", "arbitrary"),
                                             vmem_limit_bytes=VMEM_LIMIT),
        name="mlstm",
    )(zm, zif, *state_args, *params)


def _merge_body(x_ref, ya_ref, yb_ref, pre_ref, post_ref, wg_ref, pa_ref, pb_ref, wo_ref, h_ref):
    x = x_ref[0]
    u = _rms(x, pre_ref[...]).astype(BF16)
    gate = _sigmoid(jnp.dot(u, wg_ref[...], preferred_element_type=F32))
    pa = jnp.dot(ya_ref[0].astype(BF16), pa_ref[...], preferred_element_type=F32)
    pb = jnp.dot(yb_ref[0].astype(BF16), pb_ref[...], preferred_element_type=F32)
    merged = gate[:, :D_MODEL] * pa + gate[:, D_MODEL:] * pb
    o = jnp.dot(merged.astype(BF16), wo_ref[...], preferred_element_type=F32)
    h_ref[0] = x + _rms(o, post_ref[...])


def _merge(x, ya, yb, pre, post, wg, pa, pb, wo, tm):
    B, T, _ = x.shape
    row = lambda w: pl.BlockSpec((1, tm, w), lambda b, i: (b, i, 0))
    consts, const_specs = zip(*map(_resident, (pre, post, wg, pa, pb, wo)))
    return pl.pallas_call(
        _merge_body,
        out_shape=jax.ShapeDtypeStruct((B, T, D_MODEL), F32),
        grid=(B, T // tm),
        in_specs=[row(D_MODEL), row(D_RW), row(D_ML)] + list(const_specs),
        out_specs=row(D_MODEL),
        compiler_params=pltpu.CompilerParams(dimension_semantics=("arbitrary", "arbitrary"),
                                             vmem_limit_bytes=VMEM_LIMIT),
        name="merge",
    )(x, ya, yb, *consts)


FF_SPLIT = 4


def _ffn_body(h_ref, pre_ref, post_ref, wu_ref, wd_ref, o_ref):
    h = h_ref[0]
    u = _rms(h, pre_ref[...]).astype(BF16)
    step = D_FF // FF_SPLIT
    f = None
    for j in range(FF_SPLIT):
        t = jnp.maximum(jnp.dot(u, wu_ref[:, j * step:(j + 1) * step], preferred_element_type=F32), 0.0)
        part = jnp.dot((t * t).astype(BF16), wd_ref[j * step:(j + 1) * step, :], preferred_element_type=F32)
        f = part if f is None else f + part
    o_ref[0] = h + _rms(f, post_ref[...])


def _ffn(h, pre, post, wu, wd, tm):
    B, T, _ = h.shape
    row = pl.BlockSpec((1, tm, D_MODEL), lambda b, i: (b, i, 0))
    consts, const_specs = zip(*map(_resident, (pre, post, wu, wd)))
    return pl.pallas_call(
        _ffn_body,
        out_shape=jax.ShapeDtypeStruct((B, T, D_MODEL), F32),
        grid=(B, T // tm),
        in_specs=[row] + list(const_specs),
        out_specs=row,
        compiler_params=pltpu.CompilerParams(dimension_semantics=("arbitrary", "arbitrary"),
                                             vmem_limit_bytes=VMEM_LIMIT),
        name="ffn",
    )(h, *consts)


MAX_ROW_TILE = 1024


def _row_tile(T):
    return max(tm for tm in range(SUBLANES, min(MAX_ROW_TILE, max(T // 2, SUBLANES)) + 1, SUBLANES) if T % tm == 0)


def _layer(x, st, lp, L):
    B, T, _ = x.shape
    long_seq = T > L
    flat = lambda a: a.reshape(1, B * T, a.shape[-1])
    unflat = lambda a: a.reshape(B, T, a.shape[-1])
    xf = flat(x)
    tm = _row_tile(B * T)
    z_rw, z_ml, z_if = map(unflat, _in_proj(xf, lp['pre1'], lp['w_rw'], lp['w_ml'], lp['w_if'], tm))

    rw_state = ml_state = None
    if st is not None:
        l, S, sh, C, n, m, cb = st
        depth = S.shape[0]
        rw_state = (l, sh.reshape(depth, B, 1, RW_SHIFT_W), S.reshape(depth, B, D_RW, RW_HD))
        cv0 = jnp.pad(cb[l], ((0, 0), (SUBLANES - (CONV_W - 1), 0), (0, 0)))
        nm0 = jnp.concatenate([n[l], jnp.pad(m[l], ((0, 0), (0, ML_HD - ML_HEADS)))[:, None, :],
                               jnp.zeros((B, SUBLANES - ML_HEADS - 1, ML_HD), F32)], axis=1)
        ml_state = (l, cv0, C.reshape(depth, B, D_ML, ML_HD), nm0)
    ya, S1, sh1 = _rwkv(z_rw, rw_state, lp['rw_params'], L, _group(B, RW_SEQS_LONG if long_seq else RW_SEQS_SHORT))
    yb, C1, nm1, cv1 = _mlstm(z_ml, z_if, ml_state, lp['ml_params'], L,
                              _group(B, ML_SEQS_LONG if long_seq else ML_SEQS_SHORT))

    h = _merge(xf, flat(ya), flat(yb), lp['pre1'], lp['post1'], lp['w_gate'], lp['p_a'], lp['p_b'], lp['w_out'], tm)
    out = unflat(_ffn(h, lp['pre2'], lp['post2'], lp['w_ff_up'], lp['w_ff_down'], tm))
    new_state = (S1.reshape(B, RW_HEADS, RW_HD, RW_HD), sh1[:, 0, :], C1.reshape(B, ML_HEADS, ML_HD, ML_HD),
                 nm1[:, :ML_HEADS, :], nm1[:, ML_HEADS, :ML_HEADS], cv1[:, SUBLANES - (CONV_W - 1):, :])
    return out, new_state


def _layer_params(l, w_in, rw_mu, rw_w0, rw_w_up, rw_a0, rw_a_up, rw_g_up, rw_k_k, rw_k_a, rw_r_k, rw_gn_g,
                  rw_gn_b, ml_conv_w, ml_conv_b, ml_i_bias, ml_f_bias, ml_gn_g, p_a, p_b, w_out, pre1, post1,
                  pre2, post2, w_ff_up, w_ff_down):
    row = lambda a: a[l].reshape(1, -1).astype(F32)
    w = w_in[l]
    c_ml = RW_SHIFT_W
    c_if = c_ml + ML_MAIN_W
    c_gate = c_if + 2 * ML_HEADS
    half = RW_LORA_W // 2
    lora = jnp.zeros((RW_LORA_W, 2 * D_RW), F32)
    lora = lora.at[:half, :D_RW].set(rw_w_up[l]).at[half:, D_RW:].set(rw_a_up[l])
    lora_hi = lora.astype(BF16)
    lora_lo = (lora - lora_hi.astype(F32)).astype(BF16)
    lora = jnp.concatenate([lora_hi, lora_hi, lora_lo], axis=0)
    if_bias = jnp.zeros((1, GATE_W), F32)
    if_bias = if_bias.at[0, :ML_HEADS].set(ml_i_bias[l]).at[0, ML_HEADS:2 * ML_HEADS].set(ml_f_bias[l])
    return dict(
        pre1=row(pre1), post1=row(post1), pre2=row(pre2), post2=row(post2),
        w_rw=w[:, :c_ml].astype(BF16),
        w_ml=w[:, c_ml:c_if].astype(BF16),
        w_if=jnp.pad(w[:, c_if:c_gate], ((0, 0), (0, GATE_W - 2 * ML_HEADS))).astype(BF16),
        w_gate=w[:, c_gate:].astype(BF16),
        rw_params=(row(rw_mu), row(rw_w0), row(rw_a0), lora, rw_g_up[l].astype(BF16), row(rw_k_k), row(rw_k_a),
                   row(rw_r_k), row(rw_gn_g), row(rw_gn_b)),
        ml_params=(ml_conv_w[l].astype(F32), row(ml_conv_b), if_bias, row(ml_gn_g)),
        p_a=(p_a.astype(BF16), l), p_b=(p_b.astype(BF16), l), w_out=(w_out.astype(BF16), l),
        w_ff_up=(w_ff_up.astype(BF16), l), w_ff_down=(w_ff_down.astype(BF16), l),
    )


def kernel(x_prompt, x_sample, state_rwkv_S, state_rwkv_shift, state_mlstm_C, state_mlstm_n, state_mlstm_m,
           state_mlstm_conv, meta_tokens, w_in, rw_mu, rw_w0, rw_w_up, rw_a0, rw_a_up, rw_g_up, rw_k_k, rw_k_a,
           rw_r_k, rw_gn_g, rw_gn_b, ml_conv_w, ml_conv_b, ml_i_bias, ml_f_bias, ml_gn_g, p_a, p_b, w_out, pre1,
           post1, pre2, post2, w_ff_up, w_ff_down):
    B = x_prompt.shape[0]
    dt = x_prompt.dtype
    depth = w_in.shape[0]
    xm = meta_tokens[None].astype(dt)
    xp = x_prompt
    xs = x_sample
    p_states, s_states = [], []
    for l in range(depth):
        lp = _layer_params(l, w_in, rw_mu, rw_w0, rw_w_up, rw_a0, rw_a_up, rw_g_up, rw_k_k, rw_k_a, rw_r_k,
                           rw_gn_g, rw_gn_b, ml_conv_w, ml_conv_b, ml_i_bias, ml_f_bias, ml_gn_g, p_a, p_b, w_out,
                           pre1, post1, pre2, post2, w_ff_up, w_ff_down)
        xm, st_m = _layer(xm, None, lp, N_META)
        after_meta = tuple(jnp.broadcast_to(a, (1, B) + a.shape[1:]) for a in st_m)
        xp, st_p = _layer(xp, (0,) + after_meta, lp, PROMPT_CHUNK)
        st_in = (l, state_rwkv_S, state_rwkv_shift, state_mlstm_C, state_mlstm_n, state_mlstm_m, state_mlstm_conv)
        xs, st_s = _layer(xs, st_in, lp, xs.shape[1])
        p_states.append(st_p)
        s_states.append(st_s)
    stk = lambda lst, i: jnp.stack([s[i] for s in lst]).astype(dt)
    return (xp, xs,
            stk(p_states, 0), stk(p_states, 1), stk(p_states, 2), stk(p_states, 3), stk(p_states, 4), stk(p_states, 5),
            stk(s_states, 0), stk(s_states, 1), stk(s_states, 2), stk(s_states, 3), stk(s_states, 4), stk(s_states, 5))
```

```python
import functools
import math

import jax
import jax.numpy as jnp
from jax import lax
from jax.experimental import pallas as pl
from jax.experimental.pallas import tpu as pltpu

F32 = jnp.float32
BF16 = jnp.bfloat16

D_MODEL = 1024
N_META = 16
RW_HEADS = 8
RW_HD = 64
D_RW = RW_HEADS * RW_HD
RW_HALF_HEADS = RW_HEADS // 2
RW_HALF = RW_HALF_HEADS * RW_HD
RW_LORA_W = 128
RW_G_LORA = 128
RW_SHIFT_W = 3 * D_RW + RW_LORA_W + RW_G_LORA
RW_GN_EPS = 64e-5
ML_HEADS = 4
ML_HD = 128
D_ML = ML_HEADS * ML_HD
CONV_W = 4
ML_MAIN_W = 4 * D_ML
ML_GN_EPS = 1e-5
GATE_W = 128
D_FF = 4 * D_MODEL
RMS_EPS = 1e-6
SUBLANES = 8

PROMPT_CHUNK = 64
VMEM_LIMIT = 56 * 2**20
RW_SEQS_LONG, RW_SEQS_SHORT = 8, 16
ML_SEQS_LONG, ML_SEQS_SHORT = 8, 16


def _bmm_nn(a, b):
    return jnp.einsum('gmk,gkn->gmn', a.astype(BF16), b.astype(BF16), preferred_element_type=F32)


def _bmm_nt(a, b):
    return jnp.einsum('gmk,gnk->gmn', a.astype(BF16), b.astype(BF16), preferred_element_type=F32)


def _bmm_tn(a, b):
    return jnp.einsum('gkm,gkn->gmn', a.astype(BF16), b.astype(BF16), preferred_element_type=F32)


def _neumann_half(n):
    L = n.shape[-1]
    t = n + (_iota((1, L, L), 1) == _iota((1, L, L), 2)).astype(F32)
    p = n
    for _ in range(max(L.bit_length() - 3, 0)):
        p = _bmm_nn(p, p)
        t = t + _bmm_nn(t, p)
    return t


def _split(x, terms):
    parts = []
    for _ in range(terms - 1):
        p = x.astype(BF16)
        parts.append(p)
        x = x - p.astype(F32)
    parts.append(x.astype(BF16))
    return parts


def _sel_right(x, sel, terms):
    pieces = _split(x, terms)
    return jnp.dot(jnp.concatenate(pieces, axis=1), jnp.concatenate([sel] * terms, axis=0),
                   preferred_element_type=F32)


def _cumsum_rows(x):
    row = _iota(x.shape, 1)
    s = 1
    while s < x.shape[1]:
        x = x + jnp.where(row >= s, pltpu.roll(x, s, 1), 0.0)
        s *= 2
    return x


def _sigmoid(x):
    return 0.5 * jnp.tanh(0.5 * x) + 0.5


def _iota(shape, dim):
    return lax.broadcasted_iota(jnp.int32, shape, dim)


def _rms(x, g):
    return x * lax.rsqrt(jnp.mean(x * x, axis=-1, keepdims=True) + RMS_EPS) * g


def _const_spec(shape):
    nd = len(shape)
    return pl.BlockSpec(shape, lambda *_: (0,) * nd, pipeline_mode=pl.Buffered(1))


def _resident(p):
    if not isinstance(p, tuple):
        return p, _const_spec(p.shape)
    stacked, layer = p
    nd = stacked.ndim - 1
    return stacked, pl.BlockSpec((None,) + stacked.shape[1:], lambda *_: (layer,) + (0,) * nd,
                                 pipeline_mode=pl.Buffered(1))


def _group(n, target):
    return max(d for d in range(1, target + 1) if n % d == 0)


def _in_proj_body(x_ref, g_ref, wrw_ref, wml_ref, wif_ref, zrw_ref, zml_ref, zif_ref):
    u = _rms(x_ref[0], g_ref[...]).astype(BF16)
    zrw_ref[0] = jnp.dot(u, wrw_ref[...], preferred_element_type=F32)
    zml_ref[0] = jnp.dot(u, wml_ref[...], preferred_element_type=F32).astype(zml_ref.dtype)
    zif_ref[0] = jnp.dot(u, wif_ref[...], preferred_element_type=F32)


def _in_proj(x, g, w_rw, w_ml, w_if, tm, act):
    B, T, _ = x.shape
    row = lambda w: pl.BlockSpec((1, tm, w), lambda b, i: (b, i, 0))
    return pl.pallas_call(
        _in_proj_body,
        out_shape=(jax.ShapeDtypeStruct((B, T, RW_SHIFT_W), F32),
                   jax.ShapeDtypeStruct((B, T, ML_MAIN_W), act),
                   jax.ShapeDtypeStruct((B, T, GATE_W), F32)),
        grid=(B, T // tm),
        in_specs=[row(D_MODEL), _const_spec((1, D_MODEL)), _const_spec(w_rw.shape), _const_spec(w_ml.shape),
                  _const_spec(w_if.shape)],
        out_specs=(row(RW_SHIFT_W), row(ML_MAIN_W), row(GATE_W)),
        compiler_params=pltpu.CompilerParams(dimension_semantics=("arbitrary", "arbitrary"),
                                             vmem_limit_bytes=VMEM_LIMIT),
        name="in_proj",
    )(x, g, w_rw, w_ml, w_if)


def _rwkv_chunk(L, at, bt, kt, rt, b_end, k_end, v, decay_end, S, bd32):
    H = RW_HALF_HEADS
    g = at.shape[0]
    lane_head = _iota((1, 1, RW_HALF), 2) >> 6
    hmask = [(lane_head == h).astype(F32) for h in range(H)]
    if L % 16 == 0:
        hmask16 = [m.astype(BF16) for m in hmask]
        stack = lambda x: jnp.concatenate([x.astype(BF16) * hmask16[h] for h in range(H)], axis=1)
    else:
        stack = lambda x: jnp.concatenate([x * hmask[h] for h in range(H)], axis=1).astype(BF16)

    n = _bmm_nt(stack(at), bt).reshape(g * H, L, L)
    n = jnp.where(_iota((1, L, L), 1) > _iota((1, L, L), 2), n, 0.0)
    tinv = _neumann_half(n).reshape(g, H, L, L)

    v_stack = stack(v)
    ar = jnp.concatenate([at, rt], axis=1).astype(BF16)
    att = _bmm_nt(ar, jnp.concatenate([stack(bt), stack(kt)], axis=1))
    s_col = _iota((1, L, H * L), 2) & (L - 1)
    strict = s_col < _iota((1, L, H * L), 1)
    a_ab = jnp.where(strict, att[:, :L, :H * L], 0.0)
    a_ak = jnp.where(strict, att[:, :L, H * L:], 0.0)
    incl = (_iota((1, L, 2 * H * L), 2) & (L - 1)) <= _iota((1, L, 2 * H * L), 1)
    a_r = jnp.where(incl, att[:, L:, :], 0.0)

    from_state = _bmm_nt(ar, S)
    wmat = from_state[:, :L] + _bmm_nn(a_ak, v_stack)

    def solve(rhs):
        rhs = rhs.astype(BF16)
        x = hmask[0] * _bmm_nn(tinv[:, 0], rhs)
        for h in range(1, H):
            x = x + hmask[h] * _bmm_nn(tinv[:, h], rhs)
        return x

    u = solve(wmat)
    ab_hi, ab_lo = _split(a_ab, 2)
    u_hi, u_lo = _split(u, 2)
    u_hi_stack = stack(u_hi)
    nu = (jnp.einsum('gmk,gkn->gmn', jnp.concatenate([ab_hi, ab_lo], axis=2),
                     jnp.concatenate([u_hi_stack, u_hi_stack], axis=1), preferred_element_type=F32)
          + jnp.einsum('gmk,gkn->gmn', ab_hi, stack(u_lo), preferred_element_type=F32))
    u = u + solve(wmat - u + nu)
    y = from_state[:, L:] + _bmm_nn(a_r, jnp.concatenate([stack(u), v_stack], axis=1))
    upd = _bmm_tn(jnp.concatenate([u.astype(v.dtype), v], axis=1), jnp.concatenate([b_end, k_end], axis=1))
    return y, S * decay_end + upd * bd32


def _rwkv_body(L, nc, nb, has_state, z_ref, *refs):
    if has_state:
        sh0_ref, s0_ref, *refs = refs
    (mu_ref, w0_ref, a0_ref, wc_ref, gup_ref, kk_ref, ka_ref, rk_ref, gng_ref, gnb_ref, bd16_ref, bd32_ref,
     y_ref, sout_ref, shout_ref, s_scr, prev_scr) = refs
    c = pl.program_id(1)
    D, DH = D_RW, RW_HALF
    bd16 = bd16_ref[...]
    bd32 = bd32_ref[...]

    def head_sum(x, terms):
        m = x.shape[1]
        x2 = x.reshape(nb * m, D)
        r = jnp.concatenate([_sel_right(x2[:, hf * DH:(hf + 1) * DH], bd16, terms) for hf in range(2)], axis=1)
        return r.reshape(nb, m, D)

    halves = lambda x: jnp.stack([x[:, :, :DH], x[:, :, DH:]], axis=1).reshape(2 * nb, x.shape[1], DH)

    @pl.when(c == 0)
    def _():
        if has_state:
            prev_scr[...] = sh0_ref[...]
            s0 = s0_ref[...].reshape(2 * nb, DH, RW_HD)
            s_scr[...] = jnp.concatenate([s0] * RW_HALF_HEADS, axis=2) * bd32
        else:
            prev_scr[...] = jnp.zeros(prev_scr.shape, F32)
            s_scr[...] = jnp.zeros(s_scr.shape, F32)

    z = z_ref[...]
    prev = jnp.where(_iota(z.shape, 1) == 0, prev_scr[...], pltpu.roll(z, 1, 1))
    prev_scr[...] = z[:, L - 1:L, :]
    zs = z + (prev - z) * mu_ref[...]
    r = zs[:, :, 0:D]
    kraw = zs[:, :, D:2 * D]
    v = zs[:, :, 2 * D:3 * D]
    xl = zs[:, :, 3 * D:3 * D + RW_LORA_W]
    gl = zs[:, :, 3 * D + RW_LORA_W:]
    xl = jnp.where(_iota(xl.shape, 2) < RW_LORA_W // 2, jnp.tanh(xl), xl)
    xl_hi, xl_lo = _split(xl, 2)
    lora = jnp.dot(jnp.concatenate([xl_hi, xl_lo, xl_hi], axis=2).reshape(nb * L, 3 * RW_LORA_W), wc_ref[...],
                   preferred_element_type=F32).reshape(nb, L, 2 * D)
    logw = -math.exp(-0.5) * _sigmoid(w0_ref[...] + lora[:, :, :D])
    a = _sigmoid(a0_ref[...] + lora[:, :, D:])
    g = jnp.dot(_sigmoid(gl).reshape(nb * L, RW_G_LORA).astype(BF16), gup_ref[...],
                preferred_element_type=F32).reshape(nb, L, D)
    kk = kraw * kk_ref[...]
    k = kraw * (1.0 + (a - 1.0) * ka_ref[...])
    sums = head_sum(jnp.concatenate([kk * kk, r * k * rk_ref[...]], axis=1), 1)
    kk = kk * lax.rsqrt(jnp.maximum(sums[:, :L], 1e-24))
    bonus = sums[:, L:]

    cl = _cumsum_rows(logw)
    cl_last = cl[:, L - 1:L, :]
    e_neg = jnp.exp(-cl)
    e_end = jnp.exp(cl_last - cl)
    kka = kk * a
    narrow = (lambda x: x.astype(BF16)) if L % 16 == 0 else (lambda x: x)
    operands = [halves(narrow(o)) for o in (-kk * jnp.exp(cl - logw), kka * e_neg, k * e_neg, r * jnp.exp(cl),
                                            kka * e_end, k * e_end, v)]
    y_g, s_new = _rwkv_chunk(L, *operands, halves(jnp.exp(cl_last)), s_scr[...], bd32)
    s_scr[...] = s_new
    y_g = y_g.reshape(nb, 2, L, DH)
    y = jnp.concatenate([y_g[:, 0], y_g[:, 1]], axis=2)

    stats = head_sum(jnp.concatenate([y, y * y], axis=1), 2) * (1.0 / RW_HD)
    mean = stats[:, :L]
    var = jnp.maximum(stats[:, L:] - mean * mean, 0.0)
    yn = (y - mean) * lax.rsqrt(var + RW_GN_EPS) * gng_ref[...] + gnb_ref[...]
    y_ref[...] = ((yn + bonus * v) * g).astype(y_ref.dtype)

    @pl.when(c == nc - 1)
    def _():
        compact = sum(s_new[:, :, h * RW_HD:(h + 1) * RW_HD] for h in range(RW_HALF_HEADS))
        sout_ref[...] = compact.reshape(nb, D, RW_HD)
        shout_ref[...] = z[:, L - 1:L, :]


def _rwkv(z, state, params, L, nb, act):
    B, T, _ = z.shape
    nc = T // L
    per_b = lambda shp: pl.BlockSpec((nb,) + shp, lambda b, c: (b, 0, 0))
    bd = (_iota((RW_HALF, RW_HALF), 0) >> 6) == (_iota((RW_HALF, RW_HALF), 1) >> 6)
    consts = tuple(params) + (bd.astype(BF16), bd.astype(F32))
    state_args, state_specs = (), []
    if state is not None:
        layer, shift0, s0 = state
        of_layer = lambda shp: pl.BlockSpec((None, nb) + shp, lambda b, c: (layer, b, 0, 0))
        state_args, state_specs = (shift0, s0), [of_layer((1, RW_SHIFT_W)), of_layer((D_RW, RW_HD))]
    return pl.pallas_call(
        functools.partial(_rwkv_body, L, nc, nb, state is not None),
        out_shape=(jax.ShapeDtypeStruct((B, T, D_RW), act),
                   jax.ShapeDtypeStruct((B, D_RW, RW_HD), F32),
                   jax.ShapeDtypeStruct((B, 1, RW_SHIFT_W), F32)),
        grid=(B // nb, nc),
        in_specs=[pl.BlockSpec((nb, L, RW_SHIFT_W), lambda b, c: (b, c, 0))] + state_specs
                 + [_const_spec(p.shape) for p in consts],
        out_specs=(pl.BlockSpec((nb, L, D_RW), lambda b, c: (b, c, 0)),
                   per_b((D_RW, RW_HD)), per_b((1, RW_SHIFT_W))),
        scratch_shapes=[pltpu.VMEM((2 * nb, RW_HALF, RW_HALF), F32), pltpu.VMEM((nb, 1, RW_SHIFT_W), F32)],
        compiler_params=pltpu.CompilerParams(dimension_semantics=("arbitrary", "arbitrary"),
                                             vmem_limit_bytes=VMEM_LIMIT),
        name="rwkv",
    )(z, *state_args, *consts)


def _mlstm_body(L, nc, nb, has_state, zm_ref, zif_ref, *refs):
    if has_state:
        cv0_ref, c0_ref, nm0_ref, *refs = refs
    cw_ref, cb_ref, ifb_ref, gn_ref, y_ref, cout_ref, nmout_ref, cvout_ref, c_scr, nm_scr, cv_scr = refs
    c = pl.program_id(1)
    H, HD, D = ML_HEADS, ML_HD, D_ML
    G = nb * H

    @pl.when(c == 0)
    def _():
        if has_state:
            cv_scr[...] = cv0_ref[...]
            c_scr[...] = c0_ref[...]
            nm_scr[...] = nm0_ref[...]
        else:
            cv_scr[...] = jnp.zeros(cv_scr.shape, F32)
            c_scr[...] = jnp.zeros(c_scr.shape, F32)
            nm_scr[...] = jnp.zeros(nm_scr.shape, F32)

    heads = lambda x: jnp.stack([x[:, :, h * HD:(h + 1) * HD] for h in range(H)], axis=1).reshape(G, x.shape[1], HD)
    lane_pick = lambda x, lo: jnp.stack([x[:, :, lo + h:lo + h + 1] for h in range(H)], axis=1).reshape(G, x.shape[1], 1)
    row_pick = lambda x, lo: jnp.stack([x[:, lo + h:lo + h + 1, :] for h in range(H)], axis=1).reshape(G, 1, x.shape[2])

    zm = zm_ref[...].astype(F32)
    raw = zm[:, :, :2 * D]
    v = zm[:, :, 2 * D:3 * D]
    o = zm[:, :, 3 * D:]
    ext = jnp.concatenate([cv_scr[...], raw], axis=1)
    cv_scr[...] = ext[:, L:, :]
    cw = cw_ref[...]
    qk = cb_ref[...] + raw * cw[CONV_W - 1:CONV_W, :]
    for s in range(1, CONV_W):
        qk = qk + pltpu.roll(ext, s, 1)[:, SUBLANES:, :] * cw[CONV_W - 1 - s:CONV_W - s, :]
    qk = qk * _sigmoid(qk)
    qh = heads(qk[:, :, :D])
    kh = heads(qk[:, :, D:] * (HD ** -0.5))
    vh = heads(v)

    gi = zif_ref[...] + ifb_ref[...]
    lane = _iota(gi.shape, 2)
    lf = jnp.minimum(gi, 0.0) - jnp.log(1.0 + jnp.exp(-jnp.abs(gi)))
    gcol = jnp.where(lane < H, gi, jnp.where(lane < 2 * H, lf, 0.0))
    causal = _iota((1, L, L), 1) >= _iota((1, L, L), 2)
    b_col = _cumsum_rows(gcol)
    grow = jnp.swapaxes(gcol, 1, 2)
    b_row = jnp.swapaxes(b_col, 1, 2)
    bc, ic = lane_pick(b_col, H), lane_pick(gcol, 0)
    br, ir = row_pick(b_row, H), row_pick(grow, 0)

    nm = nm_scr[...]
    ch = c_scr[...].reshape(G, HD, HD)
    nh = row_pick(nm, 0)
    m_prev = lane_pick(nm[:, H:H + 1, :], 0)

    dlog = jnp.where(causal, bc - br + ir, -jnp.inf)
    inter = bc + m_prev
    m_t = jnp.maximum(inter, jnp.max(dlog, axis=-1, keepdims=True))
    amat = jnp.exp(dlog - m_t)
    sc = jnp.exp(inter - m_t)
    aqk = amat * _bmm_nt(qh, kh)
    num = _bmm_nn(aqk, vh) + sc * _bmm_nt(qh, ch)
    den = jnp.sum(aqk, axis=-1, keepdims=True) + sc * jnp.sum(qh * nh, axis=-1, keepdims=True)
    hh = num / jnp.maximum(jnp.abs(den), jnp.exp(-m_t))
    m_new = m_t[:, L - 1:L, :]
    b_last = bc[:, L - 1:L, :]
    wc = jnp.exp(b_last - bc + ic - m_new)
    dec = jnp.exp(b_last + m_prev - m_new)
    c_scr[...] = (dec * ch + _bmm_tn(vh * wc, kh)).reshape(nb, D, HD)
    n_new = (dec * nh + jnp.sum(kh * wc, axis=1, keepdims=True)).reshape(nb, H, 1, HD)
    m_new = m_new.reshape(nb, H, 1, 1)
    m_row = jnp.zeros((nb, 1, HD), F32)
    for h in range(H):
        m_row = jnp.where(_iota(m_row.shape, 2) == h, m_new[:, h], m_row)
    nm_scr[...] = jnp.concatenate([n_new[:, h] for h in range(H)] + [m_row, nm[:, H + 1:, :]], axis=1)

    mu = jnp.mean(hh, axis=-1, keepdims=True)
    xc = hh - mu
    var = jnp.mean(xc * xc, axis=-1, keepdims=True)
    gn = jnp.concatenate([gn_ref[:, h * HD:(h + 1) * HD][None] for h in range(H)] * nb, axis=0)
    out = (xc * lax.rsqrt(var + ML_GN_EPS) * gn * _sigmoid(heads(o))).reshape(nb, H, L, HD)
    y_ref[...] = jnp.concatenate([out[:, h] for h in range(H)], axis=2).astype(y_ref.dtype)

    @pl.when(c == nc - 1)
    def _():
        cout_ref[...] = c_scr[...]
        nmout_ref[...] = nm_scr[...]
        cvout_ref[...] = cv_scr[...]


def _mlstm(zm, zif, state, params, L, nb, act):
    B, T, _ = zm.shape
    nc = T // L
    per_b = lambda shp: pl.BlockSpec((nb,) + shp, lambda b, c: (b, 0, 0))
    chunk = lambda w: pl.BlockSpec((nb, L, w), lambda b, c: (b, c, 0))
    state_shapes = ((SUBLANES, 2 * D_ML), (D_ML, ML_HD), (SUBLANES, ML_HD))
    state_args, state_specs = (), []
    if state is not None:
        layer, cv0, c0, nm0 = state
        state_args = (cv0, c0, nm0)
        state_specs = [per_b(state_shapes[0]),
                       pl.BlockSpec((None, nb) + state_shapes[1], lambda b, c: (layer, b, 0, 0)),
                       per_b(state_shapes[2])]
    return pl.pallas_call(
        functools.partial(_mlstm_body, L, nc, nb, state is not None),
        out_shape=(jax.ShapeDtypeStruct((B, T, D_ML), act),
                   jax.ShapeDtypeStruct((B,) + state_shapes[1], F32),
                   jax.ShapeDtypeStruct((B,) + state_shapes[2], F32),
                   jax.ShapeDtypeStruct((B,) + state_shapes[0], F32)),
        grid=(B // nb, nc),
        in_specs=[chunk(ML_MAIN_W), chunk(GATE_W)] + state_specs + [_const_spec(p.shape) for p in params],
        out_specs=(chunk(D_ML), per_b(state_shapes[1]), per_b(state_shapes[2]), per_b(state_shapes[0])),
        scratch_shapes=[pltpu.VMEM((nb,) + state_shapes[1], F32), pltpu.VMEM((nb,) + state_shapes[2], F32),
                        pltpu.VMEM((nb,) + state_shapes[0], F32)],
        compiler_params=pltpu.CompilerParams(dimension_semantics=("arbitrary", "arbitrary"),
                                             vmem_limit_bytes=VMEM_LIMIT),
        name="mlstm",
    )(zm, zif, *state_args, *params)


def _merge_body(x_ref, ya_ref, yb_ref, pre_ref, post_ref, wg_ref, pa_ref, pb_ref, wo_ref, h_ref):
    x = x_ref[0]
    u = _rms(x, pre_ref[...]).astype(BF16)
    gate = _sigmoid(jnp.dot(u, wg_ref[...], preferred_element_type=F32))
    pa = jnp.dot(ya_ref[0].astype(BF16), pa_ref[...], preferred_element_type=F32)
    pb = jnp.dot(yb_ref[0].astype(BF16), pb_ref[...], preferred_element_type=F32)
    merged = gate[:, :D_MODEL] * pa + gate[:, D_MODEL:] * pb
    o = jnp.dot(merged.astype(BF16), wo_ref[...], preferred_element_type=F32)
    h_ref[0] = x + _rms(o, post_ref[...])


def _merge(x, ya, yb, pre, post, wg, pa, pb, wo, tm):
    B, T, _ = x.shape
    row = lambda w: pl.BlockSpec((1, tm, w), lambda b, i: (b, i, 0))
    consts, const_specs = zip(*map(_resident, (pre, post, wg, pa, pb, wo)))
    return pl.pallas_call(
        _merge_body,
        out_shape=jax.ShapeDtypeStruct((B, T, D_MODEL), F32),
        grid=(B, T // tm),
        in_specs=[row(D_MODEL), row(D_RW), row(D_ML)] + list(const_specs),
        out_specs=row(D_MODEL),
        compiler_params=pltpu.CompilerParams(dimension_semantics=("arbitrary", "arbitrary"),
                                             vmem_limit_bytes=VMEM_LIMIT),
        name="merge",
    )(x, ya, yb, *consts)


FF_SPLIT = 4


def _ffn_body(h_ref, pre_ref, post_ref, wu_ref, wd_ref, o_ref):
    h = h_ref[0]
    u = _rms(h, pre_ref[...]).astype(BF16)
    step = D_FF // FF_SPLIT
    f = None
    for j in range(FF_SPLIT):
        t = jnp.maximum(jnp.dot(u, wu_ref[:, j * step:(j + 1) * step], preferred_element_type=F32), 0.0)
        part = jnp.dot((t * t).astype(BF16), wd_ref[j * step:(j + 1) * step, :], preferred_element_type=F32)
        f = part if f is None else f + part
    o_ref[0] = h + _rms(f, post_ref[...])


def _ffn(h, pre, post, wu, wd, tm):
    B, T, _ = h.shape
    row = pl.BlockSpec((1, tm, D_MODEL), lambda b, i: (b, i, 0))
    consts, const_specs = zip(*map(_resident, (pre, post, wu, wd)))
    return pl.pallas_call(
        _ffn_body,
        out_shape=jax.ShapeDtypeStruct((B, T, D_MODEL), F32),
        grid=(B, T // tm),
        in_specs=[row] + list(const_specs),
        out_specs=row,
        compiler_params=pltpu.CompilerParams(dimension_semantics=("arbitrary", "arbitrary"),
                                             vmem_limit_bytes=VMEM_LIMIT),
        name="ffn",
    )(h, *consts)


MAX_ROW_TILE = 1024


def _row_tile(T):
    return max(tm for tm in range(SUBLANES, min(MAX_ROW_TILE, max(T // 2, SUBLANES)) + 1, SUBLANES) if T % tm == 0)


def _layer(x, st, lp, L):
    B, T, _ = x.shape
    long_seq = T > L
    flat = lambda a: a.reshape(1, B * T, a.shape[-1])
    unflat = lambda a: a.reshape(B, T, a.shape[-1])
    xf = flat(x)
    tm = _row_tile(B * T)
    act = BF16 if L % 16 == 0 else F32
    z_rw, z_ml, z_if = map(unflat, _in_proj(xf, lp['pre1'], lp['w_rw'], lp['w_ml'], lp['w_if'], tm, act))

    rw_state = ml_state = None
    if st is not None:
        l, S, sh, C, n, m, cb = st
        depth = S.shape[0]
        rw_state = (l, sh.reshape(depth, B, 1, RW_SHIFT_W), S.reshape(depth, B, D_RW, RW_HD))
        cv0 = jnp.pad(cb[l], ((0, 0), (SUBLANES - (CONV_W - 1), 0), (0, 0)))
        nm0 = jnp.concatenate([n[l], jnp.pad(m[l], ((0, 0), (0, ML_HD - ML_HEADS)))[:, None, :],
                               jnp.zeros((B, SUBLANES - ML_HEADS - 1, ML_HD), F32)], axis=1)
        ml_state = (l, cv0, C.reshape(depth, B, D_ML, ML_HD), nm0)
    ya, S1, sh1 = _rwkv(z_rw, rw_state, lp['rw_params'], L, _group(B, RW_SEQS_LONG if long_seq else RW_SEQS_SHORT),
                        act)
    yb, C1, nm1, cv1 = _mlstm(z_ml, z_if, ml_state, lp['ml_params'], L,
                              _group(B, ML_SEQS_LONG if long_seq else ML_SEQS_SHORT), act)

    h = _merge(xf, flat(ya), flat(yb), lp['pre1'], lp['post1'], lp['w_gate'], lp['p_a'], lp['p_b'], lp['w_out'], tm)
    out = unflat(_ffn(h, lp['pre2'], lp['post2'], lp['w_ff_up'], lp['w_ff_down'], tm))
    new_state = (S1.reshape(B, RW_HEADS, RW_HD, RW_HD), sh1[:, 0, :], C1.reshape(B, ML_HEADS, ML_HD, ML_HD),
                 nm1[:, :ML_HEADS, :], nm1[:, ML_HEADS, :ML_HEADS], cv1[:, SUBLANES - (CONV_W - 1):, :])
    return out, new_state


def _layer_params(l, w_in, rw_mu, rw_w0, rw_w_up, rw_a0, rw_a_up, rw_g_up, rw_k_k, rw_k_a, rw_r_k, rw_gn_g,
                  rw_gn_b, ml_conv_w, ml_conv_b, ml_i_bias, ml_f_bias, ml_gn_g, p_a, p_b, w_out, pre1, post1,
                  pre2, post2, w_ff_up, w_ff_down):
    row = lambda a: a[l].reshape(1, -1).astype(F32)
    w = w_in[l]
    c_ml = RW_SHIFT_W
    c_if = c_ml + ML_MAIN_W
    c_gate = c_if + 2 * ML_HEADS
    half = RW_LORA_W // 2
    lora = jnp.zeros((RW_LORA_W, 2 * D_RW), F32)
    lora = lora.at[:half, :D_RW].set(rw_w_up[l]).at[half:, D_RW:].set(rw_a_up[l])
    lora_hi = lora.astype(BF16)
    lora_lo = (lora - lora_hi.astype(F32)).astype(BF16)
    lora = jnp.concatenate([lora_hi, lora_hi, lora_lo], axis=0)
    if_bias = jnp.zeros((1, GATE_W), F32)
    if_bias = if_bias.at[0, :ML_HEADS].set(ml_i_bias[l]).at[0, ML_HEADS:2 * ML_HEADS].set(ml_f_bias[l])
    return dict(
        pre1=row(pre1), post1=row(post1), pre2=row(pre2), post2=row(post2),
        w_rw=w[:, :c_ml].astype(BF16),
        w_ml=w[:, c_ml:c_if].astype(BF16),
        w_if=jnp.pad(w[:, c_if:c_gate], ((0, 0), (0, GATE_W - 2 * ML_HEADS))).astype(BF16),
        w_gate=w[:, c_gate:].astype(BF16),
        rw_params=(row(rw_mu), row(rw_w0), row(rw_a0), lora, rw_g_up[l].astype(BF16), row(rw_k_k), row(rw_k_a),
                   row(rw_r_k), row(rw_gn_g), row(rw_gn_b)),
        ml_params=(ml_conv_w[l].astype(F32), row(ml_conv_b), if_bias, row(ml_gn_g)),
        p_a=(p_a.astype(BF16), l), p_b=(p_b.astype(BF16), l), w_out=(w_out.astype(BF16), l),
        w_ff_up=(w_ff_up.astype(BF16), l), w_ff_down=(w_ff_down.astype(BF16), l),
    )


def kernel(x_prompt, x_sample, state_rwkv_S, state_rwkv_shift, state_mlstm_C, state_mlstm_n, state_mlstm_m,
           state_mlstm_conv, meta_tokens, w_in, rw_mu, rw_w0, rw_w_up, rw_a0, rw_a_up, rw_g_up, rw_k_k, rw_k_a,
           rw_r_k, rw_gn_g, rw_gn_b, ml_conv_w, ml_conv_b, ml_i_bias, ml_f_bias, ml_gn_g, p_a, p_b, w_out, pre1,
           post1, pre2, post2, w_ff_up, w_ff_down):
    B = x_prompt.shape[0]
    dt = x_prompt.dtype
    depth = w_in.shape[0]
    xm = meta_tokens[None].astype(dt)
    xp = x_prompt
    xs = x_sample
    p_states, s_states = [], []
    for l in range(depth):
        lp = _layer_params(l, w_in, rw_mu, rw_w0, rw_w_up, rw_a0, rw_a_up, rw_g_up, rw_k_k, rw_k_a, rw_r_k,
                           rw_gn_g, rw_gn_b, ml_conv_w, ml_conv_b, ml_i_bias, ml_f_bias, ml_gn_g, p_a, p_b, w_out,
                           pre1, post1, pre2, post2, w_ff_up, w_ff_down)
        xm, st_m = _layer(xm, None, lp, N_META)
        after_meta = tuple(jnp.broadcast_to(a, (1, B) + a.shape[1:]) for a in st_m)
        xp, st_p = _layer(xp, (0,) + after_meta, lp, PROMPT_CHUNK)
        st_in = (l, state_rwkv_S, state_rwkv_shift, state_mlstm_C, state_mlstm_n, state_mlstm_m, state_mlstm_conv)
        xs, st_s = _layer(xs, st_in, lp, xs.shape[1])
        p_states.append(st_p)
        s_states.append(st_s)
    stk = lambda lst, i: jnp.stack([s[i] for s in lst]).astype(dt)
    return (xp, xs,
            stk(p_states, 0), stk(p_states, 1), stk(p_states, 2), stk(p_states, 3), stk(p_states, 4), stk(p_states, 5),
            stk(s_states, 0), stk(s_states, 1), stk(s_states, 2), stk(s_states, 3), stk(s_states, 4), stk(s_states, 5))
```

```python
import functools
import math

import jax
import jax.numpy as jnp
from jax import lax
from jax.experimental import pallas as pl
from jax.experimental.pallas import tpu as pltpu

F32 = jnp.float32
BF16 = jnp.bfloat16

D_MODEL = 1024
N_META = 16
RW_HEADS = 8
RW_HD = 64
D_RW = RW_HEADS * RW_HD
RW_HALF_HEADS = RW_HEADS // 2
RW_HALF = RW_HALF_HEADS * RW_HD
RW_LORA_W = 128
RW_G_LORA = 128
RW_SHIFT_W = 3 * D_RW + RW_LORA_W + RW_G_LORA
RW_GN_EPS = 64e-5
ML_HEADS = 4
ML_HD = 128
D_ML = ML_HEADS * ML_HD
CONV_W = 4
ML_MAIN_W = 4 * D_ML
ML_GN_EPS = 1e-5
GATE_W = 128
D_FF = 4 * D_MODEL
RMS_EPS = 1e-6
SUBLANES = 8

PROMPT_CHUNK = 64
VMEM_LIMIT = 56 * 2**20
RW_SEQS_LONG, RW_SEQS_SHORT = 8, 16
ML_SEQS_LONG, ML_SEQS_SHORT = 8, 16


def _bmm_nn(a, b):
    return jnp.einsum('gmk,gkn->gmn', a.astype(BF16), b.astype(BF16), preferred_element_type=F32)


def _bmm_nt(a, b):
    return jnp.einsum('gmk,gnk->gmn', a.astype(BF16), b.astype(BF16), preferred_element_type=F32)


def _bmm_tn(a, b):
    return jnp.einsum('gkm,gkn->gmn', a.astype(BF16), b.astype(BF16), preferred_element_type=F32)


def _neumann_half(n):
    L = n.shape[-1]
    t = n + (_iota((1, L, L), 1) == _iota((1, L, L), 2)).astype(F32)
    p = n
    for _ in range(max(L.bit_length() - 3, 0)):
        p = _bmm_nn(p, p)
        t = t + _bmm_nn(t, p)
    return t


def _split(x, terms):
    parts = []
    for _ in range(terms - 1):
        p = x.astype(BF16)
        parts.append(p)
        x = x - p.astype(F32)
    parts.append(x.astype(BF16))
    return parts


def _sel_right(x, sel, terms):
    pieces = _split(x, terms)
    return jnp.dot(jnp.concatenate(pieces, axis=1), jnp.concatenate([sel] * terms, axis=0),
                   preferred_element_type=F32)


def _cumsum_rows(x):
    row = _iota(x.shape, 1)
    s = 1
    while s < x.shape[1]:
        x = x + jnp.where(row >= s, pltpu.roll(x, s, 1), 0.0)
        s *= 2
    return x


def _sigmoid(x):
    return 0.5 * jnp.tanh(0.5 * x) + 0.5


def _iota(shape, dim):
    return lax.broadcasted_iota(jnp.int32, shape, dim)


def _rms(x, g):
    return x * lax.rsqrt(jnp.mean(x * x, axis=-1, keepdims=True) + RMS_EPS) * g


def _const_spec(shape):
    nd = len(shape)
    return pl.BlockSpec(shape, lambda *_: (0,) * nd, pipeline_mode=pl.Buffered(1))


def _resident(p):
    if not isinstance(p, tuple):
        return p, _const_spec(p.shape)
    stacked, layer = p
    nd = stacked.ndim - 1
    return stacked, pl.BlockSpec((None,) + stacked.shape[1:], lambda *_: (layer,) + (0,) * nd,
                                 pipeline_mode=pl.Buffered(1))


def _group(n, target):
    return max(d for d in range(1, target + 1) if n % d == 0)


def _rowwise_body(rows_fn, n_x, n_c, n_o, has_extra, *refs):
    refs = list(refs)
    take = lambda n: [refs.pop(0) for _ in range(n)]
    x_refs = take(n_x)
    e_refs = take(n_x) if has_extra else []
    c_refs = take(n_c)
    o_refs = take(n_o)
    eo_refs = take(n_o) if has_extra else []
    for o_ref, val in zip(o_refs, rows_fn(*[r[0] for r in x_refs], *c_refs)):
        o_ref[0] = val
    if has_extra:
        @pl.when(pl.program_id(1) == 0)
        def _():
            for o_ref, val in zip(eo_refs, rows_fn(*[r[0] for r in e_refs], *c_refs)):
                o_ref[0] = val


def _rowwise(name, rows_fn, xs, consts, out_widths, tm, extras=None):
    R = xs[0].shape[1]
    row = lambda w: pl.BlockSpec((1, tm, w), lambda b, i: (b, i, 0))
    consts, const_specs = zip(*map(_resident, consts))
    in_specs = [row(x.shape[2]) for x in xs]
    out_specs = [row(w) for w in out_widths]
    out_shape = [jax.ShapeDtypeStruct((1, R, w), F32) for w in out_widths]
    operands = list(xs)
    if extras is not None:
        E = extras[0].shape[1]
        whole = lambda w: pl.BlockSpec((1, E, w), lambda b, i: (0, 0, 0))
        in_specs += [whole(x.shape[2]) for x in extras]
        out_specs += [whole(w) for w in out_widths]
        out_shape += [jax.ShapeDtypeStruct((1, E, w), F32) for w in out_widths]
        operands += list(extras)
    return pl.pallas_call(
        functools.partial(_rowwise_body, rows_fn, len(xs), len(consts), len(out_widths), extras is not None),
        out_shape=tuple(out_shape),
        grid=(1, R // tm),
        in_specs=in_specs + list(const_specs),
        out_specs=tuple(out_specs),
        compiler_params=pltpu.CompilerParams(dimension_semantics=("arbitrary", "arbitrary"),
                                             vmem_limit_bytes=VMEM_LIMIT),
        name=name,
    )(*operands, *consts)


def _in_proj_rows(x, g_ref, wrw_ref, wml_ref, wif_ref):
    u = _rms(x, g_ref[...]).astype(BF16)
    return tuple(jnp.dot(u, w_ref[...], preferred_element_type=F32) for w_ref in (wrw_ref, wml_ref, wif_ref))


def _merge_rows(x, ya, yb, pre_ref, post_ref, wg_ref, pa_ref, pb_ref, wo_ref):
    u = _rms(x, pre_ref[...]).astype(BF16)
    gate = _sigmoid(jnp.dot(u, wg_ref[...], preferred_element_type=F32))
    pa = jnp.dot(ya.astype(BF16), pa_ref[...], preferred_element_type=F32)
    pb = jnp.dot(yb.astype(BF16), pb_ref[...], preferred_element_type=F32)
    merged = gate[:, :D_MODEL] * pa + gate[:, D_MODEL:] * pb
    o = jnp.dot(merged.astype(BF16), wo_ref[...], preferred_element_type=F32)
    return (x + _rms(o, post_ref[...]),)


FF_SPLIT = 4


def _ffn_rows(h, pre_ref, post_ref, wu_ref, wd_ref):
    u = _rms(h, pre_ref[...]).astype(BF16)
    step = D_FF // FF_SPLIT
    f = None
    for j in range(FF_SPLIT):
        t = jnp.maximum(jnp.dot(u, wu_ref[:, j * step:(j + 1) * step], preferred_element_type=F32), 0.0)
        part = jnp.dot((t * t).astype(BF16), wd_ref[j * step:(j + 1) * step, :], preferred_element_type=F32)
        f = part if f is None else f + part
    return (h + _rms(f, post_ref[...]),)


def _rwkv_chunk(L, at, bt, kt, rt, b_end, k_end, v, decay_end, S, bd32):
    H = RW_HALF_HEADS
    g = at.shape[0]
    lane_head = _iota((1, 1, RW_HALF), 2) >> 6
    hmask = [(lane_head == h).astype(F32) for h in range(H)]
    if L % 16 == 0:
        hmask16 = [m.astype(BF16) for m in hmask]
        stack = lambda x: jnp.concatenate([x.astype(BF16) * hmask16[h] for h in range(H)], axis=1)
    else:
        stack = lambda x: jnp.concatenate([x * hmask[h] for h in range(H)], axis=1).astype(BF16)

    n = _bmm_nt(stack(at), bt).reshape(g * H, L, L)
    n = jnp.where(_iota((1, L, L), 1) > _iota((1, L, L), 2), n, 0.0)
    tinv = _neumann_half(n).reshape(g, H, L, L)

    v_stack = stack(v)
    ar = jnp.concatenate([at, rt], axis=1).astype(BF16)
    att = _bmm_nt(ar, jnp.concatenate([stack(bt), stack(kt)], axis=1))
    s_col = _iota((1, L, H * L), 2) & (L - 1)
    strict = s_col < _iota((1, L, H * L), 1)
    a_ab = jnp.where(strict, att[:, :L, :H * L], 0.0)
    a_ak = jnp.where(strict, att[:, :L, H * L:], 0.0)
    incl = (_iota((1, L, 2 * H * L), 2) & (L - 1)) <= _iota((1, L, 2 * H * L), 1)
    a_r = jnp.where(incl, att[:, L:, :], 0.0)

    from_state = _bmm_nt(ar, S)
    wmat = from_state[:, :L] + _bmm_nn(a_ak, v_stack)

    def solve(rhs):
        rhs = rhs.astype(BF16)
        x = hmask[0] * _bmm_nn(tinv[:, 0], rhs)
        for h in range(1, H):
            x = x + hmask[h] * _bmm_nn(tinv[:, h], rhs)
        return x

    u = solve(wmat)
    ab_hi, ab_lo = _split(a_ab, 2)
    u_hi, u_lo = _split(u, 2)
    u_hi_stack = stack(u_hi)
    nu = (jnp.einsum('gmk,gkn->gmn', jnp.concatenate([ab_hi, ab_lo], axis=2),
                     jnp.concatenate([u_hi_stack, u_hi_stack], axis=1), preferred_element_type=F32)
          + jnp.einsum('gmk,gkn->gmn', ab_hi, stack(u_lo), preferred_element_type=F32))
    u = u + solve(wmat - u + nu)
    y = from_state[:, L:] + _bmm_nn(a_r, jnp.concatenate([stack(u), v_stack], axis=1))
    upd = _bmm_tn(jnp.concatenate([u.astype(v.dtype), v], axis=1), jnp.concatenate([b_end, k_end], axis=1))
    return y, S * decay_end + upd * bd32


def _rwkv_body(L, nc, nb, has_state, z_ref, *refs):
    if has_state:
        sh0_ref, s0_ref, *refs = refs
    (mu_ref, w0_ref, a0_ref, wc_ref, gup_ref, kk_ref, ka_ref, rk_ref, gng_ref, gnb_ref, bd16_ref, bd32_ref,
     y_ref, sout_ref, shout_ref, s_scr, prev_scr) = refs
    c = pl.program_id(1)
    D, DH = D_RW, RW_HALF
    bd16 = bd16_ref[...]
    bd32 = bd32_ref[...]

    def head_sum(x, terms):
        m = x.shape[1]
        x2 = x.reshape(nb * m, D)
        r = jnp.concatenate([_sel_right(x2[:, hf * DH:(hf + 1) * DH], bd16, terms) for hf in range(2)], axis=1)
        return r.reshape(nb, m, D)

    halves = lambda x: jnp.stack([x[:, :, :DH], x[:, :, DH:]], axis=1).reshape(2 * nb, x.shape[1], DH)

    @pl.when(c == 0)
    def _():
        if has_state:
            prev_scr[...] = sh0_ref[...]
            s0 = s0_ref[...].reshape(2 * nb, DH, RW_HD)
            s_scr[...] = jnp.concatenate([s0] * RW_HALF_HEADS, axis=2) * bd32
        else:
            prev_scr[...] = jnp.zeros(prev_scr.shape, F32)
            s_scr[...] = jnp.zeros(s_scr.shape, F32)

    z = z_ref[...]
    prev = jnp.where(_iota(z.shape, 1) == 0, prev_scr[...], pltpu.roll(z, 1, 1))
    prev_scr[...] = z[:, L - 1:L, :]
    zs = z + (prev - z) * mu_ref[...]
    r = zs[:, :, 0:D]
    kraw = zs[:, :, D:2 * D]
    v = zs[:, :, 2 * D:3 * D]
    xl = zs[:, :, 3 * D:3 * D + RW_LORA_W]
    gl = zs[:, :, 3 * D + RW_LORA_W:]
    xl = jnp.where(_iota(xl.shape, 2) < RW_LORA_W // 2, jnp.tanh(xl), xl)
    xl_hi, xl_lo = _split(xl, 2)
    lora = jnp.dot(jnp.concatenate([xl_hi, xl_lo, xl_hi], axis=2).reshape(nb * L, 3 * RW_LORA_W), wc_ref[...],
                   preferred_element_type=F32).reshape(nb, L, 2 * D)
    logw = -math.exp(-0.5) * _sigmoid(w0_ref[...] + lora[:, :, :D])
    a = _sigmoid(a0_ref[...] + lora[:, :, D:])
    g = jnp.dot(_sigmoid(gl).reshape(nb * L, RW_G_LORA).astype(BF16), gup_ref[...],
                preferred_element_type=F32).reshape(nb, L, D)
    kk = kraw * kk_ref[...]
    k = kraw * (1.0 + (a - 1.0) * ka_ref[...])
    sums = head_sum(jnp.concatenate([kk * kk, r * k * rk_ref[...]], axis=1), 1)
    kk = kk * lax.rsqrt(jnp.maximum(sums[:, :L], 1e-24))
    bonus = sums[:, L:]

    cl = _cumsum_rows(logw)
    cl_last = cl[:, L - 1:L, :]
    e_neg = jnp.exp(-cl)
    e_end = jnp.exp(cl_last - cl)
    kka = kk * a
    narrow = (lambda x: x.astype(BF16)) if L % 16 == 0 else (lambda x: x)
    operands = [halves(narrow(o)) for o in (-kk * jnp.exp(cl - logw), kka * e_neg, k * e_neg, r * jnp.exp(cl),
                                            kka * e_end, k * e_end, v)]
    y_g, s_new = _rwkv_chunk(L, *operands, halves(jnp.exp(cl_last)), s_scr[...], bd32)
    s_scr[...] = s_new
    y_g = y_g.reshape(nb, 2, L, DH)
    y = jnp.concatenate([y_g[:, 0], y_g[:, 1]], axis=2)

    stats = head_sum(jnp.concatenate([y, y * y], axis=1), 2) * (1.0 / RW_HD)
    mean = stats[:, :L]
    var = jnp.maximum(stats[:, L:] - mean * mean, 0.0)
    yn = (y - mean) * lax.rsqrt(var + RW_GN_EPS) * gng_ref[...] + gnb_ref[...]
    y_ref[...] = (yn + bonus * v) * g

    @pl.when(c == nc - 1)
    def _():
        compact = sum(s_new[:, :, h * RW_HD:(h + 1) * RW_HD] for h in range(RW_HALF_HEADS))
        sout_ref[...] = compact.reshape(nb, D, RW_HD)
        shout_ref[...] = z[:, L - 1:L, :]


def _rwkv(z, state, params, L, nb):
    B, T, _ = z.shape
    nc = T // L
    per_b = lambda shp: pl.BlockSpec((nb,) + shp, lambda b, c: (b, 0, 0))
    bd = (_iota((RW_HALF, RW_HALF), 0) >> 6) == (_iota((RW_HALF, RW_HALF), 1) >> 6)
    consts = tuple(params) + (bd.astype(BF16), bd.astype(F32))
    state_args, state_specs = (), []
    if state is not None:
        layer, shift0, s0 = state
        of_layer = lambda shp: pl.BlockSpec((None, nb) + shp, lambda b, c: (layer, b, 0, 0))
        state_args, state_specs = (shift0, s0), [of_layer((1, RW_SHIFT_W)), of_layer((D_RW, RW_HD))]
    return pl.pallas_call(
        functools.partial(_rwkv_body, L, nc, nb, state is not None),
        out_shape=(jax.ShapeDtypeStruct((B, T, D_RW), F32),
                   jax.ShapeDtypeStruct((B, D_RW, RW_HD), F32),
                   jax.ShapeDtypeStruct((B, 1, RW_SHIFT_W), F32)),
        grid=(B // nb, nc),
        in_specs=[pl.BlockSpec((nb, L, RW_SHIFT_W), lambda b, c: (b, c, 0))] + state_specs
                 + [_const_spec(p.shape) for p in consts],
        out_specs=(pl.BlockSpec((nb, L, D_RW), lambda b, c: (b, c, 0)),
                   per_b((D_RW, RW_HD)), per_b((1, RW_SHIFT_W))),
        scratch_shapes=[pltpu.VMEM((2 * nb, RW_HALF, RW_HALF), F32), pltpu.VMEM((nb, 1, RW_SHIFT_W), F32)],
        compiler_params=pltpu.CompilerParams(dimension_semantics=("arbitrary", "arbitrary"),
                                             vmem_limit_bytes=VMEM_LIMIT),
        name="rwkv",
    )(z, *state_args, *consts)


def _mlstm_body(L, nc, nb, has_state, zm_ref, zif_ref, *refs):
    if has_state:
        cv0_ref, c0_ref, nm0_ref, *refs = refs
    cw_ref, cb_ref, ifb_ref, gn_ref, y_ref, cout_ref, nmout_ref, cvout_ref, c_scr, nm_scr, cv_scr = refs
    c = pl.program_id(1)
    H, HD, D = ML_HEADS, ML_HD, D_ML
    G = nb * H

    @pl.when(c == 0)
    def _():
        if has_state:
            cv_scr[...] = cv0_ref[...]
            c_scr[...] = c0_ref[...]
            nm_scr[...] = nm0_ref[...]
        else:
            cv_scr[...] = jnp.zeros(cv_scr.shape, F32)
            c_scr[...] = jnp.zeros(c_scr.shape, F32)
            nm_scr[...] = jnp.zeros(nm_scr.shape, F32)

    heads = lambda x: jnp.stack([x[:, :, h * HD:(h + 1) * HD] for h in range(H)], axis=1).reshape(G, x.shape[1], HD)
    lane_pick = lambda x, lo: jnp.stack([x[:, :, lo + h:lo + h + 1] for h in range(H)], axis=1).reshape(G, x.shape[1], 1)
    row_pick = lambda x, lo: jnp.stack([x[:, lo + h:lo + h + 1, :] for h in range(H)], axis=1).reshape(G, 1, x.shape[2])

    zm = zm_ref[...]
    raw = zm[:, :, :2 * D]
    v = zm[:, :, 2 * D:3 * D]
    o = zm[:, :, 3 * D:]
    ext = jnp.concatenate([cv_scr[...], raw], axis=1)
    cv_scr[...] = ext[:, L:, :]
    cw = cw_ref[...]
    qk = cb_ref[...] + raw * cw[CONV_W - 1:CONV_W, :]
    for s in range(1, CONV_W):
        qk = qk + pltpu.roll(ext, s, 1)[:, SUBLANES:, :] * cw[CONV_W - 1 - s:CONV_W - s, :]
    qk = qk * _sigmoid(qk)
    qh = heads(qk[:, :, :D])
    kh = heads(qk[:, :, D:] * (HD ** -0.5))
    vh = heads(v)

    gi = zif_ref[...] + ifb_ref[...]
    lane = _iota(gi.shape, 2)
    lf = jnp.minimum(gi, 0.0) - jnp.log(1.0 + jnp.exp(-jnp.abs(gi)))
    gcol = jnp.where(lane < H, gi, jnp.where(lane < 2 * H, lf, 0.0))
    causal = _iota((1, L, L), 1) >= _iota((1, L, L), 2)
    b_col = _cumsum_rows(gcol)
    grow = jnp.swapaxes(gcol, 1, 2)
    b_row = jnp.swapaxes(b_col, 1, 2)
    bc, ic = lane_pick(b_col, H), lane_pick(gcol, 0)
    br, ir = row_pick(b_row, H), row_pick(grow, 0)

    nm = nm_scr[...]
    ch = c_scr[...].reshape(G, HD, HD)
    nh = row_pick(nm, 0)
    m_prev = lane_pick(nm[:, H:H + 1, :], 0)

    dlog = jnp.where(causal, bc - br + ir, -jnp.inf)
    inter = bc + m_prev
    m_t = jnp.maximum(inter, jnp.max(dlog, axis=-1, keepdims=True))
    amat = jnp.exp(dlog - m_t)
    sc = jnp.exp(inter - m_t)
    aqk = amat * _bmm_nt(qh, kh)
    num = _bmm_nn(aqk, vh) + sc * _bmm_nt(qh, ch)
    den = jnp.sum(aqk, axis=-1, keepdims=True) + sc * jnp.sum(qh * nh, axis=-1, keepdims=True)
    hh = num / jnp.maximum(jnp.abs(den), jnp.exp(-m_t))
    m_new = m_t[:, L - 1:L, :]
    b_last = bc[:, L - 1:L, :]
    wc = jnp.exp(b_last - bc + ic - m_new)
    dec = jnp.exp(b_last + m_prev - m_new)
    c_scr[...] = (dec * ch + _bmm_tn(vh * wc, kh)).reshape(nb, D, HD)
    n_new = (dec * nh + jnp.sum(kh * wc, axis=1, keepdims=True)).reshape(nb, H, 1, HD)
    m_new = m_new.reshape(nb, H, 1, 1)
    m_row = jnp.zeros((nb, 1, HD), F32)
    for h in range(H):
        m_row = jnp.where(_iota(m_row.shape, 2) == h, m_new[:, h], m_row)
    nm_scr[...] = jnp.concatenate([n_new[:, h] for h in range(H)] + [m_row, nm[:, H + 1:, :]], axis=1)

    mu = jnp.mean(hh, axis=-1, keepdims=True)
    xc = hh - mu
    var = jnp.mean(xc * xc, axis=-1, keepdims=True)
    gn = jnp.concatenate([gn_ref[:, h * HD:(h + 1) * HD][None] for h in range(H)] * nb, axis=0)
    out = (xc * lax.rsqrt(var + ML_GN_EPS) * gn * _sigmoid(heads(o))).reshape(nb, H, L, HD)
    y_ref[...] = jnp.concatenate([out[:, h] for h in range(H)], axis=2)

    @pl.when(c == nc - 1)
    def _():
        cout_ref[...] = c_scr[...]
        nmout_ref[...] = nm_scr[...]
        cvout_ref[...] = cv_scr[...]


def _mlstm(zm, zif, state, params, L, nb):
    B, T, _ = zm.shape
    nc = T // L
    per_b = lambda shp: pl.BlockSpec((nb,) + shp, lambda b, c: (b, 0, 0))
    chunk = lambda w: pl.BlockSpec((nb, L, w), lambda b, c: (b, c, 0))
    state_shapes = ((SUBLANES, 2 * D_ML), (D_ML, ML_HD), (SUBLANES, ML_HD))
    state_args, state_specs = (), []
    if state is not None:
        layer, cv0, c0, nm0 = state
        state_args = (cv0, c0, nm0)
        state_specs = [per_b(state_shapes[0]),
                       pl.BlockSpec((None, nb) + state_shapes[1], lambda b, c: (layer, b, 0, 0)),
                       per_b(state_shapes[2])]
    return pl.pallas_call(
        functools.partial(_mlstm_body, L, nc, nb, state is not None),
        out_shape=(jax.ShapeDtypeStruct((B, T, D_ML), F32),
                   jax.ShapeDtypeStruct((B,) + state_shapes[1], F32),
                   jax.ShapeDtypeStruct((B,) + state_shapes[2], F32),
                   jax.ShapeDtypeStruct((B,) + state_shapes[0], F32)),
        grid=(B // nb, nc),
        in_specs=[chunk(ML_MAIN_W), chunk(GATE_W)] + state_specs + [_const_spec(p.shape) for p in params],
        out_specs=(chunk(D_ML), per_b(state_shapes[1]), per_b(state_shapes[2]), per_b(state_shapes[0])),
        scratch_shapes=[pltpu.VMEM((nb,) + state_shapes[1], F32), pltpu.VMEM((nb,) + state_shapes[2], F32),
                        pltpu.VMEM((nb,) + state_shapes[0], F32)],
        compiler_params=pltpu.CompilerParams(dimension_semantics=("arbitrary", "arbitrary"),
                                             vmem_limit_bytes=VMEM_LIMIT),
        name="mlstm",
    )(zm, zif, *state_args, *params)


MAX_ROW_TILE = 1024


def _row_tile(T):
    return max(tm for tm in range(SUBLANES, min(MAX_ROW_TILE, max(T // 2, SUBLANES)) + 1, SUBLANES) if T % tm == 0)


def _recurrences(z_rw, z_ml, z_if, st, lp, L):
    B, T, _ = z_rw.shape
    long_seq = T > L
    rw_state = ml_state = None
    if st is not None:
        l, S, sh, C, n, m, cb = st
        depth = S.shape[0]
        rw_state = (l, sh.reshape(depth, B, 1, RW_SHIFT_W), S.reshape(depth, B, D_RW, RW_HD))
        cv0 = jnp.pad(cb[l], ((0, 0), (SUBLANES - (CONV_W - 1), 0), (0, 0)))
        nm0 = jnp.concatenate([n[l], jnp.pad(m[l], ((0, 0), (0, ML_HD - ML_HEADS)))[:, None, :],
                               jnp.zeros((B, SUBLANES - ML_HEADS - 1, ML_HD), F32)], axis=1)
        ml_state = (l, cv0, C.reshape(depth, B, D_ML, ML_HD), nm0)
    ya, S1, sh1 = _rwkv(z_rw, rw_state, lp['rw_params'], L, _group(B, RW_SEQS_LONG if long_seq else RW_SEQS_SHORT))
    yb, C1, nm1, cv1 = _mlstm(z_ml, z_if, ml_state, lp['ml_params'], L,
                              _group(B, ML_SEQS_LONG if long_seq else ML_SEQS_SHORT))
    new_state = (S1.reshape(B, RW_HEADS, RW_HD, RW_HD), sh1[:, 0, :], C1.reshape(B, ML_HEADS, ML_HD, ML_HD),
                 nm1[:, :ML_HEADS, :], nm1[:, ML_HEADS, :ML_HEADS], cv1[:, SUBLANES - (CONV_W - 1):, :])
    return ya, yb, new_state


def _layer(x, st, lp, L, prefix=None):
    B, T, _ = x.shape
    widths = (RW_SHIFT_W, ML_MAIN_W, GATE_W)
    flat = lambda a: a.reshape(1, B * T, a.shape[-1])
    unflat = lambda a: a.reshape(B, T, a.shape[-1])
    xf = flat(x)
    tm = _row_tile(B * T)
    in_consts = (lp['pre1'], lp['w_rw'], lp['w_ml'], lp['w_if'])
    merge_consts = (lp['pre1'], lp['post1'], lp['w_gate'], lp['p_a'], lp['p_b'], lp['w_out'])
    ffn_consts = (lp['pre2'], lp['post2'], lp['w_ff_up'], lp['w_ff_down'])
    if prefix is None:
        z = _rowwise("in_proj", _in_proj_rows, (xf,), in_consts, widths, tm)
        ya, yb, new_state = _recurrences(*map(unflat, z), st, lp, L)
        h, = _rowwise("merge", _merge_rows, (xf, flat(ya), flat(yb)), merge_consts, (D_MODEL,), tm)
        out, = _rowwise("ffn", _ffn_rows, (h,), ffn_consts, (D_MODEL,), tm)
        return unflat(out), new_state
    P = prefix.shape[1]
    z = _rowwise("in_proj", _in_proj_rows, (xf,), in_consts, widths, tm, extras=(prefix,))
    ya_p, yb_p, st_p = _recurrences(*z[3:], None, lp, P)
    after_prefix = (0,) + tuple(jnp.broadcast_to(a, (1, B) + a.shape[1:]) for a in st_p)
    ya, yb, new_state = _recurrences(*map(unflat, z[:3]), after_prefix, lp, L)
    h, h_p = _rowwise("merge", _merge_rows, (xf, flat(ya), flat(yb)), merge_consts, (D_MODEL,), tm,
                      extras=(prefix, ya_p, yb_p))
    out, out_p = _rowwise("ffn", _ffn_rows, (h,), ffn_consts, (D_MODEL,), tm, extras=(h_p,))
    return unflat(out), out_p, new_state


def _layer_params(l, w_in, rw_mu, rw_w0, rw_w_up, rw_a0, rw_a_up, rw_g_up, rw_k_k, rw_k_a, rw_r_k, rw_gn_g,
                  rw_gn_b, ml_conv_w, ml_conv_b, ml_i_bias, ml_f_bias, ml_gn_g, p_a, p_b, w_out, pre1, post1,
                  pre2, post2, w_ff_up, w_ff_down):
    row = lambda a: a[l].reshape(1, -1).astype(F32)
    w = w_in[l]
    c_ml = RW_SHIFT_W
    c_if = c_ml + ML_MAIN_W
    c_gate = c_if + 2 * ML_HEADS
    half = RW_LORA_W // 2
    lora = jnp.zeros((RW_LORA_W, 2 * D_RW), F32)
    lora = lora.at[:half, :D_RW].set(rw_w_up[l]).at[half:, D_RW:].set(rw_a_up[l])
    lora_hi = lora.astype(BF16)
    lora_lo = (lora - lora_hi.astype(F32)).astype(BF16)
    lora = jnp.concatenate([lora_hi, lora_hi, lora_lo], axis=0)
    if_bias = jnp.zeros((1, GATE_W), F32)
    if_bias = if_bias.at[0, :ML_HEADS].set(ml_i_bias[l]).at[0, ML_HEADS:2 * ML_HEADS].set(ml_f_bias[l])
    return dict(
        pre1=row(pre1), post1=row(post1), pre2=row(pre2), post2=row(post2),
        w_rw=w[:, :c_ml].astype(BF16),
        w_ml=w[:, c_ml:c_if].astype(BF16),
        w_if=jnp.pad(w[:, c_if:c_gate], ((0, 0), (0, GATE_W - 2 * ML_HEADS))).astype(BF16),
        w_gate=w[:, c_gate:].astype(BF16),
        rw_params=(row(rw_mu), row(rw_w0), row(rw_a0), lora, rw_g_up[l].astype(BF16), row(rw_k_k), row(rw_k_a),
                   row(rw_r_k), row(rw_gn_g), row(rw_gn_b)),
        ml_params=(ml_conv_w[l].astype(F32), row(ml_conv_b), if_bias, row(ml_gn_g)),
        p_a=(p_a.astype(BF16), l), p_b=(p_b.astype(BF16), l), w_out=(w_out.astype(BF16), l),
        w_ff_up=(w_ff_up.astype(BF16), l), w_ff_down=(w_ff_down.astype(BF16), l),
    )


def kernel(x_prompt, x_sample, state_rwkv_S, state_rwkv_shift, state_mlstm_C, state_mlstm_n, state_mlstm_m,
           state_mlstm_conv, meta_tokens, w_in, rw_mu, rw_w0, rw_w_up, rw_a0, rw_a_up, rw_g_up, rw_k_k, rw_k_a,
           rw_r_k, rw_gn_g, rw_gn_b, ml_conv_w, ml_conv_b, ml_i_bias, ml_f_bias, ml_gn_g, p_a, p_b, w_out, pre1,
           post1, pre2, post2, w_ff_up, w_ff_down):
    dt = x_prompt.dtype
    depth = w_in.shape[0]
    xm = meta_tokens[None].astype(dt)
    xp = x_prompt
    xs = x_sample
    p_states, s_states = [], []
    for l in range(depth):
        lp = _layer_params(l, w_in, rw_mu, rw_w0, rw_w_up, rw_a0, rw_a_up, rw_g_up, rw_k_k, rw_k_a, rw_r_k,
                           rw_gn_g, rw_gn_b, ml_conv_w, ml_conv_b, ml_i_bias, ml_f_bias, ml_gn_g, p_a, p_b, w_out,
                           pre1, post1, pre2, post2, w_ff_up, w_ff_down)
        xp, xm, st_p = _layer(xp, None, lp, PROMPT_CHUNK, prefix=xm)
        st_in = (l, state_rwkv_S, state_rwkv_shift, state_mlstm_C, state_mlstm_n, state_mlstm_m, state_mlstm_conv)
        xs, st_s = _layer(xs, st_in, lp, xs.shape[1])
        p_states.append(st_p)
        s_states.append(st_s)
    stk = lambda lst, i: jnp.stack([s[i] for s in lst]).astype(dt)
    return (xp, xs,
            stk(p_states, 0), stk(p_states, 1), stk(p_states, 2), stk(p_states, 3), stk(p_states, 4), stk(p_states, 5),
            stk(s_states, 0), stk(s_states, 1), stk(s_states, 2), stk(s_states, 3), stk(s_states, 4), stk(s_states, 5))
```

```python
import functools
import math

import jax
import jax.numpy as jnp
from jax import lax
from jax.experimental import pallas as pl
from jax.experimental.pallas import tpu as pltpu

F32 = jnp.float32
BF16 = jnp.bfloat16

D_MODEL = 1024
N_META = 16
RW_HEADS = 8
RW_HD = 64
RW_HD_BITS = RW_HD.bit_length() - 1
D_RW = RW_HEADS * RW_HD
RW_HALF_HEADS = RW_HEADS // 2
RW_HALF = RW_HALF_HEADS * RW_HD
RW_LORA_W = 128
RW_G_LORA = 128
RW_SHIFT_W = 3 * D_RW + RW_LORA_W + RW_G_LORA
RW_GN_EPS = 64e-5
ML_HEADS = 4
ML_HD = 128
D_ML = ML_HEADS * ML_HD
CONV_W = 4
ML_MAIN_W = 4 * D_ML
ML_GN_EPS = 1e-5
GATE_W = 128
D_FF = 4 * D_MODEL
RMS_EPS = 1e-6
SUBLANES = 8

PROMPT_CHUNK = 64
VMEM_LIMIT = 56 * 2**20
RW_SEQS_LONG, RW_SEQS_SHORT = 8, 16
ML_SEQS_LONG, ML_SEQS_SHORT = 8, 16


def _bmm_nn(a, b):
    return jnp.einsum('gmk,gkn->gmn', a.astype(BF16), b.astype(BF16), preferred_element_type=F32)


def _bmm_nt(a, b):
    return jnp.einsum('gmk,gnk->gmn', a.astype(BF16), b.astype(BF16), preferred_element_type=F32)


def _bmm_tn(a, b):
    return jnp.einsum('gkm,gkn->gmn', a.astype(BF16), b.astype(BF16), preferred_element_type=F32)


def _neumann_half(n):
    L = n.shape[-1]
    t = n + (_iota((1, L, L), 1) == _iota((1, L, L), 2)).astype(F32)
    p = n
    for _ in range(max(L.bit_length() - 3, 0)):
        p = _bmm_nn(p, p)
        t = t + _bmm_nn(t, p)
    return t


def _split(x, terms):
    parts = []
    for _ in range(terms - 1):
        p = x.astype(BF16)
        parts.append(p)
        x = x - p.astype(F32)
    parts.append(x.astype(BF16))
    return parts


def _sel_right(x, sel, terms):
    pieces = _split(x, terms)
    return jnp.dot(jnp.concatenate(pieces, axis=1), jnp.concatenate([sel] * terms, axis=0),
                   preferred_element_type=F32)


def _cumsum_rows(x):
    row = _iota(x.shape, 1)
    s = 1
    while s < x.shape[1]:
        x = x + jnp.where(row >= s, pltpu.roll(x, s, 1), 0.0)
        s *= 2
    return x


def _sigmoid(x):
    return 0.5 * jnp.tanh(0.5 * x) + 0.5


def _iota(shape, dim):
    return lax.broadcasted_iota(jnp.int32, shape, dim)


def _rms(x, g):
    return x * lax.rsqrt(jnp.mean(x * x, axis=-1, keepdims=True) + RMS_EPS) * g


def _const_spec(shape):
    nd = len(shape)
    return pl.BlockSpec(shape, lambda *_: (0,) * nd, pipeline_mode=pl.Buffered(1))


def _resident(p):
    if not isinstance(p, tuple):
        return p, _const_spec(p.shape)
    stacked, layer = p
    nd = stacked.ndim - 1
    return stacked, pl.BlockSpec((None,) + stacked.shape[1:], lambda *_: (layer,) + (0,) * nd,
                                 pipeline_mode=pl.Buffered(1))


def _group(n, target):
    return max(d for d in range(1, target + 1) if n % d == 0)


def _rowwise_body(rows_fn, n_x, n_c, n_o, has_extra, *refs):
    refs = list(refs)
    take = lambda n: [refs.pop(0) for _ in range(n)]
    x_refs = take(n_x)
    e_refs = take(n_x) if has_extra else []
    c_refs = take(n_c)
    o_refs = take(n_o)
    eo_refs = take(n_o) if has_extra else []
    for o_ref, val in zip(o_refs, rows_fn(*[r[0] for r in x_refs], *c_refs)):
        o_ref[0] = val
    if has_extra:
        @pl.when(pl.program_id(1) == 0)
        def _():
            for o_ref, val in zip(eo_refs, rows_fn(*[r[0] for r in e_refs], *c_refs)):
                o_ref[0] = val


def _rowwise(name, rows_fn, xs, consts, out_widths, tm, extras=None):
    R = xs[0].shape[1]
    row = lambda w: pl.BlockSpec((1, tm, w), lambda b, i: (b, i, 0))
    consts, const_specs = zip(*map(_resident, consts))
    in_specs = [row(x.shape[2]) for x in xs]
    out_specs = [row(w) for w in out_widths]
    out_shape = [jax.ShapeDtypeStruct((1, R, w), F32) for w in out_widths]
    operands = list(xs)
    if extras is not None:
        E = extras[0].shape[1]
        whole = lambda w: pl.BlockSpec((1, E, w), lambda b, i: (0, 0, 0))
        in_specs += [whole(x.shape[2]) for x in extras]
        out_specs += [whole(w) for w in out_widths]
        out_shape += [jax.ShapeDtypeStruct((1, E, w), F32) for w in out_widths]
        operands += list(extras)
    return pl.pallas_call(
        functools.partial(_rowwise_body, rows_fn, len(xs), len(consts), len(out_widths), extras is not None),
        out_shape=tuple(out_shape),
        grid=(1, R // tm),
        in_specs=in_specs + list(const_specs),
        out_specs=tuple(out_specs),
        compiler_params=pltpu.CompilerParams(dimension_semantics=("arbitrary", "arbitrary"),
                                             vmem_limit_bytes=VMEM_LIMIT),
        name=name,
    )(*operands, *consts)


def _in_proj_rows(x, g_ref, wrw_ref, wml_ref, wif_ref):
    u = _rms(x, g_ref[...]).astype(BF16)
    return tuple(jnp.dot(u, w_ref[...], preferred_element_type=F32) for w_ref in (wrw_ref, wml_ref, wif_ref))


def _merge_rows(x, ya, yb, pre_ref, post_ref, wg_ref, pa_ref, pb_ref, wo_ref):
    u = _rms(x, pre_ref[...]).astype(BF16)
    gate = _sigmoid(jnp.dot(u, wg_ref[...], preferred_element_type=F32))
    pa = jnp.dot(ya.astype(BF16), pa_ref[...], preferred_element_type=F32)
    pb = jnp.dot(yb.astype(BF16), pb_ref[...], preferred_element_type=F32)
    merged = gate[:, :D_MODEL] * pa + gate[:, D_MODEL:] * pb
    o = jnp.dot(merged.astype(BF16), wo_ref[...], preferred_element_type=F32)
    return (x + _rms(o, post_ref[...]),)


FF_SPLIT = 4


def _ffn_rows(h, pre_ref, post_ref, wu_ref, wd_ref):
    u = _rms(h, pre_ref[...]).astype(BF16)
    step = D_FF // FF_SPLIT
    f = None
    for j in range(FF_SPLIT):
        t = jnp.maximum(jnp.dot(u, wu_ref[:, j * step:(j + 1) * step], preferred_element_type=F32), 0.0)
        part = jnp.dot((t * t).astype(BF16), wd_ref[j * step:(j + 1) * step, :], preferred_element_type=F32)
        f = part if f is None else f + part
    return (h + _rms(f, post_ref[...]),)


def _rwkv_chunk(L, at, bt, kt, rt, b_end, k_end, v, decay_end, S, bd32):
    H = RW_HALF_HEADS
    g = at.shape[0]
    lane_head = _iota((1, 1, RW_HALF), 2) >> RW_HD_BITS
    hmask = [(lane_head == h).astype(F32) for h in range(H)]
    if L % 16 == 0:
        hmask16 = [m.astype(BF16) for m in hmask]
        stack = lambda x: jnp.concatenate([x.astype(BF16) * hmask16[h] for h in range(H)], axis=1)
    else:
        stack = lambda x: jnp.concatenate([x * hmask[h] for h in range(H)], axis=1).astype(BF16)

    n = _bmm_nt(stack(at), bt).reshape(g * H, L, L)
    n = jnp.where(_iota((1, L, L), 1) > _iota((1, L, L), 2), n, 0.0)
    tinv = _neumann_half(n).reshape(g, H, L, L)

    v_stack = stack(v)
    ar = jnp.concatenate([at, rt], axis=1).astype(BF16)
    att = _bmm_nt(ar, jnp.concatenate([stack(bt), stack(kt)], axis=1))
    s_col = _iota((1, L, H * L), 2) & (L - 1)
    strict = s_col < _iota((1, L, H * L), 1)
    a_ab = jnp.where(strict, att[:, :L, :H * L], 0.0)
    a_ak = jnp.where(strict, att[:, :L, H * L:], 0.0)
    incl = (_iota((1, L, 2 * H * L), 2) & (L - 1)) <= _iota((1, L, 2 * H * L), 1)
    a_r = jnp.where(incl, att[:, L:, :], 0.0)

    from_state = _bmm_nt(ar, S)
    wmat = from_state[:, :L] + _bmm_nn(a_ak, v_stack)

    def solve(rhs):
        rhs = rhs.astype(BF16)
        x = hmask[0] * _bmm_nn(tinv[:, 0], rhs)
        for h in range(1, H):
            x = x + hmask[h] * _bmm_nn(tinv[:, h], rhs)
        return x

    u = solve(wmat)
    ab_hi, ab_lo = _split(a_ab, 2)
    u_hi, u_lo = _split(u, 2)
    u_hi_stack = stack(u_hi)
    nu = (jnp.einsum('gmk,gkn->gmn', jnp.concatenate([ab_hi, ab_lo], axis=2),
                     jnp.concatenate([u_hi_stack, u_hi_stack], axis=1), preferred_element_type=F32)
          + jnp.einsum('gmk,gkn->gmn', ab_hi, stack(u_lo), preferred_element_type=F32))
    u = u + solve(wmat - u + nu)
    y = from_state[:, L:] + _bmm_nn(a_r, jnp.concatenate([stack(u), v_stack], axis=1))
    upd = _bmm_tn(jnp.concatenate([u.astype(v.dtype), v], axis=1), jnp.concatenate([b_end, k_end], axis=1))
    return y, S * decay_end + upd * bd32


def _rwkv_body(L, nc, nb, has_state, z_ref, *refs):
    if has_state:
        sh0_ref, s0_ref, *refs = refs
    (mu_ref, w0_ref, a0_ref, wc_ref, gup_ref, kk_ref, ka_ref, rk_ref, gng_ref, gnb_ref, bd16_ref, bd32_ref,
     y_ref, sout_ref, shout_ref, s_scr, prev_scr) = refs
    c = pl.program_id(1)
    D, DH = D_RW, RW_HALF
    bd16 = bd16_ref[...]
    bd32 = bd32_ref[...]

    def head_sum(x, terms):
        m = x.shape[1]
        x2 = x.reshape(nb * m, D)
        r = jnp.concatenate([_sel_right(x2[:, hf * DH:(hf + 1) * DH], bd16, terms) for hf in range(2)], axis=1)
        return r.reshape(nb, m, D)

    halves = lambda x: jnp.stack([x[:, :, :DH], x[:, :, DH:]], axis=1).reshape(2 * nb, x.shape[1], DH)

    @pl.when(c == 0)
    def _():
        if has_state:
            prev_scr[...] = sh0_ref[...]
            s0 = s0_ref[...].reshape(2 * nb, DH, RW_HD)
            s_scr[...] = jnp.concatenate([s0] * RW_HALF_HEADS, axis=2) * bd32
        else:
            prev_scr[...] = jnp.zeros(prev_scr.shape, F32)
            s_scr[...] = jnp.zeros(s_scr.shape, F32)

    z = z_ref[...]
    prev = jnp.where(_iota(z.shape, 1) == 0, prev_scr[...], pltpu.roll(z, 1, 1))
    prev_scr[...] = z[:, L - 1:L, :]
    zs = z + (prev - z) * mu_ref[...]
    r = zs[:, :, 0:D]
    kraw = zs[:, :, D:2 * D]
    v = zs[:, :, 2 * D:3 * D]
    xl = zs[:, :, 3 * D:3 * D + RW_LORA_W]
    gl = zs[:, :, 3 * D + RW_LORA_W:]
    xl = jnp.where(_iota(xl.shape, 2) < RW_LORA_W // 2, jnp.tanh(xl), xl)
    xl_hi, xl_lo = _split(xl, 2)
    lora = jnp.dot(jnp.concatenate([xl_hi, xl_lo, xl_hi], axis=2).reshape(nb * L, 3 * RW_LORA_W), wc_ref[...],
                   preferred_element_type=F32).reshape(nb, L, 2 * D)
    logw = -math.exp(-0.5) * _sigmoid(w0_ref[...] + lora[:, :, :D])
    a = _sigmoid(a0_ref[...] + lora[:, :, D:])
    g = jnp.dot(_sigmoid(gl).reshape(nb * L, RW_G_LORA).astype(BF16), gup_ref[...],
                preferred_element_type=F32).reshape(nb, L, D)
    kk = kraw * kk_ref[...]
    k = kraw * (1.0 + (a - 1.0) * ka_ref[...])
    sums = head_sum(jnp.concatenate([kk * kk, r * k * rk_ref[...]], axis=1), 1)
    kk = kk * lax.rsqrt(jnp.maximum(sums[:, :L], 1e-24))
    bonus = sums[:, L:]

    cl = _cumsum_rows(logw)
    cl_last = cl[:, L - 1:L, :]
    e_neg = jnp.exp(-cl)
    e_end = jnp.exp(cl_last - cl)
    kka = kk * a
    narrow = (lambda x: x.astype(BF16)) if L % 16 == 0 else (lambda x: x)
    operands = [halves(narrow(o)) for o in (-kk * jnp.exp(cl - logw), kka * e_neg, k * e_neg, r * jnp.exp(cl),
                                            kka * e_end, k * e_end, v)]
    y_g, s_new = _rwkv_chunk(L, *operands, halves(jnp.exp(cl_last)), s_scr[...], bd32)
    s_scr[...] = s_new
    y_g = y_g.reshape(nb, 2, L, DH)
    y = jnp.concatenate([y_g[:, 0], y_g[:, 1]], axis=2)

    stats = head_sum(jnp.concatenate([y, y * y], axis=1), 2) * (1.0 / RW_HD)
    mean = stats[:, :L]
    var = jnp.maximum(stats[:, L:] - mean * mean, 0.0)
    yn = (y - mean) * lax.rsqrt(var + RW_GN_EPS) * gng_ref[...] + gnb_ref[...]
    y_ref[...] = (yn + bonus * v) * g

    @pl.when(c == nc - 1)
    def _():
        compact = sum(s_new[:, :, h * RW_HD:(h + 1) * RW_HD] for h in range(RW_HALF_HEADS))
        sout_ref[...] = compact.reshape(nb, D, RW_HD)
        shout_ref[...] = z[:, L - 1:L, :]


def _rwkv(z, state, params, L, nb):
    B, T, _ = z.shape
    nc = T // L
    per_b = lambda shp: pl.BlockSpec((nb,) + shp, lambda b, c: (b, 0, 0))
    assert T % L == 0 and L & (L - 1) == 0, "chunk length must be a power of two dividing the sequence"
    bd = (_iota((RW_HALF, RW_HALF), 0) >> RW_HD_BITS) == (_iota((RW_HALF, RW_HALF), 1) >> RW_HD_BITS)
    consts = tuple(params) + (bd.astype(BF16), bd.astype(F32))
    state_args, state_specs = (), []
    if state is not None:
        layer, shift0, s0 = state
        of_layer = lambda shp: pl.BlockSpec((None, nb) + shp, lambda b, c: (layer, b, 0, 0))
        state_args, state_specs = (shift0, s0), [of_layer((1, RW_SHIFT_W)), of_layer((D_RW, RW_HD))]
    return pl.pallas_call(
        functools.partial(_rwkv_body, L, nc, nb, state is not None),
        out_shape=(jax.ShapeDtypeStruct((B, T, D_RW), F32),
                   jax.ShapeDtypeStruct((B, D_RW, RW_HD), F32),
                   jax.ShapeDtypeStruct((B, 1, RW_SHIFT_W), F32)),
        grid=(B // nb, nc),
        in_specs=[pl.BlockSpec((nb, L, RW_SHIFT_W), lambda b, c: (b, c, 0))] + state_specs
                 + [_const_spec(p.shape) for p in consts],
        out_specs=(pl.BlockSpec((nb, L, D_RW), lambda b, c: (b, c, 0)),
                   per_b((D_RW, RW_HD)), per_b((1, RW_SHIFT_W))),
        scratch_shapes=[pltpu.VMEM((2 * nb, RW_HALF, RW_HALF), F32), pltpu.VMEM((nb, 1, RW_SHIFT_W), F32)],
        compiler_params=pltpu.CompilerParams(dimension_semantics=("arbitrary", "arbitrary"),
                                             vmem_limit_bytes=VMEM_LIMIT),
        name="rwkv",
    )(z, *state_args, *consts)


def _mlstm_body(L, nc, nb, has_state, zm_ref, zif_ref, *refs):
    if has_state:
        cv0_ref, c0_ref, nm0_ref, *refs = refs
    cw_ref, cb_ref, ifb_ref, gn_ref, y_ref, cout_ref, nmout_ref, cvout_ref, c_scr, nm_scr, cv_scr = refs
    c = pl.program_id(1)
    H, HD, D = ML_HEADS, ML_HD, D_ML
    G = nb * H

    @pl.when(c == 0)
    def _():
        if has_state:
            cv_scr[...] = cv0_ref[...]
            c_scr[...] = c0_ref[...]
            nm_scr[...] = nm0_ref[...]
        else:
            cv_scr[...] = jnp.zeros(cv_scr.shape, F32)
            c_scr[...] = jnp.zeros(c_scr.shape, F32)
            nm_scr[...] = jnp.zeros(nm_scr.shape, F32)

    heads = lambda x: jnp.stack([x[:, :, h * HD:(h + 1) * HD] for h in range(H)], axis=1).reshape(G, x.shape[1], HD)
    lane_pick = lambda x, lo: jnp.stack([x[:, :, lo + h:lo + h + 1] for h in range(H)], axis=1).reshape(G, x.shape[1], 1)
    row_pick = lambda x, lo: jnp.stack([x[:, lo + h:lo + h + 1, :] for h in range(H)], axis=1).reshape(G, 1, x.shape[2])

    zm = zm_ref[...]
    raw = zm[:, :, :2 * D]
    v = zm[:, :, 2 * D:3 * D]
    o = zm[:, :, 3 * D:]
    ext = jnp.concatenate([cv_scr[...], raw], axis=1)
    cv_scr[...] = ext[:, L:, :]
    cw = cw_ref[...]
    qk = cb_ref[...] + raw * cw[CONV_W - 1:CONV_W, :]
    for s in range(1, CONV_W):
        qk = qk + pltpu.roll(ext, s, 1)[:, SUBLANES:, :] * cw[CONV_W - 1 - s:CONV_W - s, :]
    qk = qk * _sigmoid(qk)
    qh = heads(qk[:, :, :D])
    kh = heads(qk[:, :, D:] * (HD ** -0.5))
    vh = heads(v)

    gi = zif_ref[...] + ifb_ref[...]
    lane = _iota(gi.shape, 2)
    lf = jnp.minimum(gi, 0.0) - jnp.log(1.0 + jnp.exp(-jnp.abs(gi)))
    gcol = jnp.where(lane < H, gi, jnp.where(lane < 2 * H, lf, 0.0))
    causal = _iota((1, L, L), 1) >= _iota((1, L, L), 2)
    b_col = _cumsum_rows(gcol)
    grow = jnp.swapaxes(gcol, 1, 2)
    b_row = jnp.swapaxes(b_col, 1, 2)
    bc, ic = lane_pick(b_col, H), lane_pick(gcol, 0)
    br, ir = row_pick(b_row, H), row_pick(grow, 0)

    nm = nm_scr[...]
    ch = c_scr[...].reshape(G, HD, HD)
    nh = row_pick(nm, 0)
    m_prev = lane_pick(nm[:, H:H + 1, :], 0)

    dlog = jnp.where(causal, bc - br + ir, -jnp.inf)
    inter = bc + m_prev
    m_t = jnp.maximum(inter, jnp.max(dlog, axis=-1, keepdims=True))
    amat = jnp.exp(dlog - m_t)
    sc = jnp.exp(inter - m_t)
    aqk = amat * _bmm_nt(qh, kh)
    num = _bmm_nn(aqk, vh) + sc * _bmm_nt(qh, ch)
    den = jnp.sum(aqk, axis=-1, keepdims=True) + sc * jnp.sum(qh * nh, axis=-1, keepdims=True)
    hh = num / jnp.maximum(jnp.abs(den), jnp.exp(-m_t))
    m_new = m_t[:, L - 1:L, :]
    b_last = bc[:, L - 1:L, :]
    wc = jnp.exp(b_last - bc + ic - m_new)
    dec = jnp.exp(b_last + m_prev - m_new)
    c_scr[...] = (dec * ch + _bmm_tn(vh * wc, kh)).reshape(nb, D, HD)
    n_new = (dec * nh + jnp.sum(kh * wc, axis=1, keepdims=True)).reshape(nb, H, 1, HD)
    m_new = m_new.reshape(nb, H, 1, 1)
    m_row = jnp.zeros((nb, 1, HD), F32)
    for h in range(H):
        m_row = jnp.where(_iota(m_row.shape, 2) == h, m_new[:, h], m_row)
    nm_scr[...] = jnp.concatenate([n_new[:, h] for h in range(H)] + [m_row, nm[:, H + 1:, :]], axis=1)

    mu = jnp.mean(hh, axis=-1, keepdims=True)
    xc = hh - mu
    var = jnp.mean(xc * xc, axis=-1, keepdims=True)
    gn = jnp.concatenate([gn_ref[:, h * HD:(h + 1) * HD][None] for h in range(H)] * nb, axis=0)
    out = (xc * lax.rsqrt(var + ML_GN_EPS) * gn * _sigmoid(heads(o))).reshape(nb, H, L, HD)
    y_ref[...] = jnp.concatenate([out[:, h] for h in range(H)], axis=2)

    @pl.when(c == nc - 1)
    def _():
        cout_ref[...] = c_scr[...]
        nmout_ref[...] = nm_scr[...]
        cvout_ref[...] = cv_scr[...]


def _mlstm(zm, zif, state, params, L, nb):
    B, T, _ = zm.shape
    assert T % L == 0 and L % SUBLANES == 0, "chunk length must be sublane aligned and divide the sequence"
    nc = T // L
    per_b = lambda shp: pl.BlockSpec((nb,) + shp, lambda b, c: (b, 0, 0))
    chunk = lambda w: pl.BlockSpec((nb, L, w), lambda b, c: (b, c, 0))
    state_shapes = ((SUBLANES, 2 * D_ML), (D_ML, ML_HD), (SUBLANES, ML_HD))
    state_args, state_specs = (), []
    if state is not None:
        layer, cv0, c0, nm0 = state
        state_args = (cv0, c0, nm0)
        state_specs = [per_b(state_shapes[0]),
                       pl.BlockSpec((None, nb) + state_shapes[1], lambda b, c: (layer, b, 0, 0)),
                       per_b(state_shapes[2])]
    return pl.pallas_call(
        functools.partial(_mlstm_body, L, nc, nb, state is not None),
        out_shape=(jax.ShapeDtypeStruct((B, T, D_ML), F32),
                   jax.ShapeDtypeStruct((B,) + state_shapes[1], F32),
                   jax.ShapeDtypeStruct((B,) + state_shapes[2], F32),
                   jax.ShapeDtypeStruct((B,) + state_shapes[0], F32)),
        grid=(B // nb, nc),
        in_specs=[chunk(ML_MAIN_W), chunk(GATE_W)] + state_specs + [_const_spec(p.shape) for p in params],
        out_specs=(chunk(D_ML), per_b(state_shapes[1]), per_b(state_shapes[2]), per_b(state_shapes[0])),
        scratch_shapes=[pltpu.VMEM((nb,) + state_shapes[1], F32), pltpu.VMEM((nb,) + state_shapes[2], F32),
                        pltpu.VMEM((nb,) + state_shapes[0], F32)],
        compiler_params=pltpu.CompilerParams(dimension_semantics=("arbitrary", "arbitrary"),
                                             vmem_limit_bytes=VMEM_LIMIT),
        name="mlstm",
    )(zm, zif, *state_args, *params)


MAX_ROW_TILE = 1024


def _row_tile(T):
    return max(tm for tm in range(SUBLANES, min(MAX_ROW_TILE, max(T // 2, SUBLANES)) + 1, SUBLANES) if T % tm == 0)


def _recurrences(z_rw, z_ml, z_if, st, lp, L):
    B, T, _ = z_rw.shape
    long_seq = T > L
    rw_state = ml_state = None
    if st is not None:
        l, S, sh, C, n, m, cb = st
        depth = S.shape[0]
        rw_state = (l, sh.reshape(depth, B, 1, RW_SHIFT_W), S.reshape(depth, B, D_RW, RW_HD))
        cv0 = jnp.pad(cb[l], ((0, 0), (SUBLANES - (CONV_W - 1), 0), (0, 0)))
        nm0 = jnp.concatenate([n[l], jnp.pad(m[l], ((0, 0), (0, ML_HD - ML_HEADS)))[:, None, :],
                               jnp.zeros((B, SUBLANES - ML_HEADS - 1, ML_HD), F32)], axis=1)
        ml_state = (l, cv0, C.reshape(depth, B, D_ML, ML_HD), nm0)
    ya, S1, sh1 = _rwkv(z_rw, rw_state, lp['rw_params'], L, _group(B, RW_SEQS_LONG if long_seq else RW_SEQS_SHORT))
    yb, C1, nm1, cv1 = _mlstm(z_ml, z_if, ml_state, lp['ml_params'], L,
                              _group(B, ML_SEQS_LONG if long_seq else ML_SEQS_SHORT))
    new_state = (S1.reshape(B, RW_HEADS, RW_HD, RW_HD), sh1[:, 0, :], C1.reshape(B, ML_HEADS, ML_HD, ML_HD),
                 nm1[:, :ML_HEADS, :], nm1[:, ML_HEADS, :ML_HEADS], cv1[:, SUBLANES - (CONV_W - 1):, :])
    return ya, yb, new_state


def _layer(x, st, lp, L, prefix=None):
    B, T, _ = x.shape
    widths = (RW_SHIFT_W, ML_MAIN_W, GATE_W)
    flat = lambda a: a.reshape(1, B * T, a.shape[-1])
    unflat = lambda a: a.reshape(B, T, a.shape[-1])
    xf = flat(x)
    tm = _row_tile(B * T)
    in_consts = (lp['pre1'], lp['w_rw'], lp['w_ml'], lp['w_if'])
    merge_consts = (lp['pre1'], lp['post1'], lp['w_gate'], lp['p_a'], lp['p_b'], lp['w_out'])
    ffn_consts = (lp['pre2'], lp['post2'], lp['w_ff_up'], lp['w_ff_down'])
    if prefix is None:
        z = _rowwise("in_proj", _in_proj_rows, (xf,), in_consts, widths, tm)
        ya, yb, new_state = _recurrences(*map(unflat, z), st, lp, L)
        h, = _rowwise("merge", _merge_rows, (xf, flat(ya), flat(yb)), merge_consts, (D_MODEL,), tm)
        out, = _rowwise("ffn", _ffn_rows, (h,), ffn_consts, (D_MODEL,), tm)
        return unflat(out), new_state
    P = prefix.shape[1]
    z = _rowwise("in_proj", _in_proj_rows, (xf,), in_consts, widths, tm, extras=(prefix,))
    ya_p, yb_p, st_p = _recurrences(*z[3:], None, lp, P)
    after_prefix = (0,) + tuple(jnp.broadcast_to(a, (1, B) + a.shape[1:]) for a in st_p)
    ya, yb, new_state = _recurrences(*map(unflat, z[:3]), after_prefix, lp, L)
    h, h_p = _rowwise("merge", _merge_rows, (xf, flat(ya), flat(yb)), merge_consts, (D_MODEL,), tm,
                      extras=(prefix, ya_p, yb_p))
    out, out_p = _rowwise("ffn", _ffn_rows, (h,), ffn_consts, (D_MODEL,), tm, extras=(h_p,))
    return unflat(out), out_p, new_state


def _layer_params(l, w_in, rw_mu, rw_w0, rw_w_up, rw_a0, rw_a_up, rw_g_up, rw_k_k, rw_k_a, rw_r_k, rw_gn_g,
                  rw_gn_b, ml_conv_w, ml_conv_b, ml_i_bias, ml_f_bias, ml_gn_g, p_a, p_b, w_out, pre1, post1,
                  pre2, post2, w_ff_up, w_ff_down):
    row = lambda a: a[l].reshape(1, -1).astype(F32)
    w = w_in[l]
    c_ml = RW_SHIFT_W
    c_if = c_ml + ML_MAIN_W
    c_gate = c_if + 2 * ML_HEADS
    half = RW_LORA_W // 2
    lora = jnp.zeros((RW_LORA_W, 2 * D_RW), F32)
    lora = lora.at[:half, :D_RW].set(rw_w_up[l]).at[half:, D_RW:].set(rw_a_up[l])
    lora_hi = lora.astype(BF16)
    lora_lo = (lora - lora_hi.astype(F32)).astype(BF16)
    lora = jnp.concatenate([lora_hi, lora_hi, lora_lo], axis=0)
    if_bias = jnp.zeros((1, GATE_W), F32)
    if_bias = if_bias.at[0, :ML_HEADS].set(ml_i_bias[l]).at[0, ML_HEADS:2 * ML_HEADS].set(ml_f_bias[l])
    return dict(
        pre1=row(pre1), post1=row(post1), pre2=row(pre2), post2=row(post2),
        w_rw=w[:, :c_ml].astype(BF16),
        w_ml=w[:, c_ml:c_if].astype(BF16),
        w_if=jnp.pad(w[:, c_if:c_gate], ((0, 0), (0, GATE_W - 2 * ML_HEADS))).astype(BF16),
        w_gate=w[:, c_gate:].astype(BF16),
        rw_params=(row(rw_mu), row(rw_w0), row(rw_a0), lora, rw_g_up[l].astype(BF16), row(rw_k_k), row(rw_k_a),
                   row(rw_r_k), row(rw_gn_g), row(rw_gn_b)),
        ml_params=(ml_conv_w[l].astype(F32), row(ml_conv_b), if_bias, row(ml_gn_g)),
        p_a=(p_a.astype(BF16), l), p_b=(p_b.astype(BF16), l), w_out=(w_out.astype(BF16), l),
        w_ff_up=(w_ff_up.astype(BF16), l), w_ff_down=(w_ff_down.astype(BF16), l),
    )


def kernel(x_prompt, x_sample, state_rwkv_S, state_rwkv_shift, state_mlstm_C, state_mlstm_n, state_mlstm_m,
           state_mlstm_conv, meta_tokens, w_in, rw_mu, rw_w0, rw_w_up, rw_a0, rw_a_up, rw_g_up, rw_k_k, rw_k_a,
           rw_r_k, rw_gn_g, rw_gn_b, ml_conv_w, ml_conv_b, ml_i_bias, ml_f_bias, ml_gn_g, p_a, p_b, w_out, pre1,
           post1, pre2, post2, w_ff_up, w_ff_down):
    dt = x_prompt.dtype
    depth = w_in.shape[0]
    xm = meta_tokens[None].astype(dt)
    xp = x_prompt
    xs = x_sample
    p_states, s_states = [], []
    for l in range(depth):
        lp = _layer_params(l, w_in, rw_mu, rw_w0, rw_w_up, rw_a0, rw_a_up, rw_g_up, rw_k_k, rw_k_a, rw_r_k,
                           rw_gn_g, rw_gn_b, ml_conv_w, ml_conv_b, ml_i_bias, ml_f_bias, ml_gn_g, p_a, p_b, w_out,
                           pre1, post1, pre2, post2, w_ff_up, w_ff_down)
        xp, xm, st_p = _layer(xp, None, lp, PROMPT_CHUNK, prefix=xm)
        st_in = (l, state_rwkv_S, state_rwkv_shift, state_mlstm_C, state_mlstm_n, state_mlstm_m, state_mlstm_conv)
        xs, st_s = _layer(xs, st_in, lp, xs.shape[1])
        p_states.append(st_p)
        s_states.append(st_s)
    stk = lambda lst, i: jnp.stack([s[i] for s in lst]).astype(dt)
    return (xp, xs,
            stk(p_states, 0), stk(p_states, 1), stk(p_states, 2), stk(p_states, 3), stk(p_states, 4), stk(p_states, 5),
            stk(s_states, 0), stk(s_states, 1), stk(s_states, 2), stk(s_states, 3), stk(s_states, 4), stk(s_states, 5))
```

```python
import functools
import math

import jax
import jax.numpy as jnp
from jax import lax
from jax.experimental import pallas as pl
from jax.experimental.pallas import tpu as pltpu

F32 = jnp.float32
BF16 = jnp.bfloat16

D_MODEL = 1024
N_META = 16
RW_HEADS = 8
RW_HD = 64
RW_HD_BITS = RW_HD.bit_length() - 1
D_RW = RW_HEADS * RW_HD
RW_HALF_HEADS = RW_HEADS // 2
RW_HALF = RW_HALF_HEADS * RW_HD
RW_LORA_W = 128
RW_G_LORA = 128
RW_SHIFT_W = 3 * D_RW + RW_LORA_W + RW_G_LORA
RW_GN_EPS = 64e-5
ML_HEADS = 4
ML_HD = 128
D_ML = ML_HEADS * ML_HD
CONV_W = 4
ML_MAIN_W = 4 * D_ML
ML_GN_EPS = 1e-5
GATE_W = 128
D_FF = 4 * D_MODEL
RMS_EPS = 1e-6
SUBLANES = 8

PROMPT_CHUNK = 64
VMEM_LIMIT = 56 * 2**20
RW_SEQS_LONG, RW_SEQS_SHORT = 8, 16
ML_SEQS_LONG, ML_SEQS_SHORT = 8, 16


def _bmm_nn(a, b):
    return jnp.einsum('gmk,gkn->gmn', a.astype(BF16), b.astype(BF16), preferred_element_type=F32)


def _bmm_nt(a, b):
    return jnp.einsum('gmk,gnk->gmn', a.astype(BF16), b.astype(BF16), preferred_element_type=F32)


def _bmm_tn(a, b):
    return jnp.einsum('gkm,gkn->gmn', a.astype(BF16), b.astype(BF16), preferred_element_type=F32)


def _neumann_half(n):
    L = n.shape[-1]
    t = n + (_iota((1, L, L), 1) == _iota((1, L, L), 2)).astype(F32)
    p = n
    for _ in range(max(L.bit_length() - 3, 0)):
        p = _bmm_nn(p, p)
        t = t + _bmm_nn(t, p)
    return t


def _split(x, terms):
    parts = []
    for _ in range(terms - 1):
        p = x.astype(BF16)
        parts.append(p)
        x = x - p.astype(F32)
    parts.append(x.astype(BF16))
    return parts


def _sel_right(x, sel, terms):
    pieces = _split(x, terms)
    return jnp.dot(jnp.concatenate(pieces, axis=1), jnp.concatenate([sel] * terms, axis=0),
                   preferred_element_type=F32)


def _cumsum_rows(x):
    row = _iota(x.shape, 1)
    s = 1
    while s < x.shape[1]:
        x = x + jnp.where(row >= s, pltpu.roll(x, s, 1), 0.0)
        s *= 2
    return x


def _sigmoid(x):
    return 0.5 * jnp.tanh(0.5 * x) + 0.5


def _iota(shape, dim):
    return lax.broadcasted_iota(jnp.int32, shape, dim)


def _rms(x, g):
    return x * lax.rsqrt(jnp.mean(x * x, axis=-1, keepdims=True) + RMS_EPS) * g


def _const_spec(shape):
    nd = len(shape)
    return pl.BlockSpec(shape, lambda *_: (0,) * nd, pipeline_mode=pl.Buffered(1))


def _resident(p):
    if not isinstance(p, tuple):
        return p, _const_spec(p.shape)
    stacked, layer = p
    nd = stacked.ndim - 1
    return stacked, pl.BlockSpec((None,) + stacked.shape[1:], lambda *_: (layer,) + (0,) * nd,
                                 pipeline_mode=pl.Buffered(1))


def _group(n, target):
    return max(d for d in range(1, target + 1) if n % d == 0)


def _rowwise_body(rows_fn, n_x, n_c, n_o, has_extra, *refs):
    refs = list(refs)
    take = lambda n: [refs.pop(0) for _ in range(n)]
    x_refs = take(n_x)
    e_refs = take(n_x) if has_extra else []
    c_refs = take(n_c)
    o_refs = take(n_o)
    eo_refs = take(n_o) if has_extra else []
    for o_ref, val in zip(o_refs, rows_fn(*[r[0] for r in x_refs], *c_refs)):
        o_ref[0] = val
    if has_extra:
        @pl.when(pl.program_id(1) == 0)
        def _():
            for o_ref, val in zip(eo_refs, rows_fn(*[r[0] for r in e_refs], *c_refs)):
                o_ref[0] = val


def _rowwise(name, rows_fn, xs, consts, out_widths, tm, extras=None):
    R = xs[0].shape[1]
    row = lambda w: pl.BlockSpec((1, tm, w), lambda b, i: (b, i, 0))
    consts, const_specs = zip(*map(_resident, consts))
    in_specs = [row(x.shape[2]) for x in xs]
    out_specs = [row(w) for w in out_widths]
    out_shape = [jax.ShapeDtypeStruct((1, R, w), F32) for w in out_widths]
    operands = list(xs)
    if extras is not None:
        E = extras[0].shape[1]
        whole = lambda w: pl.BlockSpec((1, E, w), lambda b, i: (0, 0, 0))
        in_specs += [whole(x.shape[2]) for x in extras]
        out_specs += [whole(w) for w in out_widths]
        out_shape += [jax.ShapeDtypeStruct((1, E, w), F32) for w in out_widths]
        operands += list(extras)
    return pl.pallas_call(
        functools.partial(_rowwise_body, rows_fn, len(xs), len(consts), len(out_widths), extras is not None),
        out_shape=tuple(out_shape),
        grid=(1, R // tm),
        in_specs=in_specs + list(const_specs),
        out_specs=tuple(out_specs),
        compiler_params=pltpu.CompilerParams(dimension_semantics=("arbitrary", "arbitrary"),
                                             vmem_limit_bytes=VMEM_LIMIT),
        name=name,
    )(*operands, *consts)


def _in_proj_rows(x, g_ref, wrw_ref, wml_ref, wif_ref):
    u = _rms(x, g_ref[...]).astype(BF16)
    return tuple(jnp.dot(u, w_ref[...], preferred_element_type=F32) for w_ref in (wrw_ref, wml_ref, wif_ref))


def _merge_rows(x, ya, yb, pre_ref, post_ref, wg_ref, pa_ref, pb_ref, wo_ref):
    u = _rms(x, pre_ref[...]).astype(BF16)
    gate = _sigmoid(jnp.dot(u, wg_ref[...], preferred_element_type=F32))
    pa = jnp.dot(ya.astype(BF16), pa_ref[...], preferred_element_type=F32)
    pb = jnp.dot(yb.astype(BF16), pb_ref[...], preferred_element_type=F32)
    merged = gate[:, :D_MODEL] * pa + gate[:, D_MODEL:] * pb
    o = jnp.dot(merged.astype(BF16), wo_ref[...], preferred_element_type=F32)
    return (x + _rms(o, post_ref[...]),)


FF_SPLIT = 4


def _ffn_rows(h, pre_ref, post_ref, wu_ref, wd_ref):
    u = _rms(h, pre_ref[...]).astype(BF16)
    step = D_FF // FF_SPLIT
    f = None
    for j in range(FF_SPLIT):
        t = jnp.maximum(jnp.dot(u, wu_ref[:, j * step:(j + 1) * step], preferred_element_type=F32), 0.0)
        part = jnp.dot((t * t).astype(BF16), wd_ref[j * step:(j + 1) * step, :], preferred_element_type=F32)
        f = part if f is None else f + part
    return (h + _rms(f, post_ref[...]),)


def _rwkv_chunk(L, at, bt, kt, rt, b_end, k_end, v, decay_end, S, bd32):
    H = RW_HALF_HEADS
    g = at.shape[0]
    lane_head = _iota((1, 1, RW_HALF), 2) >> RW_HD_BITS
    hmask = [(lane_head == h).astype(F32) for h in range(H)]
    if L % 16 == 0:
        hmask16 = [m.astype(BF16) for m in hmask]
        stack = lambda x: jnp.concatenate([x.astype(BF16) * hmask16[h] for h in range(H)], axis=1)
    else:
        stack = lambda x: jnp.concatenate([x * hmask[h] for h in range(H)], axis=1).astype(BF16)

    n = _bmm_nt(stack(at), bt).reshape(g * H, L, L)
    n = jnp.where(_iota((1, L, L), 1) > _iota((1, L, L), 2), n, 0.0)
    tinv = _neumann_half(n).reshape(g, H, L, L)

    v_stack = stack(v)
    ar = jnp.concatenate([at, rt], axis=1).astype(BF16)
    att = _bmm_nt(ar, jnp.concatenate([stack(bt), stack(kt)], axis=1))
    s_col = _iota((1, L, H * L), 2) & (L - 1)
    strict = s_col < _iota((1, L, H * L), 1)
    a_ab = jnp.where(strict, att[:, :L, :H * L], 0.0)
    a_ak = jnp.where(strict, att[:, :L, H * L:], 0.0)
    incl = (_iota((1, L, 2 * H * L), 2) & (L - 1)) <= _iota((1, L, 2 * H * L), 1)
    a_r = jnp.where(incl, att[:, L:, :], 0.0)

    from_state = _bmm_nt(ar, S)
    wmat = from_state[:, :L] + _bmm_nn(a_ak, v_stack)

    tcat = jnp.concatenate([tinv[:, h] for h in range(H)], axis=2).astype(BF16)

    def solve(rhs):
        return _bmm_nn(tcat, stack(rhs))

    u = solve(wmat)
    ab_hi, ab_lo = _split(a_ab, 2)
    u_hi, u_lo = _split(u, 2)
    u_hi_stack = stack(u_hi)
    nu = (jnp.einsum('gmk,gkn->gmn', jnp.concatenate([ab_hi, ab_lo], axis=2),
                     jnp.concatenate([u_hi_stack, u_hi_stack], axis=1), preferred_element_type=F32)
          + jnp.einsum('gmk,gkn->gmn', ab_hi, stack(u_lo), preferred_element_type=F32))
    u = u + solve(wmat - u + nu)
    y = from_state[:, L:] + _bmm_nn(a_r, jnp.concatenate([stack(u), v_stack], axis=1))
    upd = _bmm_tn(jnp.concatenate([u.astype(v.dtype), v], axis=1), jnp.concatenate([b_end, k_end], axis=1))
    return y, S * decay_end + upd * bd32


def _rwkv_body(L, nc, nb, has_state, z_ref, *refs):
    if has_state:
        sh0_ref, s0_ref, *refs = refs
    (mu_ref, w0_ref, a0_ref, wc_ref, gup_ref, kk_ref, ka_ref, rk_ref, gng_ref, gnb_ref, bd16_ref, bd32_ref,
     y_ref, sout_ref, shout_ref, s_scr, prev_scr) = refs
    c = pl.program_id(1)
    D, DH = D_RW, RW_HALF
    bd16 = bd16_ref[...]
    bd32 = bd32_ref[...]

    def head_sum(x, terms):
        m = x.shape[1]
        x2 = x.reshape(nb * m, D)
        r = jnp.concatenate([_sel_right(x2[:, hf * DH:(hf + 1) * DH], bd16, terms) for hf in range(2)], axis=1)
        return r.reshape(nb, m, D)

    halves = lambda x: jnp.stack([x[:, :, :DH], x[:, :, DH:]], axis=1).reshape(2 * nb, x.shape[1], DH)

    @pl.when(c == 0)
    def _():
        if has_state:
            prev_scr[...] = sh0_ref[...]
            s0 = s0_ref[...].reshape(2 * nb, DH, RW_HD)
            s_scr[...] = jnp.concatenate([s0] * RW_HALF_HEADS, axis=2) * bd32
        else:
            prev_scr[...] = jnp.zeros(prev_scr.shape, F32)
            s_scr[...] = jnp.zeros(s_scr.shape, F32)

    z = z_ref[...]
    prev = jnp.where(_iota(z.shape, 1) == 0, prev_scr[...], pltpu.roll(z, 1, 1))
    prev_scr[...] = z[:, L - 1:L, :]
    zs = z + (prev - z) * mu_ref[...]
    r = zs[:, :, 0:D]
    kraw = zs[:, :, D:2 * D]
    v = zs[:, :, 2 * D:3 * D]
    xl = zs[:, :, 3 * D:3 * D + RW_LORA_W]
    gl = zs[:, :, 3 * D + RW_LORA_W:]
    xl = jnp.where(_iota(xl.shape, 2) < RW_LORA_W // 2, jnp.tanh(xl), xl)
    xl_hi, xl_lo = _split(xl, 2)
    lora = jnp.dot(jnp.concatenate([xl_hi, xl_lo, xl_hi], axis=2).reshape(nb * L, 3 * RW_LORA_W), wc_ref[...],
                   preferred_element_type=F32).reshape(nb, L, 2 * D)
    logw = -math.exp(-0.5) * _sigmoid(w0_ref[...] + lora[:, :, :D])
    a = _sigmoid(a0_ref[...] + lora[:, :, D:])
    g = jnp.dot(_sigmoid(gl).reshape(nb * L, RW_G_LORA).astype(BF16), gup_ref[...],
                preferred_element_type=F32).reshape(nb, L, D)
    kk = kraw * kk_ref[...]
    k = kraw * (1.0 + (a - 1.0) * ka_ref[...])
    sums = head_sum(jnp.concatenate([kk * kk, r * k * rk_ref[...]], axis=1), 1)
    kk = kk * lax.rsqrt(jnp.maximum(sums[:, :L], 1e-24))
    bonus = sums[:, L:]

    cl = _cumsum_rows(logw)
    cl_last = cl[:, L - 1:L, :]
    e_neg = jnp.exp(-cl)
    e_end = jnp.exp(cl_last - cl)
    kka = kk * a
    narrow = (lambda x: x.astype(BF16)) if L % 16 == 0 else (lambda x: x)
    operands = [halves(narrow(o)) for o in (-kk * jnp.exp(cl - logw), kka * e_neg, k * e_neg, r * jnp.exp(cl),
                                            kka * e_end, k * e_end, v)]
    y_g, s_new = _rwkv_chunk(L, *operands, halves(jnp.exp(cl_last)), s_scr[...], bd32)
    s_scr[...] = s_new
    y_g = y_g.reshape(nb, 2, L, DH)
    y = jnp.concatenate([y_g[:, 0], y_g[:, 1]], axis=2)

    stats = head_sum(jnp.concatenate([y, y * y], axis=1), 2) * (1.0 / RW_HD)
    mean = stats[:, :L]
    var = jnp.maximum(stats[:, L:] - mean * mean, 0.0)
    yn = (y - mean) * lax.rsqrt(var + RW_GN_EPS) * gng_ref[...] + gnb_ref[...]
    y_ref[...] = (yn + bonus * v) * g

    @pl.when(c == nc - 1)
    def _():
        compact = sum(s_new[:, :, h * RW_HD:(h + 1) * RW_HD] for h in range(RW_HALF_HEADS))
        sout_ref[...] = compact.reshape(nb, D, RW_HD)
        shout_ref[...] = z[:, L - 1:L, :]


def _rwkv(z, state, params, L, nb):
    B, T, _ = z.shape
    nc = T // L
    per_b = lambda shp: pl.BlockSpec((nb,) + shp, lambda b, c: (b, 0, 0))
    assert T % L == 0 and L & (L - 1) == 0, "chunk length must be a power of two dividing the sequence"
    bd = (_iota((RW_HALF, RW_HALF), 0) >> RW_HD_BITS) == (_iota((RW_HALF, RW_HALF), 1) >> RW_HD_BITS)
    consts = tuple(params) + (bd.astype(BF16), bd.astype(F32))
    state_args, state_specs = (), []
    if state is not None:
        layer, shift0, s0 = state
        of_layer = lambda shp: pl.BlockSpec((None, nb) + shp, lambda b, c: (layer, b, 0, 0))
        state_args, state_specs = (shift0, s0), [of_layer((1, RW_SHIFT_W)), of_layer((D_RW, RW_HD))]
    return pl.pallas_call(
        functools.partial(_rwkv_body, L, nc, nb, state is not None),
        out_shape=(jax.ShapeDtypeStruct((B, T, D_RW), F32),
                   jax.ShapeDtypeStruct((B, D_RW, RW_HD), F32),
                   jax.ShapeDtypeStruct((B, 1, RW_SHIFT_W), F32)),
        grid=(B // nb, nc),
        in_specs=[pl.BlockSpec((nb, L, RW_SHIFT_W), lambda b, c: (b, c, 0))] + state_specs
                 + [_const_spec(p.shape) for p in consts],
        out_specs=(pl.BlockSpec((nb, L, D_RW), lambda b, c: (b, c, 0)),
                   per_b((D_RW, RW_HD)), per_b((1, RW_SHIFT_W))),
        scratch_shapes=[pltpu.VMEM((2 * nb, RW_HALF, RW_HALF), F32), pltpu.VMEM((nb, 1, RW_SHIFT_W), F32)],
        compiler_params=pltpu.CompilerParams(dimension_semantics=("arbitrary", "arbitrary"),
                                             vmem_limit_bytes=VMEM_LIMIT),
        name="rwkv",
    )(z, *state_args, *consts)


def _mlstm_body(L, nc, nb, has_state, zm_ref, zif_ref, *refs):
    if has_state:
        cv0_ref, c0_ref, nm0_ref, *refs = refs
    cw_ref, cb_ref, ifb_ref, gn_ref, y_ref, cout_ref, nmout_ref, cvout_ref, c_scr, nm_scr, cv_scr = refs
    c = pl.program_id(1)
    H, HD, D = ML_HEADS, ML_HD, D_ML
    G = nb * H

    @pl.when(c == 0)
    def _():
        if has_state:
            cv_scr[...] = cv0_ref[...]
            c_scr[...] = c0_ref[...]
            nm_scr[...] = nm0_ref[...]
        else:
            cv_scr[...] = jnp.zeros(cv_scr.shape, F32)
            c_scr[...] = jnp.zeros(c_scr.shape, F32)
            nm_scr[...] = jnp.zeros(nm_scr.shape, F32)

    heads = lambda x: jnp.stack([x[:, :, h * HD:(h + 1) * HD] for h in range(H)], axis=1).reshape(G, x.shape[1], HD)
    lane_pick = lambda x, lo: jnp.stack([x[:, :, lo + h:lo + h + 1] for h in range(H)], axis=1).reshape(G, x.shape[1], 1)
    row_pick = lambda x, lo: jnp.stack([x[:, lo + h:lo + h + 1, :] for h in range(H)], axis=1).reshape(G, 1, x.shape[2])

    zm = zm_ref[...]
    raw = zm[:, :, :2 * D]
    v = zm[:, :, 2 * D:3 * D]
    o = zm[:, :, 3 * D:]
    ext = jnp.concatenate([cv_scr[...], raw], axis=1)
    cv_scr[...] = ext[:, L:, :]
    cw = cw_ref[...]
    qk = cb_ref[...] + raw * cw[CONV_W - 1:CONV_W, :]
    for s in range(1, CONV_W):
        qk = qk + pltpu.roll(ext, s, 1)[:, SUBLANES:, :] * cw[CONV_W - 1 - s:CONV_W - s, :]
    qk = qk * _sigmoid(qk)
    qh = heads(qk[:, :, :D])
    kh = heads(qk[:, :, D:] * (HD ** -0.5))
    vh = heads(v)

    gi = zif_ref[...] + ifb_ref[...]
    lane = _iota(gi.shape, 2)
    lf = jnp.minimum(gi, 0.0) - jnp.log(1.0 + jnp.exp(-jnp.abs(gi)))
    gcol = jnp.where(lane < H, gi, jnp.where(lane < 2 * H, lf, 0.0))
    causal = _iota((1, L, L), 1) >= _iota((1, L, L), 2)
    b_col = _cumsum_rows(gcol)
    grow = jnp.swapaxes(gcol, 1, 2)
    b_row = jnp.swapaxes(b_col, 1, 2)
    bc, ic = lane_pick(b_col, H), lane_pick(gcol, 0)
    br, ir = row_pick(b_row, H), row_pick(grow, 0)

    nm = nm_scr[...]
    ch = c_scr[...].reshape(G, HD, HD)
    nh = row_pick(nm, 0)
    m_prev = lane_pick(nm[:, H:H + 1, :], 0)

    dlog = jnp.where(causal, bc - br + ir, -jnp.inf)
    inter = bc + m_prev
    m_t = jnp.maximum(inter, jnp.max(dlog, axis=-1, keepdims=True))
    amat = jnp.exp(dlog - m_t)
    sc = jnp.exp(inter - m_t)
    aqk = amat * _bmm_nt(qh, kh)
    num = _bmm_nn(aqk, vh) + sc * _bmm_nt(qh, ch)
    den = jnp.sum(aqk, axis=-1, keepdims=True) + sc * jnp.sum(qh * nh, axis=-1, keepdims=True)
    hh = num / jnp.maximum(jnp.abs(den), jnp.exp(-m_t))
    m_new = m_t[:, L - 1:L, :]
    b_last = bc[:, L - 1:L, :]
    wc = jnp.exp(b_last - bc + ic - m_new)
    dec = jnp.exp(b_last + m_prev - m_new)
    c_scr[...] = (dec * ch + _bmm_tn(vh * wc, kh)).reshape(nb, D, HD)
    n_new = (dec * nh + jnp.sum(kh * wc, axis=1, keepdims=True)).reshape(nb, H, 1, HD)
    m_new = m_new.reshape(nb, H, 1, 1)
    m_row = jnp.zeros((nb, 1, HD), F32)
    for h in range(H):
        m_row = jnp.where(_iota(m_row.shape, 2) == h, m_new[:, h], m_row)
    nm_scr[...] = jnp.concatenate([n_new[:, h] for h in range(H)] + [m_row, nm[:, H + 1:, :]], axis=1)

    mu = jnp.mean(hh, axis=-1, keepdims=True)
    xc = hh - mu
    var = jnp.mean(xc * xc, axis=-1, keepdims=True)
    gn = jnp.concatenate([gn_ref[:, h * HD:(h + 1) * HD][None] for h in range(H)] * nb, axis=0)
    out = (xc * lax.rsqrt(var + ML_GN_EPS) * gn * _sigmoid(heads(o))).reshape(nb, H, L, HD)
    y_ref[...] = jnp.concatenate([out[:, h] for h in range(H)], axis=2)

    @pl.when(c == nc - 1)
    def _():
        cout_ref[...] = c_scr[...]
        nmout_ref[...] = nm_scr[...]
        cvout_ref[...] = cv_scr[...]


def _mlstm(zm, zif, state, params, L, nb):
    B, T, _ = zm.shape
    assert T % L == 0 and L % SUBLANES == 0, "chunk length must be sublane aligned and divide the sequence"
    nc = T // L
    per_b = lambda shp: pl.BlockSpec((nb,) + shp, lambda b, c: (b, 0, 0))
    chunk = lambda w: pl.BlockSpec((nb, L, w), lambda b, c: (b, c, 0))
    state_shapes = ((SUBLANES, 2 * D_ML), (D_ML, ML_HD), (SUBLANES, ML_HD))
    state_args, state_specs = (), []
    if state is not None:
        layer, cv0, c0, nm0 = state
        state_args = (cv0, c0, nm0)
        state_specs = [per_b(state_shapes[0]),
                       pl.BlockSpec((None, nb) + state_shapes[1], lambda b, c: (layer, b, 0, 0)),
                       per_b(state_shapes[2])]
    return pl.pallas_call(
        functools.partial(_mlstm_body, L, nc, nb, state is not None),
        out_shape=(jax.ShapeDtypeStruct((B, T, D_ML), F32),
                   jax.ShapeDtypeStruct((B,) + state_shapes[1], F32),
                   jax.ShapeDtypeStruct((B,) + state_shapes[2], F32),
                   jax.ShapeDtypeStruct((B,) + state_shapes[0], F32)),
        grid=(B // nb, nc),
        in_specs=[chunk(ML_MAIN_W), chunk(GATE_W)] + state_specs + [_const_spec(p.shape) for p in params],
        out_specs=(chunk(D_ML), per_b(state_shapes[1]), per_b(state_shapes[2]), per_b(state_shapes[0])),
        scratch_shapes=[pltpu.VMEM((nb,) + state_shapes[1], F32), pltpu.VMEM((nb,) + state_shapes[2], F32),
                        pltpu.VMEM((nb,) + state_shapes[0], F32)],
        compiler_params=pltpu.CompilerParams(dimension_semantics=("arbitrary", "arbitrary"),
                                             vmem_limit_bytes=VMEM_LIMIT),
        name="mlstm",
    )(zm, zif, *state_args, *params)


MAX_ROW_TILE = 1024


def _row_tile(T):
    return max(tm for tm in range(SUBLANES, min(MAX_ROW_TILE, max(T // 2, SUBLANES)) + 1, SUBLANES) if T % tm == 0)


def _recurrences(z_rw, z_ml, z_if, st, lp, L):
    B, T, _ = z_rw.shape
    long_seq = T > L
    rw_state = ml_state = None
    if st is not None:
        l, S, sh, C, n, m, cb = st
        depth = S.shape[0]
        rw_state = (l, sh.reshape(depth, B, 1, RW_SHIFT_W), S.reshape(depth, B, D_RW, RW_HD))
        cv0 = jnp.pad(cb[l], ((0, 0), (SUBLANES - (CONV_W - 1), 0), (0, 0)))
        nm0 = jnp.concatenate([n[l], jnp.pad(m[l], ((0, 0), (0, ML_HD - ML_HEADS)))[:, None, :],
                               jnp.zeros((B, SUBLANES - ML_HEADS - 1, ML_HD), F32)], axis=1)
        ml_state = (l, cv0, C.reshape(depth, B, D_ML, ML_HD), nm0)
    ya, S1, sh1 = _rwkv(z_rw, rw_state, lp['rw_params'], L, _group(B, RW_SEQS_LONG if long_seq else RW_SEQS_SHORT))
    yb, C1, nm1, cv1 = _mlstm(z_ml, z_if, ml_state, lp['ml_params'], L,
                              _group(B, ML_SEQS_LONG if long_seq else ML_SEQS_SHORT))
    new_state = (S1.reshape(B, RW_HEADS, RW_HD, RW_HD), sh1[:, 0, :], C1.reshape(B, ML_HEADS, ML_HD, ML_HD),
                 nm1[:, :ML_HEADS, :], nm1[:, ML_HEADS, :ML_HEADS], cv1[:, SUBLANES - (CONV_W - 1):, :])
    return ya, yb, new_state


def _layer(x, st, lp, L, prefix=None):
    B, T, _ = x.shape
    widths = (RW_SHIFT_W, ML_MAIN_W, GATE_W)
    flat = lambda a: a.reshape(1, B * T, a.shape[-1])
    unflat = lambda a: a.reshape(B, T, a.shape[-1])
    xf = flat(x)
    tm = _row_tile(B * T)
    in_consts = (lp['pre1'], lp['w_rw'], lp['w_ml'], lp['w_if'])
    merge_consts = (lp['pre1'], lp['post1'], lp['w_gate'], lp['p_a'], lp['p_b'], lp['w_out'])
    ffn_consts = (lp['pre2'], lp['post2'], lp['w_ff_up'], lp['w_ff_down'])
    if prefix is None:
        z = _rowwise("in_proj", _in_proj_rows, (xf,), in_consts, widths, tm)
        ya, yb, new_state = _recurrences(*map(unflat, z), st, lp, L)
        h, = _rowwise("merge", _merge_rows, (xf, flat(ya), flat(yb)), merge_consts, (D_MODEL,), tm)
        out, = _rowwise("ffn", _ffn_rows, (h,), ffn_consts, (D_MODEL,), tm)
        return unflat(out), new_state
    P = prefix.shape[1]
    z = _rowwise("in_proj", _in_proj_rows, (xf,), in_consts, widths, tm, extras=(prefix,))
    ya_p, yb_p, st_p = _recurrences(*z[3:], None, lp, P)
    after_prefix = (0,) + tuple(jnp.broadcast_to(a, (1, B) + a.shape[1:]) for a in st_p)
    ya, yb, new_state = _recurrences(*map(unflat, z[:3]), after_prefix, lp, L)
    h, h_p = _rowwise("merge", _merge_rows, (xf, flat(ya), flat(yb)), merge_consts, (D_MODEL,), tm,
                      extras=(prefix, ya_p, yb_p))
    out, out_p = _rowwise("ffn", _ffn_rows, (h,), ffn_consts, (D_MODEL,), tm, extras=(h_p,))
    return unflat(out), out_p, new_state


def _layer_params(l, w_in, rw_mu, rw_w0, rw_w_up, rw_a0, rw_a_up, rw_g_up, rw_k_k, rw_k_a, rw_r_k, rw_gn_g,
                  rw_gn_b, ml_conv_w, ml_conv_b, ml_i_bias, ml_f_bias, ml_gn_g, p_a, p_b, w_out, pre1, post1,
                  pre2, post2, w_ff_up, w_ff_down):
    row = lambda a: a[l].reshape(1, -1).astype(F32)
    w = w_in[l]
    c_ml = RW_SHIFT_W
    c_if = c_ml + ML_MAIN_W
    c_gate = c_if + 2 * ML_HEADS
    half = RW_LORA_W // 2
    lora = jnp.zeros((RW_LORA_W, 2 * D_RW), F32)
    lora = lora.at[:half, :D_RW].set(rw_w_up[l]).at[half:, D_RW:].set(rw_a_up[l])
    lora_hi = lora.astype(BF16)
    lora_lo = (lora - lora_hi.astype(F32)).astype(BF16)
    lora = jnp.concatenate([lora_hi, lora_hi, lora_lo], axis=0)
    if_bias = jnp.zeros((1, GATE_W), F32)
    if_bias = if_bias.at[0, :ML_HEADS].set(ml_i_bias[l]).at[0, ML_HEADS:2 * ML_HEADS].set(ml_f_bias[l])
    return dict(
        pre1=row(pre1), post1=row(post1), pre2=row(pre2), post2=row(post2),
        w_rw=w[:, :c_ml].astype(BF16),
        w_ml=w[:, c_ml:c_if].astype(BF16),
        w_if=jnp.pad(w[:, c_if:c_gate], ((0, 0), (0, GATE_W - 2 * ML_HEADS))).astype(BF16),
        w_gate=w[:, c_gate:].astype(BF16),
        rw_params=(row(rw_mu), row(rw_w0), row(rw_a0), lora, rw_g_up[l].astype(BF16), row(rw_k_k), row(rw_k_a),
                   row(rw_r_k), row(rw_gn_g), row(rw_gn_b)),
        ml_params=(ml_conv_w[l].astype(F32), row(ml_conv_b), if_bias, row(ml_gn_g)),
        p_a=(p_a.astype(BF16), l), p_b=(p_b.astype(BF16), l), w_out=(w_out.astype(BF16), l),
        w_ff_up=(w_ff_up.astype(BF16), l), w_ff_down=(w_ff_down.astype(BF16), l),
    )


def kernel(x_prompt, x_sample, state_rwkv_S, state_rwkv_shift, state_mlstm_C, state_mlstm_n, state_mlstm_m,
           state_mlstm_conv, meta_tokens, w_in, rw_mu, rw_w0, rw_w_up, rw_a0, rw_a_up, rw_g_up, rw_k_k, rw_k_a,
           rw_r_k, rw_gn_g, rw_gn_b, ml_conv_w, ml_conv_b, ml_i_bias, ml_f_bias, ml_gn_g, p_a, p_b, w_out, pre1,
           post1, pre2, post2, w_ff_up, w_ff_down):
    dt = x_prompt.dtype
    depth = w_in.shape[0]
    xm = meta_tokens[None].astype(dt)
    xp = x_prompt
    xs = x_sample
    p_states, s_states = [], []
    for l in range(depth):
        lp = _layer_params(l, w_in, rw_mu, rw_w0, rw_w_up, rw_a0, rw_a_up, rw_g_up, rw_k_k, rw_k_a, rw_r_k,
                           rw_gn_g, rw_gn_b, ml_conv_w, ml_conv_b, ml_i_bias, ml_f_bias, ml_gn_g, p_a, p_b, w_out,
                           pre1, post1, pre2, post2, w_ff_up, w_ff_down)
        xp, xm, st_p = _layer(xp, None, lp, PROMPT_CHUNK, prefix=xm)
        st_in = (l, state_rwkv_S, state_rwkv_shift, state_mlstm_C, state_mlstm_n, state_mlstm_m, state_mlstm_conv)
        xs, st_s = _layer(xs, st_in, lp, xs.shape[1])
        p_states.append(st_p)
        s_states.append(st_s)
    stk = lambda lst, i: jnp.stack([s[i] for s in lst]).astype(dt)
    return (xp, xs,
            stk(p_states, 0), stk(p_states, 1), stk(p_states, 2), stk(p_states, 3), stk(p_states, 4), stk(p_states, 5),
            stk(s_states, 0), stk(s_states, 1), stk(s_states, 2), stk(s_states, 3), stk(s_states, 4), stk(s_states, 5))
```
